```python
import math
import numpy as np
import jax
import jax.numpy as jnp
from jax import lax


D_MODEL = 2048
BATCH = 8
SEQ = 2048
DEPTH = 2

MEM_LEN = 256
MIX_WIDTH = D_MODEL
MIX_GROUP_WIDTH = MIX_WIDTH // 4
N_MIXERS = 4
HEAD_DIM = 64
Q_BLOCK = 128
ROPE_THETA = 10000.0
LN_EPS = 1e-5
RMS_EPS = 1e-6
SSM_CH = 16
SSM_GROUPS = MIX_GROUP_WIDTH // SSM_CH
SSM_STATE = 64
NSA_HEADS = MIX_GROUP_WIDTH // HEAD_DIM
NSA_KV_HEADS = 2
NSA_REP = NSA_HEADS // NSA_KV_HEADS
CMP_LEN = 32
CMP_STRIDE = 16
CMP_HIDDEN = 128
SEL_BLOCK = 64
SEL_TOPN = 8
NSA_WINDOW = 512
SB_HEADS = MIX_GROUP_WIDTH // HEAD_DIM
DIL_HEADS = MIX_GROUP_WIDTH // HEAD_DIM
DIL_CONFIGS = ((128, 1), (512, 4), (2048, 16))
XA_HEADS = 4
XA_HEAD_DIM = 128
XA_WIDTH = XA_HEADS * XA_HEAD_DIM
FFN_HIDDEN = -(-8 * D_MODEL // (3 * 256)) * 256
DEEPNORM_ALPHA = (2 * DEPTH) ** 0.25
DEEPNORM_BETA = (8 * DEPTH) ** -0.25
SSM_WIDTH = SSM_GROUPS * SSM_CH
NSA_Q_WIDTH = NSA_HEADS * HEAD_DIM
NSA_KV_WIDTH = 6 * NSA_KV_HEADS * HEAD_DIM
NSA_GATE_WIDTH = 3 * NSA_HEADS
SB_QKV_WIDTH = 3 * SB_HEADS * HEAD_DIM
DIL_QKV_WIDTH = 3 * DIL_HEADS * HEAD_DIM
IN_WIDTHS = (SSM_WIDTH, NSA_Q_WIDTH, NSA_KV_WIDTH, NSA_GATE_WIDTH, SB_QKV_WIDTH, DIL_QKV_WIDTH)
IN_WIDTH = SSM_WIDTH + NSA_Q_WIDTH + NSA_KV_WIDTH + NSA_GATE_WIDTH + SB_QKV_WIDTH + DIL_QKV_WIDTH

kernel_name = 'hybrid_s5_nsa_stickbreak_dilated_deepnorm'

F32 = jnp.float32


def layer_norm(x, g, b):
    xf = x.astype(F32)
    mu = jnp.mean(xf, -1, keepdims=True)
    xc = xf - mu
    var = jnp.mean(xc * xc, -1, keepdims=True)
    return (xc * lax.rsqrt(var + LN_EPS) * g + b).astype(x.dtype)


def rms_norm(x, g):
    xf = x.astype(F32)
    return (xf * lax.rsqrt(jnp.mean(xf * xf, -1, keepdims=True) + RMS_EPS) * g).astype(x.dtype)


def rope(x, pos):
    half = x.shape[-1] // 2
    inv_freq = ROPE_THETA ** (-jnp.arange(half, dtype=F32) / half)
    ang = pos.astype(F32)[:, None] * inv_freq[None, :]
    cos = jnp.cos(ang)[:, None, :]
    sin = jnp.sin(ang)[:, None, :]
    x1 = x[..., :half].astype(F32)
    x2 = x[..., half:].astype(F32)
    return jnp.concatenate([x1 * cos - x2 * sin, x2 * cos + x1 * sin], -1).astype(x.dtype)


def masked_softmax(s, mask, axis):
    s = jnp.where(mask, s, -jnp.inf)
    m = jnp.max(s, axis=axis, keepdims=True)
    m = jnp.where(jnp.isfinite(m), m, 0.0)
    p = jnp.where(mask, jnp.exp(s - m), 0.0)
    den = jnp.sum(p, axis=axis, keepdims=True)
    return p / jnp.maximum(den, 1e-30), m + jnp.log(den)


def _s5_combine(e1, e2):
    a1r, a1i, b1r, b1i = e1
    a2r, a2i, b2r, b2i = e2
    return (a2r * a1r - a2i * a1i, a2r * a1i + a2i * a1r,
            a2r * b1r - a2i * b1i + b2r, a2r * b1i + a2i * b1r + b2i)


def s5_mixer(u, lam_re, lam_im, log_dt, b_re, b_im, c_re, c_im, d_skip, w_glu, b_glu):
    Bsz, S, _ = u.shape
    uf = u.astype(F32).reshape(Bsz, S, SSM_GROUPS, SSM_CH)
    lr = jnp.minimum(lam_re.astype(F32), -1e-4)
    li = lam_im.astype(F32)
    dt = jnp.exp(log_dt.astype(F32))[:, None]
    mag = jnp.exp(lr * dt)
    a_re = mag * jnp.cos(li * dt)
    a_im = mag * jnp.sin(li * dt)
    den = lr * lr + li * li
    z_re = ((a_re - 1.0) * lr + a_im * li) / den
    z_im = (a_im * lr - (a_re - 1.0) * li) / den
    bb_re = z_re[..., None] * b_re - z_im[..., None] * b_im
    bb_im = z_re[..., None] * b_im + z_im[..., None] * b_re
    bu_re = jnp.einsum('bsgc,gpc->bsgp', uf, bb_re)
    bu_im = jnp.einsum('bsgc,gpc->bsgp', uf, bb_im)
    shape = bu_re.shape
    elems = (jnp.broadcast_to(a_re, shape), jnp.broadcast_to(a_im, shape), bu_re, bu_im)
    _, _, x_re, x_im = lax.associative_scan(_s5_combine, elems, axis=1)
    y = (jnp.einsum('bsgp,gcp->bsgc', x_re, c_re) - jnp.einsum('bsgp,gcp->bsgc', x_im, c_im)
         + d_skip * uf)
    g = jax.nn.gelu(y.reshape(Bsz, S, SSM_WIDTH))
    return g * jax.nn.sigmoid(g @ w_glu + b_glu)


def nsa_mixer(q, k_cmp, v_cmp, k_slc, v_slc, k_win, v_win, gates, cmp_pe, cmp_w1, cmp_w2):
    Bsz, S = q.shape[:2]
    G, R, dh = NSA_KV_HEADS, NSA_REP, HEAD_DIM
    scale = dh ** -0.5
    n_cmp = (S - CMP_LEN) // CMP_STRIDE + 1
    blk_idx = jnp.arange(n_cmp)[:, None] * CMP_STRIDE + jnp.arange(CMP_LEN)[None, :]
    cmp_end = jnp.arange(n_cmp) * CMP_STRIDE + CMP_LEN - 1

    def compress(t, j):
        blocks = t[:, blk_idx] + cmp_pe[j][:, None, :]
        flat = blocks.transpose(0, 1, 3, 2, 4).reshape(Bsz, n_cmp, G, CMP_LEN * dh)
        return jax.nn.gelu(flat @ cmp_w1[j]) @ cmp_w2[j]

    kc = rope(compress(k_cmp, 0), cmp_end)
    vc = compress(v_cmp, 1)

    n_sel = S // SEL_BLOCK
    top_n = min(SEL_TOPN, n_sel)
    ci = np.arange(n_cmp)[:, None] * CMP_STRIDE
    sj = np.arange(n_sel)[None, :] * SEL_BLOCK
    overlap = np.clip(np.minimum(ci + CMP_LEN, sj + SEL_BLOCK) - np.maximum(ci, sj), 0, None) / CMP_LEN
    overlap = jnp.asarray(overlap, dtype=F32)
    ks_blocks = k_slc.reshape(Bsz, n_sel, SEL_BLOCK, G, dh).transpose(0, 3, 1, 2, 4)
    vs_blocks = v_slc.reshape(Bsz, n_sel, SEL_BLOCK, G, dh).transpose(0, 3, 1, 2, 4)
    pad = jnp.zeros((Bsz, NSA_WINDOW, G, dh), k_win.dtype)
    kw_pad = jnp.concatenate([pad, k_win], 1)
    vw_pad = jnp.concatenate([pad, v_win], 1)
    n_qb = S // Q_BLOCK
    qb = q.reshape(Bsz, n_qb, Q_BLOCK, G, R, dh).transpose(1, 0, 2, 3, 4, 5)
    gb = gates.reshape(Bsz, n_qb, Q_BLOCK, G, R, 3).transpose(1, 0, 2, 3, 4, 5)
    b_ix = jnp.arange(Bsz)[:, None, None, None]
    g_ix = jnp.arange(G)[None, :, None, None]
    sel_j = jnp.arange(n_sel)

    def block(args):
        qi, g_blk, blk = args
        t = blk * Q_BLOCK + jnp.arange(Q_BLOCK)
        s_c = jnp.einsum('bqgrd,bcgd->bgrqc', qi, kc).astype(F32) * scale
        p_c, _ = masked_softmax(s_c, cmp_end[None, :] <= t[:, None], -1)
        o_c = jnp.einsum('bgrqc,bcgd->bqgrd', p_c.astype(vc.dtype), vc)
        imp = jnp.einsum('bgrqc,cn->bgqn', p_c, overlap)
        cur = (t // SEL_BLOCK)[:, None]
        imp = jnp.where(sel_j[None, :] * SEL_BLOCK > t[:, None], -jnp.inf, imp)
        forced = (sel_j[None, :] == 0) | (sel_j[None, :] == cur) | (sel_j[None, :] == cur - 1)
        imp = jnp.where(forced, jnp.inf, imp)
        top_val, top_idx = lax.top_k(imp, top_n)
        ks = ks_blocks[b_ix, g_ix, top_idx]
        vs = vs_blocks[b_ix, g_ix, top_idx]
        s_s = jnp.einsum('bqgrd,bgqnkd->bgrqnk', qi, ks).astype(F32) * scale
        key_pos = top_idx[..., None] * SEL_BLOCK + jnp.arange(SEL_BLOCK)
        m_s = (top_val > -jnp.inf)[..., None] & (key_pos <= t[:, None, None])
        p_s, _ = masked_softmax(s_s, m_s[:, :, None], (-2, -1))
        o_s = jnp.einsum('bgrqnk,bgqnkd->bqgrd', p_s.astype(vs.dtype), vs)
        kw = lax.dynamic_slice_in_dim(kw_pad, blk * Q_BLOCK, Q_BLOCK + NSA_WINDOW, axis=1)
        vw = lax.dynamic_slice_in_dim(vw_pad, blk * Q_BLOCK, Q_BLOCK + NSA_WINDOW, axis=1)
        s_w = jnp.einsum('bqgrd,bkgd->bgrqk', qi, kw).astype(F32) * scale
        kpos = blk * Q_BLOCK - NSA_WINDOW + jnp.arange(Q_BLOCK + NSA_WINDOW)
        diff = t[:, None] - kpos[None, :]
        m_w = (kpos[None, :] >= 0) & (diff >= 0) & (diff < NSA_WINDOW)
        p_w, _ = masked_softmax(s_w, m_w, -1)
        o_w = jnp.einsum('bgrqk,bkgd->bqgrd', p_w.astype(vw.dtype), vw)
        return g_blk[..., 0:1] * o_c + g_blk[..., 1:2] * o_s + g_blk[..., 2:3] * o_w

    out = lax.map(block, (qb, gb, jnp.arange(n_qb)))
    return out.transpose(1, 0, 2, 3, 4, 5).reshape(Bsz, S, NSA_HEADS * dh)


def stick_breaking_mixer(q, k, v):
    Bsz, S, H, dh = q.shape
    n_qb = S // Q_BLOCK
    qb = q.reshape(Bsz, n_qb, Q_BLOCK, H, dh).transpose(1, 0, 2, 3, 4)
    kpos = jnp.arange(S)

    def block(args):
        qi, blk = args
        t = blk * Q_BLOCK + jnp.arange(Q_BLOCK)
        z = jnp.einsum('bqhd,bkhd->bhqk', qi, k).astype(F32) * dh ** -0.5
        mask = kpos[None, :] < t[:, None]
        log_fail = jnp.where(mask, jax.nn.log_sigmoid(-z), 0.0)
        after = lax.cumsum(log_fail, axis=3, reverse=True) - log_fail
        w = jnp.where(mask, jnp.exp(jax.nn.log_sigmoid(z) + after), 0.0)
        return jnp.einsum('bhqk,bkhd->bqhd', w.astype(v.dtype), v)

    out = lax.map(block, (qb, jnp.arange(n_qb)))
    return out.transpose(1, 0, 2, 3, 4).reshape(Bsz, S, H * dh)


def dilated_branch(q, k, v, window, dil):
    Bsz, S, H, dh = q.shape
    L = S // dil
    wd = window // dil
    qlen = Q_BLOCK // dil
    n_b = L // qlen

    def sub(a):
        return a.reshape(Bsz, L, dil, H, dh)

    qs = sub(q).reshape(Bsz, n_b, qlen, dil, H, dh).transpose(1, 0, 2, 3, 4, 5)
    pad = jnp.zeros((Bsz, wd, dil, H, dh), k.dtype)
    ks = jnp.concatenate([pad, sub(k)], 1)
    vs = jnp.concatenate([pad, sub(v)], 1)
    i = jnp.arange(qlen)[:, None]
    j = jnp.arange(qlen + wd)[None, :]
    band = (j >= i) & (j <= i + wd)

    def block(args):
        qi, blk = args
        kb = lax.dynamic_slice_in_dim(ks, blk * qlen, qlen + wd, axis=1)
        vb = lax.dynamic_slice_in_dim(vs, blk * qlen, qlen + wd, axis=1)
        s = jnp.einsum('bqrhd,bkrhd->brhqk', qi, kb).astype(F32) * dh ** -0.5
        valid = band & (blk * qlen + j - wd >= 0)
        p, lse = masked_softmax(s, valid, -1)
        o = jnp.einsum('brhqk,bkrhd->bqrhd', p.astype(vb.dtype), vb)
        return o, lse[..., 0]

    o, lse = lax.map(block, (qs, jnp.arange(n_b)))
    o = o.transpose(1, 0, 2, 3, 4, 5).reshape(Bsz, S, H, dh)
    lse = lse.transpose(1, 0, 4, 2, 3).reshape(Bsz, S, H)
    return o, lse


def dilated_mixer(q, k, v):
    Bsz, S, H, dh = q.shape
    outs, lses = [], []
    for window, dil in DIL_CONFIGS:
        o, lse = dilated_branch(q, k, v, window, dil)
        outs.append(o.astype(F32))
        lses.append(lse)
    wts = jax.nn.softmax(jnp.stack(lses, 0), axis=0)
    out = jnp.einsum('cbsh,cbshd->bshd', wts, jnp.stack(outs, 0))
    return out.reshape(Bsz, S, H * dh)


def hybrid_mixer(h, w_in, lam_re, lam_im, log_dt, b_re, b_im, c_re, c_im, d_skip, w_glu, b_glu,
                 cmp_pe, cmp_w1, cmp_w2, norm_g, w_out):
    Bsz, S, _ = h.shape
    proj = h @ w_in
    offs = np.cumsum(IN_WIDTHS)[:-1].tolist()
    u, nq, nkv, ngate, sb_qkv, dil_qkv = jnp.split(proj, offs, axis=-1)
    pos = jnp.arange(S)
    y_a = s5_mixer(u, lam_re, lam_im, log_dt, b_re, b_im, c_re, c_im, d_skip, w_glu, b_glu)
    q = rope(nq.reshape(Bsz, S, NSA_HEADS, HEAD_DIM), pos)
    kv = nkv.reshape(Bsz, S, 6, NSA_KV_HEADS, HEAD_DIM)
    gates = jax.nn.sigmoid(ngate.astype(F32)).reshape(Bsz, S, NSA_HEADS, 3)
    y_b = nsa_mixer(q, kv[:, :, 0], kv[:, :, 1], rope(kv[:, :, 2], pos), kv[:, :, 3],
                    rope(kv[:, :, 4], pos), kv[:, :, 5], gates, cmp_pe, cmp_w1, cmp_w2)
    sqkv = sb_qkv.reshape(Bsz, S, 3, SB_HEADS, HEAD_DIM)
    y_c = stick_breaking_mixer(sqkv[:, :, 0], sqkv[:, :, 1], sqkv[:, :, 2])
    dqkv = dil_qkv.reshape(Bsz, S, 3, DIL_HEADS, HEAD_DIM)
    y_d = dilated_mixer(rope(dqkv[:, :, 0], pos), rope(dqkv[:, :, 1], pos), dqkv[:, :, 2])
    ys = jnp.stack([y_a.astype(F32), y_b.astype(F32), y_c.astype(F32), y_d.astype(F32)], axis=2)
    ys = rms_norm(ys, norm_g)
    return ys.reshape(Bsz, S, MIX_WIDTH).astype(h.dtype) @ w_out


def memory_cross_attention(h, mem, wq, wkv, wo):
    Bsz, S, _ = h.shape
    M = mem.shape[1]
    q = (h @ wq).reshape(Bsz, S, XA_HEADS, XA_HEAD_DIM)
    kv = (mem @ wkv).reshape(Bsz, M, 2, XA_HEADS, XA_HEAD_DIM)
    s = jnp.einsum('bshd,bmhd->bhsm', q, kv[:, :, 0]).astype(F32) * XA_HEAD_DIM ** -0.5
    p = jax.nn.softmax(s, axis=-1)
    o = jnp.einsum('bhsm,bmhd->bshd', p.astype(kv.dtype), kv[:, :, 1])
    return o.reshape(Bsz, S, XA_WIDTH) @ wo


def swiglu_ffn(h, wg, wu, wd):
    return (jax.nn.silu(h @ wg) * (h @ wu)) @ wd


def setup_inputs(seed: int = 0) -> dict:
    key = jax.random.key(seed)
    keys = iter(jax.random.split(key, 40))

    def nrm(shape, scale):
        return jax.random.normal(next(keys), shape, F32) * scale

    L, D = DEPTH, D_MODEL
    G, P, C = SSM_GROUPS, SSM_STATE, SSM_CH
    return {
        'x': nrm((BATCH, SEQ, D), 1.0),
        'mem': nrm((BATCH, MEM_LEN, D), 1.0),
        'ln_in_g': 1.0 + nrm((D,), 0.01),
        'ln_in_b': nrm((D,), 0.01),
        'w_in': nrm((L, D, IN_WIDTH), D ** -0.5),
        's5_lambda_re': -0.5 + nrm((L, G, P), 0.01),
        's5_lambda_im': math.pi * jnp.arange(P, dtype=F32) + nrm((L, G, P), 0.01),
        's5_log_dt': jax.random.uniform(next(keys), (L, G), F32, math.log(1e-3), math.log(1e-1)),
        's5_b_re': nrm((L, G, P, C), (2 * C) ** -0.5),
        's5_b_im': nrm((L, G, P, C), (2 * C) ** -0.5),
        's5_c_re': nrm((L, G, C, P), (2 * P) ** -0.5),
        's5_c_im': nrm((L, G, C, P), (2 * P) ** -0.5),
        's5_d': nrm((L, G, C), 1.0),
        's5_w_glu': nrm((L, SSM_WIDTH, SSM_WIDTH), SSM_WIDTH ** -0.5),
        's5_b_glu': nrm((L, SSM_WIDTH), 0.01),
        'nsa_cmp_pe': nrm((L, 2, CMP_LEN, HEAD_DIM), 0.02),
        'nsa_cmp_w1': nrm((L, 2, CMP_LEN * HEAD_DIM, CMP_HIDDEN), (CMP_LEN * HEAD_DIM) ** -0.5),
        'nsa_cmp_w2': nrm((L, 2, CMP_HIDDEN, HEAD_DIM), CMP_HIDDEN ** -0.5),
        'mix_norm_g': 1.0 + nrm((L, N_MIXERS, MIX_GROUP_WIDTH), 0.01),
        'w_out': nrm((L, MIX_WIDTH, D), MIX_WIDTH ** -0.5 * DEEPNORM_BETA),
        'ln1_g': 1.0 + nrm((L, D), 0.01),
        'ln1_b': nrm((L, D), 0.01),
        'xa_wq': nrm((L, D, XA_WIDTH), D ** -0.5),
        'xa_wkv': nrm((L, D, 2 * XA_WIDTH), D ** -0.5),
        'xa_wo': nrm((L, XA_WIDTH, D), XA_WIDTH ** -0.5 * DEEPNORM_BETA),
        'ln2_g': 1.0 + nrm((L, D), 0.01),
        'ln2_b': nrm((L, D), 0.01),
        'ffn_w_gate': nrm((L, D, FFN_HIDDEN), D ** -0.5),
        'ffn_w_up': nrm((L, D, FFN_HIDDEN), D ** -0.5),
        'ffn_w_down': nrm((L, FFN_HIDDEN, D), FFN_HIDDEN ** -0.5 * DEEPNORM_BETA),
        'ln3_g': 1.0 + nrm((L, D), 0.01),
        'ln3_b': nrm((L, D), 0.01),
    }


def reference(x, mem, ln_in_g, ln_in_b, w_in, s5_lambda_re, s5_lambda_im, s5_log_dt, s5_b_re, s5_b_im,
              s5_c_re, s5_c_im, s5_d, s5_w_glu, s5_b_glu, nsa_cmp_pe, nsa_cmp_w1, nsa_cmp_w2, mix_norm_g,
              w_out, ln1_g, ln1_b, xa_wq, xa_wkv, xa_wo, ln2_g, ln2_b, ffn_w_gate, ffn_w_up, ffn_w_down,
              ln3_g, ln3_b):
    h = layer_norm(x, ln_in_g, ln_in_b)
    for l in range(DEPTH):
        f = hybrid_mixer(h, w_in[l], s5_lambda_re[l], s5_lambda_im[l], s5_log_dt[l], s5_b_re[l], s5_b_im[l],
                         s5_c_re[l], s5_c_im[l], s5_d[l], s5_w_glu[l], s5_b_glu[l], nsa_cmp_pe[l],
                         nsa_cmp_w1[l], nsa_cmp_w2[l], mix_norm_g[l], w_out[l])
        h = layer_norm(DEEPNORM_ALPHA * h + f, ln1_g[l], ln1_b[l])
        f = memory_cross_attention(h, mem, xa_wq[l], xa_wkv[l], xa_wo[l])
        h = layer_norm(DEEPNORM_ALPHA * h + f, ln2_g[l], ln2_b[l])
        f = swiglu_ffn(h, ffn_w_gate[l], ffn_w_up[l], ffn_w_down[l])
        h = layer_norm(DEEPNORM_ALPHA * h + f, ln3_g[l], ln3_b[l])
    return h
```

```python
import functools
import math

import numpy as np
import jax
import jax.numpy as jnp
from jax import lax
from jax.experimental import pallas as pl
from jax.experimental.pallas import tpu as pltpu

F32 = jnp.float32
BF16 = jnp.bfloat16

LANES = 128
VMEM_LIMIT = 56 * 1024 * 1024

HEAD_DIM = 64
HALF = HEAD_DIM // 2
ROPE_THETA = 10000.0
LN_EPS = 1e-5
RMS_EPS = 1e-6
SSM_CH = 16
SSM_STATE = 64
NSA_KV_HEADS = 2
NSA_REP = 4
CMP_LEN = 32
CMP_STRIDE = 16
CMP_HIDDEN = 128
SEL_BLOCK = 64
SEL_TOPN = 8
NSA_WINDOW = 512
DIL_CONFIGS = ((128, 1), (512, 4), (2048, 16))
XA_HEADS = 4
XA_HEAD_DIM = 128
Q_TILE = 128
NEG_BIG = -1e30


def _params(*sem):
    return pltpu.CompilerParams(dimension_semantics=sem, vmem_limit_bytes=VMEM_LIMIT)


def _dot(a, b):
    return jnp.dot(a.astype(BF16), b.astype(BF16), preferred_element_type=F32)


def _dot_nt(a, b):
    return lax.dot_general(a.astype(BF16), b.astype(BF16), (((1,), (1,)), ((), ())),
                           preferred_element_type=F32)


def _dot_split(a, b_exact):
    hi = a.astype(BF16)
    lo = (a - hi.astype(F32)).astype(BF16)
    return (jnp.dot(hi, b_exact, preferred_element_type=F32)
            + jnp.dot(lo, b_exact, preferred_element_type=F32))


def _layer_norm(x, g, b):
    mu = jnp.mean(x, -1, keepdims=True)
    xc = x - mu
    var = jnp.mean(xc * xc, -1, keepdims=True)
    return xc * lax.rsqrt(var + LN_EPS) * g + b


def _lane_iota(shape):
    return lax.broadcasted_iota(jnp.int32, shape, len(shape) - 1)


def _row_iota(shape):
    return lax.broadcasted_iota(jnp.int32, shape, len(shape) - 2)


def _swap_halves(x):
    return pltpu.roll(x, HEAD_DIM, axis=x.ndim - 1)


def _ln_in_kernel(x_ref, g_ref, b_ref, h_ref, hb_ref):
    y = _layer_norm(x_ref[...], g_ref[...], b_ref[...])
    h_ref[...] = y
    hb_ref[...] = y.astype(BF16)


def _ln_in(x2, g, b, tm):
    T, D = x2.shape
    return pl.pallas_call(
        _ln_in_kernel,
        grid=(T // tm,),
        in_specs=[pl.BlockSpec((tm, D), lambda i: (i, 0)),
                  pl.BlockSpec((1, D), lambda i: (0, 0)),
                  pl.BlockSpec((1, D), lambda i: (0, 0))],
        out_specs=[pl.BlockSpec((tm, D), lambda i: (i, 0)),
                   pl.BlockSpec((tm, D), lambda i: (i, 0))],
        out_shape=[jax.ShapeDtypeStruct((T, D), F32), jax.ShapeDtypeStruct((T, D), BF16)],
        compiler_params=_params("parallel"),
        name="ln_in",
    )(x2, g.reshape(1, D), b.reshape(1, D))


def _proj_kernel(a_ref, w_ref, o_ref):
    o_ref[...] = jnp.dot(a_ref[...], w_ref[...], preferred_element_type=F32)


def _proj_rope_kernel(a_ref, w_ref, cos_ref, sin_ref, o_ref):
    acc = jnp.dot(a_ref[...], w_ref[...], preferred_element_type=F32)
    cos = cos_ref[...]
    sin = sin_ref[...]
    first = (_lane_iota((1, LANES)) % HEAD_DIM) < HALF
    for c in range(acc.shape[1] // LANES):
        x = acc[:, c * LANES:(c + 1) * LANES]
        partner = jnp.where(first, pltpu.roll(x, LANES - HALF, axis=1), pltpu.roll(x, HALF, axis=1))
        o_ref[:, c * LANES:(c + 1) * LANES] = x * cos + partner * sin


def _proj(a, w, tm, tn, seq=None, rope=None, time_major_batch=None):
    M, K = a.shape
    N = w.shape[1]
    nm, nn = M // tm, N // tn
    in_specs = [pl.BlockSpec((tm, K), lambda j, i: (i, 0)),
                pl.BlockSpec((K, tn), lambda j, i: (0, j))]
    args = [a, w]
    kern = _proj_kernel
    if rope is not None:
        ns = seq // tm
        in_specs += [pl.BlockSpec((tm, LANES), lambda j, i: (i % ns, 0))] * 2
        args += list(rope)
        kern = _proj_rope_kernel
    if time_major_batch is None:
        out_spec = pl.BlockSpec((tm, tn), lambda j, i: (i, j))
        out_shape = jax.ShapeDtypeStruct((M, N), F32)
    else:
        assert nn == 1
        ns = seq // tm
        out_spec = pl.BlockSpec((tm, N), lambda j, i: (i % ns, i // ns))
        out_shape = jax.ShapeDtypeStruct((seq, time_major_batch * N), F32)
    out = pl.pallas_call(
        kern, grid=(nn, nm), in_specs=in_specs, out_specs=out_spec, out_shape=out_shape,
        compiler_params=_params("parallel", "parallel"), name="proj",
    )(*args)
    if time_major_batch is not None:
        out = out.reshape(seq * time_major_batch, N)
    return out


def _s5_kernel(u_ref, bb_ref, cc_ref, are_ref, aim_ref, d_ref, wg_ref, bg_ref, y_ref,
               buf_ref, st_ref, *, nb, ts, lane_chunk):
    nstate = are_ref.shape[1]

    @pl.when(pl.program_id(0) == 0)
    def _():
        st_ref[...] = jnp.zeros_like(st_ref)

    u = u_ref[...]
    buf_ref[...] = jnp.dot(u.astype(BF16), bb_ref[...], preferred_element_type=F32)

    for c in range(nstate // lane_chunk):
        lo = c * lane_chunk
        a_re = jnp.broadcast_to(are_ref[:, lo:lo + lane_chunk], (nb, lane_chunk))
        a_im = jnp.broadcast_to(aim_ref[:, lo:lo + lane_chunk], (nb, lane_chunk))

        def step(t, carry):
            x_re, x_im = carry
            r = pl.multiple_of(t * nb, nb)
            b_re = buf_ref[pl.ds(r, nb), lo:lo + lane_chunk]
            b_im = buf_ref[pl.ds(r, nb), nstate + lo:nstate + lo + lane_chunk]
            n_re = a_re * x_re - a_im * x_im + b_re
            n_im = a_re * x_im + a_im * x_re + b_im
            buf_ref[pl.ds(r, nb), lo:lo + lane_chunk] = n_re
            buf_ref[pl.ds(r, nb), nstate + lo:nstate + lo + lane_chunk] = n_im
            return n_re, n_im

        x_re, x_im = lax.fori_loop(
            0, ts, step,
            (st_ref[:, lo:lo + lane_chunk], st_ref[:, nstate + lo:nstate + lo + lane_chunk]),
            unroll=8)
        st_ref[:, lo:lo + lane_chunk] = x_re
        st_ref[:, nstate + lo:nstate + lo + lane_chunk] = x_im

    y = jnp.dot(buf_ref[...].astype(BF16), cc_ref[...], preferred_element_type=F32) + d_ref[...] * u
    g = jax.nn.gelu(y)
    z = jnp.dot(g.astype(BF16), wg_ref[...], preferred_element_type=F32) + bg_ref[...]
    y_ref[...] = g * jax.nn.sigmoid(z)


def _s5(u_tm, nb, lam_re, lam_im, log_dt, b_re, b_im, c_re, c_im, d_skip, w_glu, b_glu, ts=64):
    R, W = u_tm.shape
    G, P = lam_re.shape
    C = SSM_CH
    lr = jnp.minimum(lam_re, -1e-4)
    li = lam_im
    dt = jnp.exp(log_dt)[:, None]
    mag = jnp.exp(lr * dt)
    a_re = mag * jnp.cos(li * dt)
    a_im = mag * jnp.sin(li * dt)
    den = lr * lr + li * li
    z_re = ((a_re - 1.0) * lr + a_im * li) / den
    z_im = (a_im * lr - (a_re - 1.0) * li) / den
    bb_re = z_re[..., None] * b_re - z_im[..., None] * b_im
    bb_im = z_re[..., None] * b_im + z_im[..., None] * b_re
    eye = jnp.eye(G, dtype=F32)

    def block_diag_in(m):
        return jnp.einsum('gpc,gh->gchp', m, eye).reshape(G * C, G * P)

    def block_diag_out(m):
        return jnp.einsum('gcp,gh->gphc', m, eye).reshape(G * P, G * C)

    bb = jnp.concatenate([block_diag_in(bb_re), block_diag_in(bb_im)], 1).astype(BF16)
    cc = jnp.concatenate([block_diag_out(c_re), -block_diag_out(c_im)], 0).astype(BF16)
    ns = G * P
    rows = ts * nb
    kern = functools.partial(_s5_kernel, nb=nb, ts=ts, lane_chunk=512)
    const = lambda i: (0, 0)
    return pl.pallas_call(
        kern,
        grid=(R // rows,),
        in_specs=[pl.BlockSpec((rows, W), lambda i: (i, 0)),
                  pl.BlockSpec((W, 2 * ns), const),
                  pl.BlockSpec((2 * ns, W), const),
                  pl.BlockSpec((1, ns), const),
                  pl.BlockSpec((1, ns), const),
                  pl.BlockSpec((1, W), const),
                  pl.BlockSpec((W, W), const),
                  pl.BlockSpec((1, W), const)],
        out_specs=pl.BlockSpec((rows, W), lambda i: (i, 0)),
        out_shape=jax.ShapeDtypeStruct((R, W), F32),
        scratch_shapes=[pltpu.VMEM((rows, 2 * ns), F32), pltpu.VMEM((nb, 2 * ns), F32)],
        compiler_params=_params("arbitrary"),
        name="s5",
    )(u_tm, bb, cc, a_re.reshape(1, ns), a_im.reshape(1, ns), d_skip.reshape(1, W),
      w_glu.astype(BF16), b_glu.reshape(1, W))


def _stack_pair(q):
    first = _lane_iota((1, LANES)) < HEAD_DIM
    return jnp.concatenate([jnp.where(first, q, 0.0), jnp.where(first, 0.0, q)], 0)


def _unstack_pair(x):
    t = x.shape[0] // 2
    first = _lane_iota((1, LANES)) < HEAD_DIM
    return jnp.where(first, x[:t], x[t:])


def _sb_kernel(q_ref, k_ref, v_ref, o_ref):
    tq = q_ref.shape[0]
    tk = Q_TILE
    qi = pl.program_id(2)
    qs = _stack_pair(q_ref[...]).astype(BF16)
    t_pos = qi * tq + (_row_iota((2 * tq, tk)) % tq)
    col = _lane_iota((2 * tq, tk))
    tri = (_row_iota((tk, tk)) > _lane_iota((tk, tk))).astype(BF16)

    def body(i, carry):
        acc, tail = carry
        kj = qi * (tq // tk) + (tq // tk - 1) - i
        r = pl.multiple_of(kj * tk, tk)
        kt = k_ref[pl.ds(r, tk), :]
        vt = v_ref[pl.ds(r, tk), :]
        z = _dot_nt(qs, kt)
        mask = (kj * tk + col) < t_pos
        lf = jnp.where(mask, -(jnp.maximum(z, 0.0) + jnp.log1p(jnp.exp(-jnp.abs(z)))), 0.0)
        after = _dot_split(lf, tri) + tail
        w = jnp.where(mask, jnp.exp(z + lf + after), 0.0)
        acc = acc + _dot(w, vt)
        tail = tail + jnp.sum(lf, -1, keepdims=True)
        return acc, tail

    n_tiles = (qi + 1) * (tq // tk)
    acc, _ = lax.fori_loop(0, n_tiles, body,
                           (jnp.zeros((2 * tq, LANES), F32), jnp.zeros((2 * tq, 1), F32)))
    o_ref[...] = _unstack_pair(acc)


def _stick_breaking(qkv, nb, seq, q_col, k_col, v_col, n_pairs):
    tq = Q_TILE
    nq = seq // tq
    return pl.pallas_call(
        _sb_kernel,
        grid=(nb, n_pairs, nq),
        in_specs=[pl.BlockSpec((tq, LANES), lambda b, p, i: (b * nq + i, q_col + p)),
                  pl.BlockSpec((seq, LANES), lambda b, p, i: (b, k_col + p)),
                  pl.BlockSpec((seq, LANES), lambda b, p, i: (b, v_col + p))],
        out_specs=pl.BlockSpec((tq, LANES), lambda b, p, i: (b * nq + i, p)),
        out_shape=jax.ShapeDtypeStruct((nb * seq, n_pairs * LANES), F32),
        compiler_params=_params("parallel", "parallel", "arbitrary"),
        name="stick_breaking",
    )(qkv, qkv, qkv)


def _dil_kernel(q_ref, k_ref, v_ref, o_ref, m_ref, l_ref, a_ref):
    seq = q_ref.shape[0]
    tq = Q_TILE
    first = _lane_iota((1, LANES)) < HEAD_DIM

    def tile(c, dil, wd, n_tiles, r, i):
        base = r + dil * tq * i

        def rows(ref, start):
            if dil == 1:
                return ref[pl.ds(pl.multiple_of(start, tq), tq), :]
            return ref[pl.ds(start, tq, stride=dil), :]

        qs = _stack_pair(rows(q_ref, base)).astype(BF16)
        q_idx = i * tq + (_row_iota((2 * tq, 1)) % tq)
        if n_tiles > 1:
            prev = r + dil * tq * jnp.maximum(i - 1, 0)
            kk = jnp.concatenate([rows(k_ref, prev), rows(k_ref, base)], 0)
            vv = jnp.concatenate([rows(v_ref, prev), rows(v_ref, base)], 0)
            k_idx = (i - 1) * tq + _lane_iota((1, 2 * tq))
        else:
            kk = rows(k_ref, base)
            vv = rows(v_ref, base)
            k_idx = _lane_iota((1, tq))
        valid = (k_idx >= 0) & (k_idx <= q_idx) & (q_idx - k_idx <= wd)
        s = jnp.where(valid, _dot_nt(qs, kk), NEG_BIG)
        m = jnp.max(s, -1, keepdims=True)
        p = jnp.where(valid, jnp.exp(s - m), 0.0)
        l = jnp.sum(p, -1, keepdims=True)
        acc = _dot(p, vv)
        m2 = jnp.where(first, m[:tq], m[tq:])
        l2 = jnp.where(first, l[:tq], l[tq:])
        a2 = jnp.where(first, acc[:tq], acc[tq:])
        if dil == 1:
            sl = pl.ds(pl.multiple_of(base, tq), tq)
        else:
            sl = pl.ds(base, tq, stride=dil)
        m_ref[c, sl, :] = m2
        l_ref[c, sl, :] = l2
        a_ref[c, sl, :] = a2

    for c, (window, dil) in enumerate(DIL_CONFIGS):
        wd = window // dil
        n_tiles = seq // dil // tq

        def per_residue(r, _, c=c, dil=dil, wd=wd, n_tiles=n_tiles):
            def per_tile(i, _):
                tile(c, dil, wd, n_tiles, r, i)
                return 0
            lax.fori_loop(0, n_tiles, per_tile, 0)
            return 0

        lax.fori_loop(0, dil, per_residue, 0)

    def combine(i, _):
        sl = pl.ds(pl.multiple_of(i * tq, tq), tq)
        m0, m1, m2 = m_ref[0, sl, :], m_ref[1, sl, :], m_ref[2, sl, :]
        mx = jnp.maximum(jnp.maximum(m0, m1), m2)
        e0, e1, e2 = jnp.exp(m0 - mx), jnp.exp(m1 - mx), jnp.exp(m2 - mx)
        num = e0 * a_ref[0, sl, :] + e1 * a_ref[1, sl, :] + e2 * a_ref[2, sl, :]
        den = e0 * l_ref[0, sl, :] + e1 * l_ref[1, sl, :] + e2 * l_ref[2, sl, :]
        o_ref[sl, :] = num / den
        return 0

    lax.fori_loop(0, seq // tq, combine, 0)


def _dilated(q_arr, q_col, k_arr, k_col, v_arr, v_col, nb, seq, n_pairs):
    return pl.pallas_call(
        _dil_kernel,
        grid=(nb, n_pairs),
        in_specs=[pl.BlockSpec((seq, LANES), lambda b, p: (b, q_col + p)),
                  pl.BlockSpec((seq, LANES), lambda b, p: (b, k_col + p)),
                  pl.BlockSpec((seq, LANES), lambda b, p: (b, v_col + p))],
        out_specs=pl.BlockSpec((seq, LANES), lambda b, p: (b, p)),
        out_shape=jax.ShapeDtypeStruct((nb * seq, n_pairs * LANES), F32),
        scratch_shapes=[pltpu.VMEM((3, seq, LANES), F32)] * 3,
        compiler_params=_params("parallel", "parallel"),
        name="dilated",
    )(q_arr, k_arr, v_arr)


def _cmp_kernel(t_ref, pe_ref, w1a_ref, w1b_ref, w2_ref, cos_ref, sin_ref, o_ref):
    nblk = t_ref.shape[0] // CMP_STRIDE
    j = pl.program_id(1)
    out = jnp.zeros((nblk, LANES), F32)
    for g in range(NSA_KV_HEADS):
        p1 = jnp.zeros((nblk, CMP_HIDDEN), F32)
        p2 = jnp.zeros((nblk, CMP_HIDDEN), F32)
        for l in range(CMP_STRIDE):
            x = t_ref[pl.ds(l, nblk, stride=CMP_STRIDE), :]
            p1 = p1 + _dot(x + pe_ref[0, l:l + 1, :], w1a_ref[0, g, l])
            p2 = p2 + _dot(x + pe_ref[0, CMP_STRIDE + l:CMP_STRIDE + l + 1, :], w1b_ref[0, g, l])
        hidden = p1 + pltpu.roll(p2, nblk - 1, axis=0)
        out = out + _dot(jax.nn.gelu(hidden), w2_ref[0, g])
    first = (_lane_iota((1, LANES)) % HEAD_DIM) < HALF
    partner = jnp.where(first, pltpu.roll(out, LANES - HALF, axis=1), pltpu.roll(out, HALF, axis=1))
    roped = out * cos_ref[...] + partner * sin_ref[...]
    o_ref[0] = jnp.where(j == 0, roped, out)


def _compress(p_arr, col0, nb, seq, pe, w1, w2, cos_c, sin_c):
    nblk = seq // CMP_STRIDE
    G = NSA_KV_HEADS
    pe2 = jnp.tile(pe, (1, 1, G))
    w1r = w1.reshape(2, CMP_LEN, HEAD_DIM, CMP_HIDDEN)
    w1e = jnp.zeros((2, G, CMP_LEN, LANES, CMP_HIDDEN), F32)
    w2e = jnp.zeros((2, G, CMP_HIDDEN, LANES), F32)
    for g in range(G):
        w1e = w1e.at[:, g, :, g * HEAD_DIM:(g + 1) * HEAD_DIM, :].set(w1r)
        w2e = w2e.at[:, g, :, g * HEAD_DIM:(g + 1) * HEAD_DIM].set(w2)
    w1e = w1e.astype(BF16)
    w2e = w2e.astype(BF16)
    return pl.pallas_call(
        _cmp_kernel,
        grid=(nb, 2),
        in_specs=[pl.BlockSpec((seq, LANES), lambda b, j: (b, col0 + j)),
                  pl.BlockSpec((1, CMP_LEN, LANES), lambda b, j: (j, 0, 0)),
                  pl.BlockSpec((1, G, CMP_STRIDE, LANES, CMP_HIDDEN), lambda b, j: (j, 0, 0, 0, 0)),
                  pl.BlockSpec((1, G, CMP_STRIDE, LANES, CMP_HIDDEN), lambda b, j: (j, 0, 1, 0, 0)),
                  pl.BlockSpec((1, G, CMP_HIDDEN, LANES), lambda b, j: (j, 0, 0, 0)),
                  pl.BlockSpec((nblk, LANES), lambda b, j: (0, 0)),
                  pl.BlockSpec((nblk, LANES), lambda b, j: (0, 0))],
        out_specs=pl.BlockSpec((1, nblk, LANES), lambda b, j: (j, b, 0)),
        out_shape=jax.ShapeDtypeStruct((2, nb * nblk, LANES), F32),
        compiler_params=_params("parallel", "parallel"),
        name="nsa_compress",
    )(p_arr, pe2, w1e, w1e, w2e, cos_c, sin_c)


def _nsa_kernel(q_ref, gate_ref, kc_ref, vc_ref, ks_ref, vs_ref, kw_ref, vw_ref, ov_ref, o_ref):
    tq = Q_TILE
    R = NSA_REP
    seq = ks_ref.shape[0]
    n_sel = seq // SEL_BLOCK
    n_cmp = (seq - CMP_LEN) // CMP_STRIDE + 1
    g = pl.program_id(1)
    qi = pl.program_id(2)
    lane = _lane_iota((1, LANES))
    mine = (lane // HEAD_DIM) == g

    parts = []
    for h in range(R):
        x = q_ref[:, (h // 2) * LANES:(h // 2 + 1) * LANES]
        x = jnp.where(g == (h % 2), x, _swap_halves(x))
        parts.append(jnp.where(mine, x, 0.0))
    qs = jnp.concatenate(parts, 0).astype(BF16)

    t_row = qi * tq + _row_iota((tq, 1))
    t_col = qi * tq + _row_iota((tq, LANES))

    nblk = kc_ref.shape[1]
    c_idx = _lane_iota((tq, nblk))
    cmask = (c_idx < n_cmp) & (c_idx * CMP_STRIDE + (CMP_LEN - 1) <= qi * tq + _row_iota((tq, nblk)))
    s_c = _dot_nt(qs, kc_ref[0]).reshape(R, tq, nblk)
    s_c = jnp.where(cmask, s_c, NEG_BIG)
    m_c = jnp.max(s_c, -1, keepdims=True)
    m_c = jnp.where(m_c > 0.5 * NEG_BIG, m_c, 0.0)
    p_c = jnp.where(cmask, jnp.exp(s_c - m_c), 0.0)
    p_c = p_c / jnp.maximum(jnp.sum(p_c, -1, keepdims=True), 1e-30)
    o_c = _dot(p_c.reshape(R * tq, nblk), vc_ref[0])

    imp = _dot_split(jnp.sum(p_c, 0), ov_ref[...])
    n_idx = _lane_iota((tq, LANES))
    cur = t_col // SEL_BLOCK
    imp = jnp.where((n_idx * SEL_BLOCK > t_col) | (n_idx >= n_sel), -1.0, imp)
    forced = (n_idx == 0) | (n_idx == cur) | (n_idx == cur - 1)
    imp = jnp.where(forced, 1e9, imp)
    rank = jnp.zeros((tq, LANES), F32)
    for m in range(n_sel):
        col = imp[:, m:m + 1]
        earlier = (n_idx > m).astype(F32)
        rank = rank + jnp.where(col > imp, 1.0, jnp.where(col == imp, earlier, 0.0))
    top_n = min(SEL_TOPN, n_sel)
    sel = jnp.where((rank < top_n) & (imp > -0.5), 1.0, 0.0).astype(BF16)

    k_col = _lane_iota((tq, tq))
    blk_row = _row_iota((LANES, tq))
    blk_of_key = _lane_iota((LANES, tq)) // SEL_BLOCK

    def attend(k_ref, v_ref, j_lo, j_hi, mask_fn):
        def body(j, carry):
            m_run, l_run, acc = carry
            r = pl.multiple_of(j * tq, tq)
            kt = k_ref[pl.ds(r, tq), :]
            vt = v_ref[pl.ds(r, tq), :]
            valid = mask_fn(j)
            s = jnp.where(valid, _dot_nt(qs, kt).reshape(R, tq, tq), NEG_BIG)
            m_new = jnp.maximum(m_run, jnp.max(s, -1, keepdims=True))
            alpha = jnp.exp(m_run - m_new)
            p = jnp.where(valid, jnp.exp(s - m_new), 0.0)
            l_run = alpha * l_run + jnp.sum(p, -1, keepdims=True)
            acc = alpha * acc + _dot(p.reshape(R * tq, tq), vt).reshape(R, tq, LANES)
            return m_new, l_run, acc

        init = (jnp.full((R, tq, 1), NEG_BIG, F32), jnp.zeros((R, tq, 1), F32),
                jnp.zeros((R, tq, LANES), F32))
        _, l_run, acc = lax.fori_loop(j_lo, j_hi, body, init)
        return acc / l_run

    def sel_mask(j):
        expand = (blk_row == (j * (tq // SEL_BLOCK) + blk_of_key)).astype(BF16)
        chosen = jnp.dot(sel, expand, preferred_element_type=F32) > 0.5
        return chosen & ((j * tq + k_col) <= t_row)

    def win_mask(j):
        diff = t_row - (j * tq + k_col)
        return (diff >= 0) & (diff < NSA_WINDOW)

    o_s = attend(ks_ref, vs_ref, 0, qi + 1, sel_mask)
    o_w = attend(kw_ref, vw_ref, jnp.maximum(qi - NSA_WINDOW // tq, 0), qi + 1, win_mask)
    o_c = o_c.reshape(R, tq, LANES)

    gate = jax.nn.sigmoid(gate_ref[...])
    outs = []
    for h in range(R):
        g_c = gate[:, 3 * h:3 * h + 1]
        g_s = gate[:, 3 * h + 1:3 * h + 2]
        g_w = gate[:, 3 * h + 2:3 * h + 3]
        y = g_c * o_c[h] + g_s * o_s[h] + g_w * o_w[h]
        outs.append(jnp.where(g == (h % 2), y, _swap_halves(y)))
    first = lane < HEAD_DIM
    for pair in range(R // 2):
        o_ref[:, pair * LANES:(pair + 1) * LANES] = jnp.where(first, outs[2 * pair], outs[2 * pair + 1])


def _nsa(r_arr, p_arr, cmp_kv, overlap, nb, seq, cols):
    tq = Q_TILE
    nq = seq // tq
    nblk = seq // CMP_STRIDE
    G = NSA_KV_HEADS
    qw = NSA_REP * HEAD_DIM
    return pl.pallas_call(
        _nsa_kernel,
        grid=(nb, G, nq),
        in_specs=[pl.BlockSpec((tq, qw), lambda b, g, i: (b * nq + i, cols['q'] * LANES // qw + g)),
                  pl.BlockSpec((tq, LANES), lambda b, g, i: (b * nq + i, cols['gate'] + g)),
                  pl.BlockSpec((1, nblk, LANES), lambda b, g, i: (0, b, 0)),
                  pl.BlockSpec((1, nblk, LANES), lambda b, g, i: (1, b, 0)),
                  pl.BlockSpec((seq, LANES), lambda b, g, i: (b, cols['k_slc'])),
                  pl.BlockSpec((seq, LANES), lambda b, g, i: (b, cols['v_slc'])),
                  pl.BlockSpec((seq, LANES), lambda b, g, i: (b, cols['k_win'])),
                  pl.BlockSpec((seq, LANES), lambda b, g, i: (b, cols['v_win'])),
                  pl.BlockSpec((nblk, LANES), lambda b, g, i: (0, 0))],
        out_specs=pl.BlockSpec((tq, qw), lambda b, g, i: (b * nq + i, g)),
        out_shape=jax.ShapeDtypeStruct((nb * seq, G * qw), F32),
        compiler_params=_params("parallel", "parallel", "arbitrary"),
        name="nsa",
    )(r_arr, p_arr, cmp_kv, cmp_kv, r_arr, p_arr, r_arr, p_arr, overlap)


def _mix_out_kernel(ya_ref, yb_ref, yc_ref, yd_ref, ng_ref, w_ref, h_ref, g_ref, b_ref,
                    o_ref, ob_ref, *, alpha):
    gw = ya_ref.shape[1]
    acc = alpha * h_ref[...]
    for n, y_ref in enumerate((ya_ref, yb_ref, yc_ref, yd_ref)):
        y = y_ref[...]
        y = y * lax.rsqrt(jnp.mean(y * y, -1, keepdims=True) + RMS_EPS) * ng_ref[n:n + 1, :]
        acc = acc + jnp.dot(y.astype(BF16), w_ref[n * gw:(n + 1) * gw, :], preferred_element_type=F32)
    out = _layer_norm(acc, g_ref[...], b_ref[...])
    o_ref[...] = out
    ob_ref[...] = out.astype(BF16)


def _mix_out(ya_tm, yb, yc, yd, norm_g, w_out, h, ln_g, ln_b, nb, seq, alpha, tm):
    T, D = h.shape
    gw = yb.shape[1]
    ns = seq // tm
    ya_view = ya_tm.reshape(seq, nb * gw)
    tok = lambda i: (i, 0)
    const = lambda i: (0, 0)
    return pl.pallas_call(
        functools.partial(_mix_out_kernel, alpha=alpha),
        grid=(T // tm,),
        in_specs=[pl.BlockSpec((tm, gw), lambda i: (i % ns, i // ns)),
                  pl.BlockSpec((tm, gw), tok), pl.BlockSpec((tm, gw), tok), pl.BlockSpec((tm, gw), tok),
                  pl.BlockSpec((4, gw), const),
                  pl.BlockSpec((4 * gw, D), const),
                  pl.BlockSpec((tm, D), tok),
                  pl.BlockSpec((1, D), const), pl.BlockSpec((1, D), const)],
        out_specs=[pl.BlockSpec((tm, D), tok), pl.BlockSpec((tm, D), tok)],
        out_shape=[jax.ShapeDtypeStruct((T, D), F32), jax.ShapeDtypeStruct((T, D), BF16)],
        compiler_params=_params("parallel"),
        name="mix_out",
    )(ya_view, yb, yc, yd, norm_g, w_out.astype(BF16), h, ln_g.reshape(1, D), ln_b.reshape(1, D))


def _xattn_kernel(hb_ref, h_ref, wq_ref, kv_ref, wo_ref, g_ref, b_ref, o_ref, ob_ref, *, alpha):
    width = wq_ref.shape[1]
    q = jnp.dot(hb_ref[...], wq_ref[...], preferred_element_type=F32)
    acc = alpha * h_ref[...]
    for hd in range(XA_HEADS):
        lo = hd * XA_HEAD_DIM
        k = kv_ref[:, lo:lo + XA_HEAD_DIM]
        v = kv_ref[:, width + lo:width + lo + XA_HEAD_DIM]
        s = _dot_nt(q[:, lo:lo + XA_HEAD_DIM], k) * (XA_HEAD_DIM ** -0.5)
        p = jnp.exp(s - jnp.max(s, -1, keepdims=True))
        p = p / jnp.sum(p, -1, keepdims=True)
        o = _dot(p, v)
        acc = acc + jnp.dot(o.astype(BF16), wo_ref[lo:lo + XA_HEAD_DIM, :], preferred_element_type=F32)
    out = _layer_norm(acc, g_ref[...], b_ref[...])
    o_ref[...] = out
    ob_ref[...] = out.astype(BF16)


def _xattn(hb, h, wq, kv, wo, ln_g, ln_b, seq, mem_len, alpha, tm):
    T, D = h.shape
    width = wq.shape[1]
    ns = seq // tm
    tok = lambda i: (i, 0)
    const = lambda i: (0, 0)
    return pl.pallas_call(
        functools.partial(_xattn_kernel, alpha=alpha),
        grid=(T // tm,),
        in_specs=[pl.BlockSpec((tm, D), tok), pl.BlockSpec((tm, D), tok),
                  pl.BlockSpec((D, width), const),
                  pl.BlockSpec((mem_len, 2 * width), lambda i: (i // ns, 0)),
                  pl.BlockSpec((width, D), const),
                  pl.BlockSpec((1, D), const), pl.BlockSpec((1, D), const)],
        out_specs=[pl.BlockSpec((tm, D), tok), pl.BlockSpec((tm, D), tok)],
        out_shape=[jax.ShapeDtypeStruct((T, D), F32), jax.ShapeDtypeStruct((T, D), BF16)],
        compiler_params=_params("parallel"),
        name="cross_attn",
    )(hb, h, wq.astype(BF16), kv, wo.astype(BF16), ln_g.reshape(1, D), ln_b.reshape(1, D))


def _ffn_kernel(hb_ref, h_ref, wg_ref, wu_ref, wd_ref, g_ref, b_ref, o_ref, ob_ref, acc_ref, *, alpha):
    j = pl.program_id(1)

    @pl.when(j == 0)
    def _():
        acc_ref[...] = alpha * h_ref[...]

    x = hb_ref[...]
    gate = jnp.dot(x, wg_ref[...], preferred_element_type=F32)
    up = jnp.dot(x, wu_ref[...], preferred_element_type=F32)
    act = (jax.nn.silu(gate) * up).astype(BF16)
    acc_ref[...] += jnp.dot(act, wd_ref[...], preferred_element_type=F32)

    @pl.when(j == pl.num_programs(1) - 1)
    def _():
        out = _layer_norm(acc_ref[...], g_ref[...], b_ref[...])
        o_ref[...] = out
        ob_ref[...] = out.astype(BF16)


def _ffn(hb, h, wg, wu, wd, ln_g, ln_b, alpha, tm, th):
    T, D = h.shape
    H = wg.shape[1]
    tok = lambda i, j: (i, 0)
    const = lambda i, j: (0, 0)
    return pl.pallas_call(
        functools.partial(_ffn_kernel, alpha=alpha),
        grid=(T // tm, H // th),
        in_specs=[pl.BlockSpec((tm, D), tok), pl.BlockSpec((tm, D), tok),
                  pl.BlockSpec((D, th), lambda i, j: (0, j)),
                  pl.BlockSpec((D, th), lambda i, j: (0, j)),
                  pl.BlockSpec((th, D), lambda i, j: (j, 0)),
                  pl.BlockSpec((1, D), const), pl.BlockSpec((1, D), const)],
        out_specs=[pl.BlockSpec((tm, D), tok), pl.BlockSpec((tm, D), tok)],
        out_shape=[jax.ShapeDtypeStruct((T, D), F32), jax.ShapeDtypeStruct((T, D), BF16)],
        scratch_shapes=[pltpu.VMEM((tm, D), F32)],
        compiler_params=_params("parallel", "arbitrary"),
        name="ffn",
    )(hb, h, wg.astype(BF16), wu.astype(BF16), wd.astype(BF16), ln_g.reshape(1, D), ln_b.reshape(1, D))


def _rope_tables(pos):
    inv_freq = ROPE_THETA ** (-jnp.arange(HALF, dtype=F32) / HALF)
    ang = pos.astype(F32)[:, None] * inv_freq[None, :]
    cos = jnp.tile(jnp.cos(ang), (1, LANES // HALF))
    sin = jnp.sin(ang)
    sin = jnp.tile(jnp.concatenate([-sin, sin], 1), (1, LANES // HEAD_DIM))
    return cos, sin


def _overlap_table(seq):
    n_cmp = (seq - CMP_LEN) // CMP_STRIDE + 1
    n_sel = seq // SEL_BLOCK
    ci = np.arange(n_cmp)[:, None] * CMP_STRIDE
    sj = np.arange(n_sel)[None, :] * SEL_BLOCK
    ov = np.clip(np.minimum(ci + CMP_LEN, sj + SEL_BLOCK) - np.maximum(ci, sj), 0, None) / CMP_LEN
    full = np.zeros((seq // CMP_STRIDE, LANES), np.float32)
    full[:n_cmp, :n_sel] = ov
    return jnp.asarray(full, dtype=BF16)


def _split_w_in(w_in, width):
    hw = width // 4
    kvw = NSA_KV_HEADS * HEAD_DIM
    n_gate = 3 * (hw // HEAD_DIM)
    o = np.cumsum([0, hw, hw, 6 * kvw, n_gate, 3 * hw, 3 * hw])
    u = w_in[:, o[0]:o[1]]
    nq = w_in[:, o[1]:o[2]]
    kv = [w_in[:, o[2] + j * kvw:o[2] + (j + 1) * kvw] for j in range(6)]
    gate = w_in[:, o[3]:o[4]]
    sb = w_in[:, o[4]:o[5]]
    dil = w_in[:, o[5]:o[6]]
    scale = HEAD_DIM ** -0.5
    per_group = 3 * NSA_REP
    gates = [jnp.pad(gate[:, g * per_group:(g + 1) * per_group], ((0, 0), (0, LANES - per_group)))
             for g in range(NSA_KV_HEADS)]
    w_rope = jnp.concatenate([nq * scale, kv[2], kv[4], dil[:, :hw] * scale, dil[:, hw:2 * hw]], 1)
    w_plain = jnp.concatenate([kv[0], kv[1], kv[3], kv[5]] + gates
                              + [sb[:, :hw] * scale, sb[:, hw:], dil[:, 2 * hw:]], 1)
    return u.astype(BF16), w_rope.astype(BF16), w_plain.astype(BF16)


R_NQ, R_KSLC, R_KWIN, R_DILQ, R_DILK = 0, 4, 5, 6, 10
P_KCMP, P_VSLC, P_VWIN, P_GATE, P_SBQ, P_SBK, P_SBV, P_DILV = 0, 2, 3, 4, 6, 10, 14, 18


def _hybrid_mixer(hb, h, nb, seq, w_in, s5_params, cmp_pe, cmp_w1, cmp_w2, norm_g, w_out,
                  ln_g, ln_b, tables, alpha, tm):
    cos, sin, cos_c, sin_c, overlap = tables
    width = w_out.shape[0]
    n_pairs = width // 4 // LANES
    w_u, w_rope, w_plain = _split_w_in(w_in, width)
    u_tm = _proj(hb, w_u, tm, w_u.shape[1], seq=seq, time_major_batch=nb)
    r_arr = _proj(hb, w_rope, tm, w_rope.shape[1] // 2, seq=seq, rope=(cos, sin))
    p_arr = _proj(hb, w_plain, tm, w_plain.shape[1] // 2)
    y_a = _s5(u_tm, nb, *s5_params)
    cmp_kv = _compress(p_arr, P_KCMP, nb, seq, cmp_pe, cmp_w1, cmp_w2, cos_c, sin_c)
    y_b = _nsa(r_arr, p_arr, cmp_kv, overlap, nb, seq,
               dict(q=R_NQ, gate=P_GATE, k_slc=R_KSLC, v_slc=P_VSLC, k_win=R_KWIN, v_win=P_VWIN))
    y_c = _stick_breaking(p_arr, nb, seq, P_SBQ, P_SBK, P_SBV, n_pairs)
    y_d = _dilated(r_arr, R_DILQ, r_arr, R_DILK, p_arr, P_DILV, nb, seq, n_pairs)
    return _mix_out(y_a, y_b, y_c, y_d, norm_g, w_out, h, ln_g, ln_b, nb, seq, alpha, tm)


def kernel(x, mem, ln_in_g, ln_in_b, w_in, s5_lambda_re, s5_lambda_im, s5_log_dt, s5_b_re, s5_b_im, s5_c_re, s5_c_im, s5_d, s5_w_glu, s5_b_glu, nsa_cmp_pe, nsa_cmp_w1, nsa_cmp_w2, mix_norm_g, w_out, ln1_g, ln1_b, xa_wq, xa_wkv, xa_wo, ln2_g, ln2_b, ffn_w_gate, ffn_w_up, ffn_w_down, ln3_g, ln3_b):
    nb, seq, d_model = x.shape
    mem_len = mem.shape[1]
    depth = w_in.shape[0]
    alpha = (2 * depth) ** 0.25
    tm = min(512, seq)
    pos = jnp.arange(seq)
    cos, sin = _rope_tables(pos)
    nblk = seq // CMP_STRIDE
    cos_c, sin_c = _rope_tables(jnp.arange(nblk) * CMP_STRIDE + CMP_LEN - 1)
    tables = (cos, sin, cos_c, sin_c, _overlap_table(seq))
    mem_b = mem.reshape(nb * mem_len, d_model).astype(BF16)

    h, hb = _ln_in(x.reshape(nb * seq, d_model), ln_in_g, ln_in_b, tm)
    for l in range(depth):
        s5_params = (s5_lambda_re[l], s5_lambda_im[l], s5_log_dt[l], s5_b_re[l], s5_b_im[l],
                     s5_c_re[l], s5_c_im[l], s5_d[l], s5_w_glu[l], s5_b_glu[l])
        h, hb = _hybrid_mixer(hb, h, nb, seq, w_in[l], s5_params, nsa_cmp_pe[l], nsa_cmp_w1[l],
                              nsa_cmp_w2[l], mix_norm_g[l], w_out[l], ln1_g[l], ln1_b[l], tables, alpha, tm)
        kv = _proj(mem_b, xa_wkv[l].astype(BF16), min(512, nb * mem_len), xa_wkv.shape[2] // 2)
        h, hb = _xattn(hb, h, xa_wq[l], kv, xa_wo[l], ln2_g[l], ln2_b[l], seq, mem_len, alpha, tm)
        h, hb = _ffn(hb, h, ffn_w_gate[l], ffn_w_up[l], ffn_w_down[l], ln3_g[l], ln3_b[l], alpha,
                     tm, 512)
    return h.reshape(nb, seq, d_model)
```

```python
import functools
import math

import numpy as np
import jax
import jax.numpy as jnp
from jax import lax
from jax.experimental import pallas as pl
from jax.experimental.pallas import tpu as pltpu

F32 = jnp.float32
BF16 = jnp.bfloat16

LANES = 128
VMEM_LIMIT = 56 * 1024 * 1024

HEAD_DIM = 64
HALF = HEAD_DIM // 2
ROPE_THETA = 10000.0
LN_EPS = 1e-5
RMS_EPS = 1e-6
SSM_CH = 16
SSM_STATE = 64
NSA_KV_HEADS = 2
NSA_REP = 4
CMP_LEN = 32
CMP_STRIDE = 16
CMP_HIDDEN = 128
SEL_BLOCK = 64
SEL_TOPN = 8
NSA_WINDOW = 512
DIL_CONFIGS = ((128, 1), (512, 4), (2048, 16))
XA_HEADS = 4
XA_HEAD_DIM = 128
Q_TILE = 128
SB_TILE = 256
NSA_TILE = 256
NEG_BIG = -1e30


def _params(*sem):
    return pltpu.CompilerParams(dimension_semantics=sem, vmem_limit_bytes=VMEM_LIMIT)


def _dot(a, b):
    return jnp.dot(a.astype(BF16), b.astype(BF16), preferred_element_type=F32)


def _dot_nt(a, b):
    return lax.dot_general(a.astype(BF16), b.astype(BF16), (((1,), (1,)), ((), ())),
                           preferred_element_type=F32)


def _dot_split(a, b_exact):
    hi = a.astype(BF16)
    lo = (a - hi.astype(F32)).astype(BF16)
    return (jnp.dot(hi, b_exact, preferred_element_type=F32)
            + jnp.dot(lo, b_exact, preferred_element_type=F32))


def _layer_norm(x, g, b):
    mu = jnp.mean(x, -1, keepdims=True)
    xc = x - mu
    var = jnp.mean(xc * xc, -1, keepdims=True)
    return xc * lax.rsqrt(var + LN_EPS) * g + b


def _lane_iota(shape):
    return lax.broadcasted_iota(jnp.int32, shape, len(shape) - 1)


def _row_iota(shape):
    return lax.broadcasted_iota(jnp.int32, shape, len(shape) - 2)


def _swap_halves(x):
    return pltpu.roll(x, HEAD_DIM, axis=x.ndim - 1)


def _ln_in_kernel(x_ref, g_ref, b_ref, h_ref, hb_ref):
    y = _layer_norm(x_ref[...], g_ref[...], b_ref[...])
    h_ref[...] = y
    hb_ref[...] = y.astype(BF16)


def _ln_in(x2, g, b, tm):
    T, D = x2.shape
    return pl.pallas_call(
        _ln_in_kernel,
        grid=(T // tm,),
        in_specs=[pl.BlockSpec((tm, D), lambda i: (i, 0)),
                  pl.BlockSpec((1, D), lambda i: (0, 0)),
                  pl.BlockSpec((1, D), lambda i: (0, 0))],
        out_specs=[pl.BlockSpec((tm, D), lambda i: (i, 0)),
                   pl.BlockSpec((tm, D), lambda i: (i, 0))],
        out_shape=[jax.ShapeDtypeStruct((T, D), F32), jax.ShapeDtypeStruct((T, D), BF16)],
        compiler_params=_params("parallel"),
        name="ln_in",
    )(x2, g.reshape(1, D), b.reshape(1, D))


def _proj_kernel(a_ref, w_ref, o_ref):
    o_ref[...] = jnp.dot(a_ref[...], w_ref[...], preferred_element_type=F32)


def _proj_rope_kernel(a_ref, w_ref, cos_ref, sin_ref, o_ref):
    acc = jnp.dot(a_ref[...], w_ref[...], preferred_element_type=F32)
    cos = cos_ref[...]
    sin = sin_ref[...]
    first = (_lane_iota((1, LANES)) % HEAD_DIM) < HALF
    for c in range(acc.shape[1] // LANES):
        x = acc[:, c * LANES:(c + 1) * LANES]
        partner = jnp.where(first, pltpu.roll(x, LANES - HALF, axis=1), pltpu.roll(x, HALF, axis=1))
        o_ref[:, c * LANES:(c + 1) * LANES] = x * cos + partner * sin


def _proj(a, w, tm, tn, seq=None, rope=None, time_major_batch=None):
    M, K = a.shape
    N = w.shape[1]
    nm, nn = M // tm, N // tn
    in_specs = [pl.BlockSpec((tm, K), lambda j, i: (i, 0)),
                pl.BlockSpec((K, tn), lambda j, i: (0, j))]
    args = [a, w]
    kern = _proj_kernel
    if rope is not None:
        ns = seq // tm
        in_specs += [pl.BlockSpec((tm, LANES), lambda j, i: (i % ns, 0))] * 2
        args += list(rope)
        kern = _proj_rope_kernel
    if time_major_batch is None:
        out_spec = pl.BlockSpec((tm, tn), lambda j, i: (i, j))
        out_shape = jax.ShapeDtypeStruct((M, N), F32)
    else:
        assert nn == 1
        ns = seq // tm
        out_spec = pl.BlockSpec((tm, N), lambda j, i: (i % ns, i // ns))
        out_shape = jax.ShapeDtypeStruct((seq, time_major_batch * N), F32)
    out = pl.pallas_call(
        kern, grid=(nn, nm), in_specs=in_specs, out_specs=out_spec, out_shape=out_shape,
        compiler_params=_params("parallel", "parallel"), name="proj",
    )(*args)
    if time_major_batch is not None:
        out = out.reshape(seq * time_major_batch, N)
    return out


def _s5_kernel(u_ref, bb_ref, cc_ref, are_ref, aim_ref, d_ref, wg_ref, bg_ref, y_ref,
               buf_ref, st_ref, *, nb, ts, lane_chunk):
    nstate = are_ref.shape[1]

    @pl.when(pl.program_id(0) == 0)
    def _():
        st_ref[...] = jnp.zeros_like(st_ref)

    u = u_ref[...]
    buf_ref[...] = jnp.dot(u.astype(BF16), bb_ref[...], preferred_element_type=F32)

    for c in range(nstate // lane_chunk):
        lo = c * lane_chunk
        a_re = jnp.broadcast_to(are_ref[:, lo:lo + lane_chunk], (nb, lane_chunk))
        a_im = jnp.broadcast_to(aim_ref[:, lo:lo + lane_chunk], (nb, lane_chunk))

        def step(t, carry):
            x_re, x_im = carry
            r = pl.multiple_of(t * nb, nb)
            b_re = buf_ref[pl.ds(r, nb), lo:lo + lane_chunk]
            b_im = buf_ref[pl.ds(r, nb), nstate + lo:nstate + lo + lane_chunk]
            n_re = a_re * x_re - a_im * x_im + b_re
            n_im = a_re * x_im + a_im * x_re + b_im
            buf_ref[pl.ds(r, nb), lo:lo + lane_chunk] = n_re
            buf_ref[pl.ds(r, nb), nstate + lo:nstate + lo + lane_chunk] = n_im
            return n_re, n_im

        x_re, x_im = lax.fori_loop(
            0, ts, step,
            (st_ref[:, lo:lo + lane_chunk], st_ref[:, nstate + lo:nstate + lo + lane_chunk]),
            unroll=8)
        st_ref[:, lo:lo + lane_chunk] = x_re
        st_ref[:, nstate + lo:nstate + lo + lane_chunk] = x_im

    y = jnp.dot(buf_ref[...].astype(BF16), cc_ref[...], preferred_element_type=F32) + d_ref[...] * u
    g = jax.nn.gelu(y)
    z = jnp.dot(g.astype(BF16), wg_ref[...], preferred_element_type=F32) + bg_ref[...]
    y_ref[...] = g * jax.nn.sigmoid(z)


def _s5(u_tm, nb, lam_re, lam_im, log_dt, b_re, b_im, c_re, c_im, d_skip, w_glu, b_glu, ts=64):
    R, W = u_tm.shape
    G, P = lam_re.shape
    C = SSM_CH
    lr = jnp.minimum(lam_re, -1e-4)
    li = lam_im
    dt = jnp.exp(log_dt)[:, None]
    mag = jnp.exp(lr * dt)
    a_re = mag * jnp.cos(li * dt)
    a_im = mag * jnp.sin(li * dt)
    den = lr * lr + li * li
    z_re = ((a_re - 1.0) * lr + a_im * li) / den
    z_im = (a_im * lr - (a_re - 1.0) * li) / den
    bb_re = z_re[..., None] * b_re - z_im[..., None] * b_im
    bb_im = z_re[..., None] * b_im + z_im[..., None] * b_re
    eye = jnp.eye(G, dtype=F32)

    def block_diag_in(m):
        return jnp.einsum('gpc,gh->gchp', m, eye).reshape(G * C, G * P)

    def block_diag_out(m):
        return jnp.einsum('gcp,gh->gphc', m, eye).reshape(G * P, G * C)

    bb = jnp.concatenate([block_diag_in(bb_re), block_diag_in(bb_im)], 1).astype(BF16)
    cc = jnp.concatenate([block_diag_out(c_re), -block_diag_out(c_im)], 0).astype(BF16)
    ns = G * P
    rows = ts * nb
    kern = functools.partial(_s5_kernel, nb=nb, ts=ts, lane_chunk=512)
    const = lambda i: (0, 0)
    return pl.pallas_call(
        kern,
        grid=(R // rows,),
        in_specs=[pl.BlockSpec((rows, W), lambda i: (i, 0)),
                  pl.BlockSpec((W, 2 * ns), const),
                  pl.BlockSpec((2 * ns, W), const),
                  pl.BlockSpec((1, ns), const),
                  pl.BlockSpec((1, ns), const),
                  pl.BlockSpec((1, W), const),
                  pl.BlockSpec((W, W), const),
                  pl.BlockSpec((1, W), const)],
        out_specs=pl.BlockSpec((rows, W), lambda i: (i, 0)),
        out_shape=jax.ShapeDtypeStruct((R, W), F32),
        scratch_shapes=[pltpu.VMEM((rows, 2 * ns), F32), pltpu.VMEM((nb, 2 * ns), F32)],
        compiler_params=_params("arbitrary"),
        name="s5",
    )(u_tm, bb, cc, a_re.reshape(1, ns), a_im.reshape(1, ns), d_skip.reshape(1, W),
      w_glu.astype(BF16), b_glu.reshape(1, W))


def _stack_pair(q):
    first = _lane_iota((1, LANES)) < HEAD_DIM
    return jnp.concatenate([jnp.where(first, q, 0.0), jnp.where(first, 0.0, q)], 0)


def _unstack_pair(x):
    t = x.shape[0] // 2
    first = _lane_iota((1, LANES)) < HEAD_DIM
    return jnp.where(first, x[:t], x[t:])


def _sb_kernel(q_ref, k_ref, v_ref, o_ref):
    tq = q_ref.shape[0]
    qi = pl.program_id(2)
    qs = _stack_pair(q_ref[...]).astype(BF16)
    tri = (_row_iota((tq, tq)) > _lane_iota((tq, tq))).astype(BF16)

    def tile(kj, acc, tail, mask):
        r = pl.multiple_of(kj * tq, tq)
        z = _dot_nt(qs, k_ref[pl.ds(r, tq), :])
        lf = -(jnp.maximum(z, 0.0) + jnp.log(1.0 + jnp.exp(-jnp.abs(z))))
        if mask is not None:
            lf = jnp.where(mask, lf, 0.0)
        w = jnp.exp(z + lf + (_dot_split(lf, tri) + tail))
        if mask is not None:
            w = jnp.where(mask, w, 0.0)
        acc = acc + _dot(w, v_ref[pl.ds(r, tq), :])
        return acc, tail + jnp.sum(lf, -1, keepdims=True)

    diag = _lane_iota((2 * tq, tq)) < (_row_iota((2 * tq, tq)) % tq)
    carry = tile(qi, jnp.zeros((2 * tq, LANES), F32), jnp.zeros((2 * tq, 1), F32), diag)
    acc, _ = lax.fori_loop(0, qi, lambda i, c: tile(qi - 1 - i, c[0], c[1], None), carry)
    o_ref[...] = _unstack_pair(acc)


def _stick_breaking(qkv, nb, seq, q_col, k_col, v_col, n_pairs):
    tq = min(SB_TILE, seq)
    nq = seq // tq
    return pl.pallas_call(
        _sb_kernel,
        grid=(nb, n_pairs, nq),
        in_specs=[pl.BlockSpec((tq, LANES), lambda b, p, i: (b * nq + i, q_col + p)),
                  pl.BlockSpec((seq, LANES), lambda b, p, i: (b, k_col + p)),
                  pl.BlockSpec((seq, LANES), lambda b, p, i: (b, v_col + p))],
        out_specs=pl.BlockSpec((tq, LANES), lambda b, p, i: (b * nq + i, p)),
        out_shape=jax.ShapeDtypeStruct((nb * seq, n_pairs * LANES), F32),
        compiler_params=_params("parallel", "parallel", "arbitrary"),
        name="stick_breaking",
    )(qkv, qkv, qkv)


def _dil_kernel(q_ref, k_ref, v_ref, o_ref, m_ref, l_ref, a_ref):
    seq = q_ref.shape[0]
    tq = Q_TILE
    first = _lane_iota((1, LANES)) < HEAD_DIM

    def tile(c, dil, wd, n_tiles, r, i):
        base = r + dil * tq * i

        def rows(ref, start):
            if dil == 1:
                return ref[pl.ds(pl.multiple_of(start, tq), tq), :]
            return ref[pl.ds(start, tq, stride=dil), :]

        qs = _stack_pair(rows(q_ref, base)).astype(BF16)
        q_idx = i * tq + (_row_iota((2 * tq, 1)) % tq)
        if n_tiles > 1:
            prev = r + dil * tq * jnp.maximum(i - 1, 0)
            kk = jnp.concatenate([rows(k_ref, prev), rows(k_ref, base)], 0)
            vv = jnp.concatenate([rows(v_ref, prev), rows(v_ref, base)], 0)
            k_idx = (i - 1) * tq + _lane_iota((1, 2 * tq))
        else:
            kk = rows(k_ref, base)
            vv = rows(v_ref, base)
            k_idx = _lane_iota((1, tq))
        valid = (k_idx >= 0) & (k_idx <= q_idx) & (q_idx - k_idx <= wd)
        s = jnp.where(valid, _dot_nt(qs, kk), NEG_BIG)
        m = jnp.max(s, -1, keepdims=True)
        p = jnp.where(valid, jnp.exp(s - m), 0.0)
        l = jnp.sum(p, -1, keepdims=True)
        acc = _dot(p, vv)
        m2 = jnp.where(first, m[:tq], m[tq:])
        l2 = jnp.where(first, l[:tq], l[tq:])
        a2 = jnp.where(first, acc[:tq], acc[tq:])
        if dil == 1:
            sl = pl.ds(pl.multiple_of(base, tq), tq)
        else:
            sl = pl.ds(base, tq, stride=dil)
        m_ref[c, sl, :] = m2
        l_ref[c, sl, :] = l2
        a_ref[c, sl, :] = a2

    for c, (window, dil) in enumerate(DIL_CONFIGS):
        wd = window // dil
        n_tiles = seq // dil // tq

        def per_tile(n, _, c=c, dil=dil, wd=wd, n_tiles=n_tiles):
            tile(c, dil, wd, n_tiles, n // n_tiles, n % n_tiles)
            return 0

        lax.fori_loop(0, dil * n_tiles, per_tile, 0, unroll=4)

    def combine(i, _):
        sl = pl.ds(pl.multiple_of(i * tq, tq), tq)
        m0, m1, m2 = m_ref[0, sl, :], m_ref[1, sl, :], m_ref[2, sl, :]
        mx = jnp.maximum(jnp.maximum(m0, m1), m2)
        e0, e1, e2 = jnp.exp(m0 - mx), jnp.exp(m1 - mx), jnp.exp(m2 - mx)
        num = e0 * a_ref[0, sl, :] + e1 * a_ref[1, sl, :] + e2 * a_ref[2, sl, :]
        den = e0 * l_ref[0, sl, :] + e1 * l_ref[1, sl, :] + e2 * l_ref[2, sl, :]
        o_ref[sl, :] = num / den
        return 0

    lax.fori_loop(0, seq // tq, combine, 0)


def _dilated(q_arr, q_col, k_arr, k_col, v_arr, v_col, nb, seq, n_pairs):
    return pl.pallas_call(
        _dil_kernel,
        grid=(nb, n_pairs),
        in_specs=[pl.BlockSpec((seq, LANES), lambda b, p: (b, q_col + p)),
                  pl.BlockSpec((seq, LANES), lambda b, p: (b, k_col + p)),
                  pl.BlockSpec((seq, LANES), lambda b, p: (b, v_col + p))],
        out_specs=pl.BlockSpec((seq, LANES), lambda b, p: (b, p)),
        out_shape=jax.ShapeDtypeStruct((nb * seq, n_pairs * LANES), F32),
        scratch_shapes=[pltpu.VMEM((3, seq, LANES), F32)] * 3,
        compiler_params=_params("parallel", "parallel"),
        name="dilated",
    )(q_arr, k_arr, v_arr)


def _cmp_kernel(t_ref, pe_ref, w1a_ref, w1b_ref, w2_ref, cos_ref, sin_ref, o_ref):
    nblk = t_ref.shape[0] // CMP_STRIDE
    j = pl.program_id(1)
    out = jnp.zeros((nblk, LANES), F32)
    for g in range(NSA_KV_HEADS):
        p1 = jnp.zeros((nblk, CMP_HIDDEN), F32)
        p2 = jnp.zeros((nblk, CMP_HIDDEN), F32)
        for l in range(CMP_STRIDE):
            x = t_ref[pl.ds(l, nblk, stride=CMP_STRIDE), :]
            p1 = p1 + _dot(x + pe_ref[0, l:l + 1, :], w1a_ref[0, g, l])
            p2 = p2 + _dot(x + pe_ref[0, CMP_STRIDE + l:CMP_STRIDE + l + 1, :], w1b_ref[0, g, l])
        hidden = p1 + pltpu.roll(p2, nblk - 1, axis=0)
        out = out + _dot(jax.nn.gelu(hidden), w2_ref[0, g])
    first = (_lane_iota((1, LANES)) % HEAD_DIM) < HALF
    partner = jnp.where(first, pltpu.roll(out, LANES - HALF, axis=1), pltpu.roll(out, HALF, axis=1))
    roped = out * cos_ref[...] + partner * sin_ref[...]
    o_ref[0] = jnp.where(j == 0, roped, out)


def _compress(p_arr, col0, nb, seq, pe, w1, w2, cos_c, sin_c):
    nblk = seq // CMP_STRIDE
    G = NSA_KV_HEADS
    pe2 = jnp.tile(pe, (1, 1, G))
    w1r = w1.reshape(2, CMP_LEN, HEAD_DIM, CMP_HIDDEN)
    w1e = jnp.zeros((2, G, CMP_LEN, LANES, CMP_HIDDEN), F32)
    w2e = jnp.zeros((2, G, CMP_HIDDEN, LANES), F32)
    for g in range(G):
        w1e = w1e.at[:, g, :, g * HEAD_DIM:(g + 1) * HEAD_DIM, :].set(w1r)
        w2e = w2e.at[:, g, :, g * HEAD_DIM:(g + 1) * HEAD_DIM].set(w2)
    w1e = w1e.astype(BF16)
    w2e = w2e.astype(BF16)
    return pl.pallas_call(
        _cmp_kernel,
        grid=(nb, 2),
        in_specs=[pl.BlockSpec((seq, LANES), lambda b, j: (b, col0 + j)),
                  pl.BlockSpec((1, CMP_LEN, LANES), lambda b, j: (j, 0, 0)),
                  pl.BlockSpec((1, G, CMP_STRIDE, LANES, CMP_HIDDEN), lambda b, j: (j, 0, 0, 0, 0)),
                  pl.BlockSpec((1, G, CMP_STRIDE, LANES, CMP_HIDDEN), lambda b, j: (j, 0, 1, 0, 0)),
                  pl.BlockSpec((1, G, CMP_HIDDEN, LANES), lambda b, j: (j, 0, 0, 0)),
                  pl.BlockSpec((nblk, LANES), lambda b, j: (0, 0)),
                  pl.BlockSpec((nblk, LANES), lambda b, j: (0, 0))],
        out_specs=pl.BlockSpec((1, nblk, LANES), lambda b, j: (j, b, 0)),
        out_shape=jax.ShapeDtypeStruct((2, nb * nblk, LANES), F32),
        compiler_params=_params("parallel", "parallel"),
        name="nsa_compress",
    )(p_arr, pe2, w1e, w1e, w2e, cos_c, sin_c)


def _nsa_kernel(q_ref, gate_ref, kc_ref, vc_ref, ks_ref, vs_ref, kw_ref, vw_ref, ovt_ref, o_ref):
    tq = q_ref.shape[0]
    R = NSA_REP
    seq = ks_ref.shape[0]
    n_sel = seq // SEL_BLOCK
    n_cmp = (seq - CMP_LEN) // CMP_STRIDE + 1
    g = pl.program_id(1)
    qi = pl.program_id(2)
    lane = _lane_iota((1, LANES))
    mine = (lane // HEAD_DIM) == g

    parts = []
    for h in range(R):
        x = q_ref[:, (h // 2) * LANES:(h // 2 + 1) * LANES]
        x = jnp.where(g == (h % 2), x, _swap_halves(x))
        parts.append(jnp.where(mine, x, 0.0))
    qs = jnp.concatenate(parts, 0).astype(BF16)

    t_row = qi * tq + _row_iota((tq, 1))

    nblk = kc_ref.shape[1]
    c_idx = _lane_iota((tq, nblk))
    cmask = (c_idx < n_cmp) & (c_idx * CMP_STRIDE + (CMP_LEN - 1) <= qi * tq + _row_iota((tq, nblk)))
    s_c = _dot_nt(qs, kc_ref[0]).reshape(R, tq, nblk)
    s_c = jnp.where(cmask, s_c, NEG_BIG)
    m_c = jnp.max(s_c, -1, keepdims=True)
    m_c = jnp.where(m_c > 0.5 * NEG_BIG, m_c, 0.0)
    p_c = jnp.where(cmask, jnp.exp(s_c - m_c), 0.0)
    p_c = p_c / jnp.maximum(jnp.sum(p_c, -1, keepdims=True), 1e-30)
    o_c = _dot(p_c.reshape(R * tq, nblk), vc_ref[0])

    psum = jnp.sum(p_c, 0)
    p_hi = psum.astype(BF16)
    p_lo = (psum - p_hi.astype(F32)).astype(BF16)
    imp = (_dot_nt(ovt_ref[...], p_hi) + _dot_nt(ovt_ref[...], p_lo))[:n_sel]
    n_idx = _row_iota((n_sel, tq))
    t_q = qi * tq + _lane_iota((n_sel, tq))
    cur = t_q // SEL_BLOCK
    imp = jnp.where(n_idx * SEL_BLOCK > t_q, -1.0, imp)
    forced = (n_idx == 0) | (n_idx == cur) | (n_idx == cur - 1)
    imp = jnp.where(forced, 1e9, imp)
    rank = jnp.zeros((n_sel, tq), F32)
    for m in range(n_sel):
        row = imp[m:m + 1, :]
        earlier = (n_idx > m).astype(F32)
        rank = rank + jnp.where(row > imp, 1.0, jnp.where(row == imp, earlier, 0.0))
    top_n = min(SEL_TOPN, n_sel)
    sel_t = jnp.where((rank < top_n) & (imp > -0.5), 1.0, 0.0)
    sel_t = jnp.concatenate([sel_t, jnp.zeros((LANES - n_sel, tq), F32)], 0)
    sel = sel_t.T.astype(BF16)

    k_col = _lane_iota((tq, tq))
    blk_row = _row_iota((LANES, tq))
    blk_of_key = _lane_iota((LANES, tq)) // SEL_BLOCK

    def attend(k_ref, v_ref, j_lo, j_hi, mask_fn):
        def body(j, carry):
            m_run, l_run, acc = carry
            r = pl.multiple_of(j * tq, tq)
            kt = k_ref[pl.ds(r, tq), :]
            vt = v_ref[pl.ds(r, tq), :]
            valid = mask_fn(j)
            s = jnp.where(valid, _dot_nt(qs, kt).reshape(R, tq, tq), NEG_BIG)
            m_new = jnp.maximum(m_run, jnp.max(s, -1, keepdims=True))
            alpha = jnp.exp(m_run - m_new)
            p = jnp.where(valid, jnp.exp(s - m_new), 0.0)
            l_run = alpha * l_run + jnp.sum(p, -1, keepdims=True)
            acc = alpha * acc + _dot(p.reshape(R * tq, tq), vt).reshape(R, tq, LANES)
            return m_new, l_run, acc

        init = (jnp.full((R, tq, 1), NEG_BIG, F32), jnp.zeros((R, tq, 1), F32),
                jnp.zeros((R, tq, LANES), F32))
        _, l_run, acc = lax.fori_loop(j_lo, j_hi, body, init)
        return acc / l_run

    def sel_mask(j):
        expand = (blk_row == (j * (tq // SEL_BLOCK) + blk_of_key)).astype(BF16)
        chosen = jnp.dot(sel, expand, preferred_element_type=F32) > 0.5
        return chosen & ((j * tq + k_col) <= t_row)

    def win_mask(j):
        diff = t_row - (j * tq + k_col)
        return (diff >= 0) & (diff < NSA_WINDOW)

    o_s = attend(ks_ref, vs_ref, 0, qi + 1, sel_mask)
    o_w = attend(kw_ref, vw_ref, jnp.maximum(qi - NSA_WINDOW // tq, 0), qi + 1, win_mask)
    o_c = o_c.reshape(R, tq, LANES)

    gate = jax.nn.sigmoid(gate_ref[...])
    outs = []
    for h in range(R):
        g_c = gate[:, 3 * h:3 * h + 1]
        g_s = gate[:, 3 * h + 1:3 * h + 2]
        g_w = gate[:, 3 * h + 2:3 * h + 3]
        y = g_c * o_c[h] + g_s * o_s[h] + g_w * o_w[h]
        outs.append(jnp.where(g == (h % 2), y, _swap_halves(y)))
    first = lane < HEAD_DIM
    for pair in range(R // 2):
        o_ref[:, pair * LANES:(pair + 1) * LANES] = jnp.where(first, outs[2 * pair], outs[2 * pair + 1])


def _nsa(r_arr, p_arr, cmp_kv, overlap, nb, seq, cols):
    tq = min(NSA_TILE, seq)
    nq = seq // tq
    nblk = seq // CMP_STRIDE
    G = NSA_KV_HEADS
    qw = NSA_REP * HEAD_DIM
    return pl.pallas_call(
        _nsa_kernel,
        grid=(nb, G, nq),
        in_specs=[pl.BlockSpec((tq, qw), lambda b, g, i: (b * nq + i, cols['q'] * LANES // qw + g)),
                  pl.BlockSpec((tq, LANES), lambda b, g, i: (b * nq + i, cols['gate'] + g)),
                  pl.BlockSpec((1, nblk, LANES), lambda b, g, i: (0, b, 0)),
                  pl.BlockSpec((1, nblk, LANES), lambda b, g, i: (1, b, 0)),
                  pl.BlockSpec((seq, LANES), lambda b, g, i: (b, cols['k_slc'])),
                  pl.BlockSpec((seq, LANES), lambda b, g, i: (b, cols['v_slc'])),
                  pl.BlockSpec((seq, LANES), lambda b, g, i: (b, cols['k_win'])),
                  pl.BlockSpec((seq, LANES), lambda b, g, i: (b, cols['v_win'])),
                  pl.BlockSpec((LANES, nblk), lambda b, g, i: (0, 0))],
        out_specs=pl.BlockSpec((tq, qw), lambda b, g, i: (b * nq + i, g)),
        out_shape=jax.ShapeDtypeStruct((nb * seq, G * qw), F32),
        compiler_params=_params("parallel", "parallel", "arbitrary"),
        name="nsa",
    )(r_arr, p_arr, cmp_kv, cmp_kv, r_arr, p_arr, r_arr, p_arr, overlap)


def _mix_out_kernel(ya_ref, yb_ref, yc_ref, yd_ref, ng_ref, w_ref, h_ref, g_ref, b_ref,
                    o_ref, ob_ref, *, alpha):
    gw = ya_ref.shape[1]
    acc = alpha * h_ref[...]
    for n, y_ref in enumerate((ya_ref, yb_ref, yc_ref, yd_ref)):
        y = y_ref[...]
        y = y * lax.rsqrt(jnp.mean(y * y, -1, keepdims=True) + RMS_EPS) * ng_ref[n:n + 1, :]
        acc = acc + jnp.dot(y.astype(BF16), w_ref[n * gw:(n + 1) * gw, :], preferred_element_type=F32)
    out = _layer_norm(acc, g_ref[...], b_ref[...])
    o_ref[...] = out
    ob_ref[...] = out.astype(BF16)


def _mix_out(ya_tm, yb, yc, yd, norm_g, w_out, h, ln_g, ln_b, nb, seq, alpha, tm):
    T, D = h.shape
    gw = yb.shape[1]
    ns = seq // tm
    ya_view = ya_tm.reshape(seq, nb * gw)
    tok = lambda i: (i, 0)
    const = lambda i: (0, 0)
    return pl.pallas_call(
        functools.partial(_mix_out_kernel, alpha=alpha),
        grid=(T // tm,),
        in_specs=[pl.BlockSpec((tm, gw), lambda i: (i % ns, i // ns)),
                  pl.BlockSpec((tm, gw), tok), pl.BlockSpec((tm, gw), tok), pl.BlockSpec((tm, gw), tok),
                  pl.BlockSpec((4, gw), const),
                  pl.BlockSpec((4 * gw, D), const),
                  pl.BlockSpec((tm, D), tok),
                  pl.BlockSpec((1, D), const), pl.BlockSpec((1, D), const)],
        out_specs=[pl.BlockSpec((tm, D), tok), pl.BlockSpec((tm, D), tok)],
        out_shape=[jax.ShapeDtypeStruct((T, D), F32), jax.ShapeDtypeStruct((T, D), BF16)],
        compiler_params=_params("parallel"),
        name="mix_out",
    )(ya_view, yb, yc, yd, norm_g, w_out.astype(BF16), h, ln_g.reshape(1, D), ln_b.reshape(1, D))


def _xattn_kernel(hb_ref, h_ref, wq_ref, kv_ref, wo_ref, g_ref, b_ref, o_ref, ob_ref, *, alpha):
    width = wq_ref.shape[1]
    q = jnp.dot(hb_ref[...], wq_ref[...], preferred_element_type=F32)
    acc = alpha * h_ref[...]
    for hd in range(XA_HEADS):
        lo = hd * XA_HEAD_DIM
        k = kv_ref[:, lo:lo + XA_HEAD_DIM]
        v = kv_ref[:, width + lo:width + lo + XA_HEAD_DIM]
        s = _dot_nt(q[:, lo:lo + XA_HEAD_DIM], k) * (XA_HEAD_DIM ** -0.5)
        p = jnp.exp(s - jnp.max(s, -1, keepdims=True))
        p = p / jnp.sum(p, -1, keepdims=True)
        o = _dot(p, v)
        acc = acc + jnp.dot(o.astype(BF16), wo_ref[lo:lo + XA_HEAD_DIM, :], preferred_element_type=F32)
    out = _layer_norm(acc, g_ref[...], b_ref[...])
    o_ref[...] = out
    ob_ref[...] = out.astype(BF16)


def _xattn(hb, h, wq, kv, wo, ln_g, ln_b, seq, mem_len, alpha, tm):
    T, D = h.shape
    width = wq.shape[1]
    ns = seq // tm
    tok = lambda i: (i, 0)
    const = lambda i: (0, 0)
    return pl.pallas_call(
        functools.partial(_xattn_kernel, alpha=alpha),
        grid=(T // tm,),
        in_specs=[pl.BlockSpec((tm, D), tok), pl.BlockSpec((tm, D), tok),
                  pl.BlockSpec((D, width), const),
                  pl.BlockSpec((mem_len, 2 * width), lambda i: (i // ns, 0)),
                  pl.BlockSpec((width, D), const),
                  pl.BlockSpec((1, D), const), pl.BlockSpec((1, D), const)],
        out_specs=[pl.BlockSpec((tm, D), tok), pl.BlockSpec((tm, D), tok)],
        out_shape=[jax.ShapeDtypeStruct((T, D), F32), jax.ShapeDtypeStruct((T, D), BF16)],
        compiler_params=_params("parallel"),
        name="cross_attn",
    )(hb, h, wq.astype(BF16), kv, wo.astype(BF16), ln_g.reshape(1, D), ln_b.reshape(1, D))


def _ffn_kernel(hb_ref, h_ref, wg_ref, wu_ref, wd_ref, g_ref, b_ref, o_ref, ob_ref, acc_ref, *, alpha):
    j = pl.program_id(1)

    @pl.when(j == 0)
    def _():
        acc_ref[...] = alpha * h_ref[...]

    x = hb_ref[...]
    gate = jnp.dot(x, wg_ref[...], preferred_element_type=F32)
    up = jnp.dot(x, wu_ref[...], preferred_element_type=F32)
    act = (jax.nn.silu(gate) * up).astype(BF16)
    acc_ref[...] += jnp.dot(act, wd_ref[...], preferred_element_type=F32)

    @pl.when(j == pl.num_programs(1) - 1)
    def _():
        out = _layer_norm(acc_ref[...], g_ref[...], b_ref[...])
        o_ref[...] = out
        ob_ref[...] = out.astype(BF16)


def _ffn(hb, h, wg, wu, wd, ln_g, ln_b, alpha, tm, th):
    T, D = h.shape
    H = wg.shape[1]
    tok = lambda i, j: (i, 0)
    const = lambda i, j: (0, 0)
    return pl.pallas_call(
        functools.partial(_ffn_kernel, alpha=alpha),
        grid=(T // tm, H // th),
        in_specs=[pl.BlockSpec((tm, D), tok), pl.BlockSpec((tm, D), tok),
                  pl.BlockSpec((D, th), lambda i, j: (0, j)),
                  pl.BlockSpec((D, th), lambda i, j: (0, j)),
                  pl.BlockSpec((th, D), lambda i, j: (j, 0)),
                  pl.BlockSpec((1, D), const), pl.BlockSpec((1, D), const)],
        out_specs=[pl.BlockSpec((tm, D), tok), pl.BlockSpec((tm, D), tok)],
        out_shape=[jax.ShapeDtypeStruct((T, D), F32), jax.ShapeDtypeStruct((T, D), BF16)],
        scratch_shapes=[pltpu.VMEM((tm, D), F32)],
        compiler_params=_params("parallel", "arbitrary"),
        name="ffn",
    )(hb, h, wg.astype(BF16), wu.astype(BF16), wd.astype(BF16), ln_g.reshape(1, D), ln_b.reshape(1, D))


def _rope_tables(pos):
    inv_freq = ROPE_THETA ** (-jnp.arange(HALF, dtype=F32) / HALF)
    ang = pos.astype(F32)[:, None] * inv_freq[None, :]
    cos = jnp.tile(jnp.cos(ang), (1, LANES // HALF))
    sin = jnp.sin(ang)
    sin = jnp.tile(jnp.concatenate([-sin, sin], 1), (1, LANES // HEAD_DIM))
    return cos, sin


def _overlap_table(seq):
    n_cmp = (seq - CMP_LEN) // CMP_STRIDE + 1
    n_sel = seq // SEL_BLOCK
    ci = np.arange(n_cmp)[:, None] * CMP_STRIDE
    sj = np.arange(n_sel)[None, :] * SEL_BLOCK
    ov = np.clip(np.minimum(ci + CMP_LEN, sj + SEL_BLOCK) - np.maximum(ci, sj), 0, None) / CMP_LEN
    full = np.zeros((LANES, seq // CMP_STRIDE), np.float32)
    full[:n_sel, :n_cmp] = ov.T
    return jnp.asarray(full, dtype=BF16)


def _split_w_in(w_in, width):
    hw = width // 4
    kvw = NSA_KV_HEADS * HEAD_DIM
    n_gate = 3 * (hw // HEAD_DIM)
    o = np.cumsum([0, hw, hw, 6 * kvw, n_gate, 3 * hw, 3 * hw])
    u = w_in[:, o[0]:o[1]]
    nq = w_in[:, o[1]:o[2]]
    kv = [w_in[:, o[2] + j * kvw:o[2] + (j + 1) * kvw] for j in range(6)]
    gate = w_in[:, o[3]:o[4]]
    sb = w_in[:, o[4]:o[5]]
    dil = w_in[:, o[5]:o[6]]
    scale = HEAD_DIM ** -0.5
    per_group = 3 * NSA_REP
    gates = [jnp.pad(gate[:, g * per_group:(g + 1) * per_group], ((0, 0), (0, LANES - per_group)))
             for g in range(NSA_KV_HEADS)]
    w_rope = jnp.concatenate([nq * scale, kv[2], kv[4], dil[:, :hw] * scale, dil[:, hw:2 * hw]], 1)
    w_plain = jnp.concatenate([kv[0], kv[1], kv[3], kv[5]] + gates
                              + [sb[:, :hw] * scale, sb[:, hw:], dil[:, 2 * hw:]], 1)
    return u.astype(BF16), w_rope.astype(BF16), w_plain.astype(BF16)


R_NQ, R_KSLC, R_KWIN, R_DILQ, R_DILK = 0, 4, 5, 6, 10
P_KCMP, P_VSLC, P_VWIN, P_GATE, P_SBQ, P_SBK, P_SBV, P_DILV = 0, 2, 3, 4, 6, 10, 14, 18


def _hybrid_mixer(hb, h, nb, seq, w_in, s5_params, cmp_pe, cmp_w1, cmp_w2, norm_g, w_out,
                  ln_g, ln_b, tables, alpha, tm):
    cos, sin, cos_c, sin_c, overlap = tables
    width = w_out.shape[0]
    n_pairs = width // 4 // LANES
    w_u, w_rope, w_plain = _split_w_in(w_in, width)
    u_tm = _proj(hb, w_u, tm, w_u.shape[1], seq=seq, time_major_batch=nb)
    r_arr = _proj(hb, w_rope, tm, w_rope.shape[1] // 2, seq=seq, rope=(cos, sin))
    p_arr = _proj(hb, w_plain, tm, w_plain.shape[1] // 2)
    y_a = _s5(u_tm, nb, *s5_params)
    cmp_kv = _compress(p_arr, P_KCMP, nb, seq, cmp_pe, cmp_w1, cmp_w2, cos_c, sin_c)
    y_b = _nsa(r_arr, p_arr, cmp_kv, overlap, nb, seq,
               dict(q=R_NQ, gate=P_GATE, k_slc=R_KSLC, v_slc=P_VSLC, k_win=R_KWIN, v_win=P_VWIN))
    y_c = _stick_breaking(p_arr, nb, seq, P_SBQ, P_SBK, P_SBV, n_pairs)
    y_d = _dilated(r_arr, R_DILQ, r_arr, R_DILK, p_arr, P_DILV, nb, seq, n_pairs)
    return _mix_out(y_a, y_b, y_c, y_d, norm_g, w_out, h, ln_g, ln_b, nb, seq, alpha, tm)


def kernel(x, mem, ln_in_g, ln_in_b, w_in, s5_lambda_re, s5_lambda_im, s5_log_dt, s5_b_re, s5_b_im, s5_c_re, s5_c_im, s5_d, s5_w_glu, s5_b_glu, nsa_cmp_pe, nsa_cmp_w1, nsa_cmp_w2, mix_norm_g, w_out, ln1_g, ln1_b, xa_wq, xa_wkv, xa_wo, ln2_g, ln2_b, ffn_w_gate, ffn_w_up, ffn_w_down, ln3_g, ln3_b):
    nb, seq, d_model = x.shape
    mem_len = mem.shape[1]
    depth = w_in.shape[0]
    alpha = (2 * depth) ** 0.25
    tm = min(512, seq)
    pos = jnp.arange(seq)
    cos, sin = _rope_tables(pos)
    nblk = seq // CMP_STRIDE
    cos_c, sin_c = _rope_tables(jnp.arange(nblk) * CMP_STRIDE + CMP_LEN - 1)
    tables = (cos, sin, cos_c, sin_c, _overlap_table(seq))
    mem_b = mem.reshape(nb * mem_len, d_model).astype(BF16)

    h, hb = _ln_in(x.reshape(nb * seq, d_model), ln_in_g, ln_in_b, tm)
    for l in range(depth):
        s5_params = (s5_lambda_re[l], s5_lambda_im[l], s5_log_dt[l], s5_b_re[l], s5_b_im[l],
                     s5_c_re[l], s5_c_im[l], s5_d[l], s5_w_glu[l], s5_b_glu[l])
        h, hb = _hybrid_mixer(hb, h, nb, seq, w_in[l], s5_params, nsa_cmp_pe[l], nsa_cmp_w1[l],
                              nsa_cmp_w2[l], mix_norm_g[l], w_out[l], ln1_g[l], ln1_b[l], tables, alpha, tm)
        kv = _proj(mem_b, xa_wkv[l].astype(BF16), min(512, nb * mem_len), xa_wkv.shape[2] // 2)
        h, hb = _xattn(hb, h, xa_wq[l], kv, xa_wo[l], ln2_g[l], ln2_b[l], seq, mem_len, alpha, tm)
        h, hb = _ffn(hb, h, ffn_w_gate[l], ffn_w_up[l], ffn_w_down[l], ln3_g[l], ln3_b[l], alpha,
                     tm, 512)
    return h.reshape(nb, seq, d_model)
```

```python
import functools
import math

import numpy as np
import jax
import jax.numpy as jnp
from jax import lax
from jax.experimental import pallas as pl
from jax.experimental.pallas import tpu as pltpu

F32 = jnp.float32
BF16 = jnp.bfloat16

LANES = 128
VMEM_LIMIT = 56 * 1024 * 1024

HEAD_DIM = 64
HALF = HEAD_DIM // 2
ROPE_THETA = 10000.0
LN_EPS = 1e-5
RMS_EPS = 1e-6
SSM_CH = 16
SSM_STATE = 64
NSA_KV_HEADS = 2
NSA_REP = 4
CMP_LEN = 32
CMP_STRIDE = 16
CMP_HIDDEN = 128
SEL_BLOCK = 64
SEL_TOPN = 8
NSA_WINDOW = 512
DIL_CONFIGS = ((128, 1), (512, 4), (2048, 16))
XA_HEADS = 4
XA_HEAD_DIM = 128
Q_TILE = 128
SB_TILE = 256
SB_PAIRS = 4
NSA_TILE = 256
NSA_CHAINS = 2
NEG_BIG = -1e30


def _params(*sem):
    return pltpu.CompilerParams(dimension_semantics=sem, vmem_limit_bytes=VMEM_LIMIT)


def _dot(a, b):
    return jnp.dot(a.astype(BF16), b.astype(BF16), preferred_element_type=F32)


def _dot_nt(a, b):
    return lax.dot_general(a.astype(BF16), b.astype(BF16), (((1,), (1,)), ((), ())),
                           preferred_element_type=F32)


def _dot_split(a, b_exact):
    hi = a.astype(BF16)
    lo = (a - hi.astype(F32)).astype(BF16)
    return (jnp.dot(hi, b_exact, preferred_element_type=F32)
            + jnp.dot(lo, b_exact, preferred_element_type=F32))


def _layer_norm(x, g, b):
    mu = jnp.mean(x, -1, keepdims=True)
    xc = x - mu
    var = jnp.mean(xc * xc, -1, keepdims=True)
    return xc * lax.rsqrt(var + LN_EPS) * g + b


def _lane_iota(shape):
    return lax.broadcasted_iota(jnp.int32, shape, len(shape) - 1)


def _row_iota(shape):
    return lax.broadcasted_iota(jnp.int32, shape, len(shape) - 2)


def _swap_halves(x):
    return pltpu.roll(x, HEAD_DIM, axis=x.ndim - 1)


def _ln_in_kernel(x_ref, g_ref, b_ref, h_ref, hb_ref):
    y = _layer_norm(x_ref[...], g_ref[...], b_ref[...])
    h_ref[...] = y
    hb_ref[...] = y.astype(BF16)


def _ln_in(x2, g, b, tm):
    T, D = x2.shape
    return pl.pallas_call(
        _ln_in_kernel,
        grid=(T // tm,),
        in_specs=[pl.BlockSpec((tm, D), lambda i: (i, 0)),
                  pl.BlockSpec((1, D), lambda i: (0, 0)),
                  pl.BlockSpec((1, D), lambda i: (0, 0))],
        out_specs=[pl.BlockSpec((tm, D), lambda i: (i, 0)),
                   pl.BlockSpec((tm, D), lambda i: (i, 0))],
        out_shape=[jax.ShapeDtypeStruct((T, D), F32), jax.ShapeDtypeStruct((T, D), BF16)],
        compiler_params=_params("parallel"),
        name="ln_in",
    )(x2, g.reshape(1, D), b.reshape(1, D))


def _proj_kernel(a_ref, w_ref, o_ref):
    o_ref[...] = jnp.dot(a_ref[...], w_ref[...], preferred_element_type=F32)


def _proj_rope_kernel(a_ref, w_ref, cos_ref, sin_ref, o_ref):
    acc = jnp.dot(a_ref[...], w_ref[...], preferred_element_type=F32)
    cos = cos_ref[...]
    sin = sin_ref[...]
    first = (_lane_iota((1, LANES)) % HEAD_DIM) < HALF
    for c in range(acc.shape[1] // LANES):
        x = acc[:, c * LANES:(c + 1) * LANES]
        partner = jnp.where(first, pltpu.roll(x, LANES - HALF, axis=1), pltpu.roll(x, HALF, axis=1))
        o_ref[:, c * LANES:(c + 1) * LANES] = x * cos + partner * sin


def _proj(a, w, tm, tn, seq=None, rope=None, time_major_batch=None):
    M, K = a.shape
    N = w.shape[1]
    nm, nn = M // tm, N // tn
    in_specs = [pl.BlockSpec((tm, K), lambda j, i: (i, 0)),
                pl.BlockSpec((K, tn), lambda j, i: (0, j))]
    args = [a, w]
    kern = _proj_kernel
    if rope is not None:
        ns = seq // tm
        in_specs += [pl.BlockSpec((tm, LANES), lambda j, i: (i % ns, 0))] * 2
        args += list(rope)
        kern = _proj_rope_kernel
    if time_major_batch is None:
        out_spec = pl.BlockSpec((tm, tn), lambda j, i: (i, j))
        out_shape = jax.ShapeDtypeStruct((M, N), F32)
    else:
        assert nn == 1
        ns = seq // tm
        out_spec = pl.BlockSpec((tm, N), lambda j, i: (i % ns, i // ns))
        out_shape = jax.ShapeDtypeStruct((seq, time_major_batch * N), F32)
    out = pl.pallas_call(
        kern, grid=(nn, nm), in_specs=in_specs, out_specs=out_spec, out_shape=out_shape,
        compiler_params=_params("parallel", "parallel"), name="proj",
    )(*args)
    if time_major_batch is not None:
        out = out.reshape(seq * time_major_batch, N)
    return out


def _s5_kernel(u_ref, bb_ref, cc_ref, are_ref, aim_ref, d_ref, wg_ref, bg_ref, y_ref,
               buf_ref, st_ref, *, nb, ts):
    n_chunks, cw, sw2 = bb_ref.shape
    sw = sw2 // 2

    @pl.when(pl.program_id(0) == 0)
    def _():
        st_ref[...] = jnp.zeros_like(st_ref)

    u = u_ref[...]
    ys = []
    for c in range(n_chunks):
        re0, im0 = c * sw2, c * sw2 + sw
        buf_ref[:, re0:re0 + sw2] = jnp.dot(u[:, c * cw:(c + 1) * cw].astype(BF16), bb_ref[c],
                                            preferred_element_type=F32)
        a_re = jnp.broadcast_to(are_ref[:, c * sw:(c + 1) * sw], (nb, sw))
        a_im = jnp.broadcast_to(aim_ref[:, c * sw:(c + 1) * sw], (nb, sw))

        def step(t, carry, re0=re0, im0=im0, a_re=a_re, a_im=a_im):
            x_re, x_im = carry
            r = pl.multiple_of(t * nb, nb)
            n_re = a_re * x_re - a_im * x_im + buf_ref[pl.ds(r, nb), re0:re0 + sw]
            n_im = a_re * x_im + a_im * x_re + buf_ref[pl.ds(r, nb), im0:im0 + sw]
            buf_ref[pl.ds(r, nb), re0:re0 + sw] = n_re
            buf_ref[pl.ds(r, nb), im0:im0 + sw] = n_im
            return n_re, n_im

        x_re, x_im = lax.fori_loop(0, ts, step, (st_ref[:, re0:re0 + sw], st_ref[:, im0:im0 + sw]),
                                   unroll=8)
        st_ref[:, re0:re0 + sw] = x_re
        st_ref[:, im0:im0 + sw] = x_im
        ys.append(jnp.dot(buf_ref[:, re0:re0 + sw2].astype(BF16), cc_ref[c], preferred_element_type=F32))

    y = jnp.concatenate(ys, 1) + d_ref[...] * u
    g = jax.nn.gelu(y)
    z = jnp.dot(g.astype(BF16), wg_ref[...], preferred_element_type=F32) + bg_ref[...]
    y_ref[...] = g * jax.nn.sigmoid(z)


def _s5(u_tm, nb, lam_re, lam_im, log_dt, b_re, b_im, c_re, c_im, d_skip, w_glu, b_glu, ts=64):
    R, W = u_tm.shape
    G, P = lam_re.shape
    C = SSM_CH
    lr = jnp.minimum(lam_re, -1e-4)
    li = lam_im
    dt = jnp.exp(log_dt)[:, None]
    mag = jnp.exp(lr * dt)
    a_re = mag * jnp.cos(li * dt)
    a_im = mag * jnp.sin(li * dt)
    den = lr * lr + li * li
    z_re = ((a_re - 1.0) * lr + a_im * li) / den
    z_im = (a_im * lr - (a_re - 1.0) * li) / den
    bb_re = z_re[..., None] * b_re - z_im[..., None] * b_im
    bb_im = z_re[..., None] * b_im + z_im[..., None] * b_re
    gc = LANES // C
    nc = G // gc
    eye = jnp.eye(gc, dtype=F32)

    def block_diag_in(m):
        return jnp.einsum('ngpc,gh->ngchp', m.reshape(nc, gc, P, C), eye).reshape(nc, gc * C, gc * P)

    def block_diag_out(m):
        return jnp.einsum('ngcp,gh->ngphc', m.reshape(nc, gc, C, P), eye).reshape(nc, gc * P, gc * C)

    bb = jnp.concatenate([block_diag_in(bb_re), block_diag_in(bb_im)], 2).astype(BF16)
    cc = jnp.concatenate([block_diag_out(c_re), -block_diag_out(c_im)], 1).astype(BF16)
    ns = G * P
    rows = ts * nb
    kern = functools.partial(_s5_kernel, nb=nb, ts=ts)
    const = lambda i: (0, 0)
    const3 = lambda i: (0, 0, 0)
    return pl.pallas_call(
        kern,
        grid=(R // rows,),
        in_specs=[pl.BlockSpec((rows, W), lambda i: (i, 0)),
                  pl.BlockSpec(bb.shape, const3),
                  pl.BlockSpec(cc.shape, const3),
                  pl.BlockSpec((1, ns), const),
                  pl.BlockSpec((1, ns), const),
                  pl.BlockSpec((1, W), const),
                  pl.BlockSpec((W, W), const),
                  pl.BlockSpec((1, W), const)],
        out_specs=pl.BlockSpec((rows, W), lambda i: (i, 0)),
        out_shape=jax.ShapeDtypeStruct((R, W), F32),
        scratch_shapes=[pltpu.VMEM((rows, 2 * ns), F32), pltpu.VMEM((nb, 2 * ns), F32)],
        compiler_params=_params("arbitrary"),
        name="s5",
    )(u_tm, bb, cc, a_re.reshape(1, ns), a_im.reshape(1, ns), d_skip.reshape(1, W),
      w_glu.astype(BF16), b_glu.reshape(1, W))


def _stack_pair(q):
    first = _lane_iota((1, LANES)) < HEAD_DIM
    return jnp.concatenate([jnp.where(first, q, 0.0), jnp.where(first, 0.0, q)], 0)


def _unstack_pair(x):
    t = x.shape[0] // 2
    first = _lane_iota((1, LANES)) < HEAD_DIM
    return jnp.where(first, x[:t], x[t:])


def _sb_kernel(q_ref, k_ref, v_ref, o_ref):
    tq = q_ref.shape[0]
    qi = pl.program_id(2)
    n_pairs = q_ref.shape[1] // LANES
    groups = [slice(p * LANES, (p + 1) * LANES) for p in range(n_pairs)]
    qs = [_stack_pair(q_ref[:, g]).astype(BF16) for g in groups]
    tri = (_row_iota((tq, tq)) > _lane_iota((tq, tq))).astype(BF16)
    diag = _lane_iota((2 * tq, tq)) < (_row_iota((2 * tq, tq)) % tq)

    def tile(kj, carry, masked):
        r = pl.multiple_of(kj * tq, tq)
        zs = [_dot_nt(q, k_ref[pl.ds(r, tq), g]) for q, g in zip(qs, groups)]
        lfs, afters = [], []
        for z in zs:
            lf = -(jnp.maximum(z, 0.0) + jnp.log(1.0 + jnp.exp(-jnp.abs(z))))
            if masked:
                lf = jnp.where(diag, lf, 0.0)
            lfs.append(lf)
            afters.append(_dot_split(lf, tri))
        out = []
        for z, lf, after, g, (acc, tail) in zip(zs, lfs, afters, groups, carry):
            w = jnp.exp(z + lf + (after + tail))
            if masked:
                w = jnp.where(diag, w, 0.0)
            out.append((acc + _dot(w, v_ref[pl.ds(r, tq), g]), tail + jnp.sum(lf, -1, keepdims=True)))
        return tuple(out)

    init = tuple((jnp.zeros((2 * tq, LANES), F32), jnp.zeros((2 * tq, 1), F32)) for _ in groups)
    carry = tile(qi, init, True)
    carry = lax.fori_loop(0, qi, lambda i, c: tile(qi - 1 - i, c, False), carry)
    for g, (acc, _) in zip(groups, carry):
        o_ref[:, g] = _unstack_pair(acc)


def _stick_breaking(qkv, nb, seq, q_col, k_col, v_col, n_pairs):
    tq = min(SB_TILE, seq)
    nq = seq // tq
    per = SB_PAIRS
    w = per * LANES
    assert n_pairs % per == 0 and q_col % per == 0 and k_col % per == 0 and v_col % per == 0
    return pl.pallas_call(
        _sb_kernel,
        grid=(nb, n_pairs // per, nq),
        in_specs=[pl.BlockSpec((tq, w), lambda b, p, i: (b * nq + i, q_col // per + p)),
                  pl.BlockSpec((seq, w), lambda b, p, i: (b, k_col // per + p)),
                  pl.BlockSpec((seq, w), lambda b, p, i: (b, v_col // per + p))],
        out_specs=pl.BlockSpec((tq, w), lambda b, p, i: (b * nq + i, p)),
        out_shape=jax.ShapeDtypeStruct((nb * seq, n_pairs * LANES), F32),
        compiler_params=_params("parallel", "parallel", "arbitrary"),
        name="stick_breaking",
    )(qkv, qkv, qkv)


def _dil_kernel(q_ref, k_ref, v_ref, o_ref, m_ref, l_ref, a_ref):
    seq = q_ref.shape[0]
    tq = Q_TILE
    first = _lane_iota((1, LANES)) < HEAD_DIM

    def tile(c, dil, wd, n_tiles, r, i):
        base = r + dil * tq * i

        def rows(ref, start):
            if dil == 1:
                return ref[pl.ds(pl.multiple_of(start, tq), tq), :]
            return ref[pl.ds(start, tq, stride=dil), :]

        qs = _stack_pair(rows(q_ref, base)).astype(BF16)
        q_idx = i * tq + (_row_iota((2 * tq, 1)) % tq)
        if n_tiles > 1:
            prev = r + dil * tq * jnp.maximum(i - 1, 0)
            kk = jnp.concatenate([rows(k_ref, prev), rows(k_ref, base)], 0)
            vv = jnp.concatenate([rows(v_ref, prev), rows(v_ref, base)], 0)
            k_idx = (i - 1) * tq + _lane_iota((1, 2 * tq))
        else:
            kk = rows(k_ref, base)
            vv = rows(v_ref, base)
            k_idx = _lane_iota((1, tq))
        valid = (k_idx >= 0) & (k_idx <= q_idx) & (q_idx - k_idx <= wd)
        s = jnp.where(valid, _dot_nt(qs, kk), NEG_BIG)
        m = jnp.max(s, -1, keepdims=True)
        p = jnp.where(valid, jnp.exp(s - m), 0.0)
        l = jnp.sum(p, -1, keepdims=True)
        acc = _dot(p, vv)
        m2 = jnp.where(first, m[:tq], m[tq:])
        l2 = jnp.where(first, l[:tq], l[tq:])
        a2 = jnp.where(first, acc[:tq], acc[tq:])
        if dil == 1:
            sl = pl.ds(pl.multiple_of(base, tq), tq)
        else:
            sl = pl.ds(base, tq, stride=dil)
        m_ref[c, sl, :] = m2
        l_ref[c, sl, :] = l2
        a_ref[c, sl, :] = a2

    for c, (window, dil) in enumerate(DIL_CONFIGS):
        wd = window // dil
        n_tiles = seq // dil // tq

        def per_tile(n, _, c=c, dil=dil, wd=wd, n_tiles=n_tiles):
            tile(c, dil, wd, n_tiles, n // n_tiles, n % n_tiles)
            return 0

        lax.fori_loop(0, dil * n_tiles, per_tile, 0, unroll=4)

    def combine(i, _):
        sl = pl.ds(pl.multiple_of(i * tq, tq), tq)
        m0, m1, m2 = m_ref[0, sl, :], m_ref[1, sl, :], m_ref[2, sl, :]
        mx = jnp.maximum(jnp.maximum(m0, m1), m2)
        e0, e1, e2 = jnp.exp(m0 - mx), jnp.exp(m1 - mx), jnp.exp(m2 - mx)
        num = e0 * a_ref[0, sl, :] + e1 * a_ref[1, sl, :] + e2 * a_ref[2, sl, :]
        den = e0 * l_ref[0, sl, :] + e1 * l_ref[1, sl, :] + e2 * l_ref[2, sl, :]
        o_ref[sl, :] = num / den
        return 0

    lax.fori_loop(0, seq // tq, combine, 0)


def _dilated(q_arr, q_col, k_arr, k_col, v_arr, v_col, nb, seq, n_pairs):
    return pl.pallas_call(
        _dil_kernel,
        grid=(nb, n_pairs),
        in_specs=[pl.BlockSpec((seq, LANES), lambda b, p: (b, q_col + p)),
                  pl.BlockSpec((seq, LANES), lambda b, p: (b, k_col + p)),
                  pl.BlockSpec((seq, LANES), lambda b, p: (b, v_col + p))],
        out_specs=pl.BlockSpec((seq, LANES), lambda b, p: (b, p)),
        out_shape=jax.ShapeDtypeStruct((nb * seq, n_pairs * LANES), F32),
        scratch_shapes=[pltpu.VMEM((3, seq, LANES), F32)] * 3,
        compiler_params=_params("parallel", "parallel"),
        name="dilated",
    )(q_arr, k_arr, v_arr)


def _cmp_kernel(t_ref, pe_ref, w1a_ref, w1b_ref, w2_ref, cos_ref, sin_ref, o_ref):
    nblk = t_ref.shape[0] // CMP_STRIDE
    j = pl.program_id(1)
    out = jnp.zeros((nblk, LANES), F32)
    for g in range(NSA_KV_HEADS):
        p1 = jnp.zeros((nblk, CMP_HIDDEN), F32)
        p2 = jnp.zeros((nblk, CMP_HIDDEN), F32)
        for l in range(CMP_STRIDE):
            x = t_ref[pl.ds(l, nblk, stride=CMP_STRIDE), :]
            p1 = p1 + _dot(x + pe_ref[0, l:l + 1, :], w1a_ref[0, g, l])
            p2 = p2 + _dot(x + pe_ref[0, CMP_STRIDE + l:CMP_STRIDE + l + 1, :], w1b_ref[0, g, l])
        hidden = p1 + pltpu.roll(p2, nblk - 1, axis=0)
        out = out + _dot(jax.nn.gelu(hidden), w2_ref[0, g])
    first = (_lane_iota((1, LANES)) % HEAD_DIM) < HALF
    partner = jnp.where(first, pltpu.roll(out, LANES - HALF, axis=1), pltpu.roll(out, HALF, axis=1))
    roped = out * cos_ref[...] + partner * sin_ref[...]
    o_ref[0] = jnp.where(j == 0, roped, out)


def _compress(p_arr, col0, nb, seq, pe, w1, w2, cos_c, sin_c):
    nblk = seq // CMP_STRIDE
    G = NSA_KV_HEADS
    pe2 = jnp.tile(pe, (1, 1, G))
    w1r = w1.reshape(2, CMP_LEN, HEAD_DIM, CMP_HIDDEN)
    w1e = jnp.zeros((2, G, CMP_LEN, LANES, CMP_HIDDEN), F32)
    w2e = jnp.zeros((2, G, CMP_HIDDEN, LANES), F32)
    for g in range(G):
        w1e = w1e.at[:, g, :, g * HEAD_DIM:(g + 1) * HEAD_DIM, :].set(w1r)
        w2e = w2e.at[:, g, :, g * HEAD_DIM:(g + 1) * HEAD_DIM].set(w2)
    w1e = w1e.astype(BF16)
    w2e = w2e.astype(BF16)
    return pl.pallas_call(
        _cmp_kernel,
        grid=(nb, 2),
        in_specs=[pl.BlockSpec((seq, LANES), lambda b, j: (b, col0 + j)),
                  pl.BlockSpec((1, CMP_LEN, LANES), lambda b, j: (j, 0, 0)),
                  pl.BlockSpec((1, G, CMP_STRIDE, LANES, CMP_HIDDEN), lambda b, j: (j, 0, 0, 0, 0)),
                  pl.BlockSpec((1, G, CMP_STRIDE, LANES, CMP_HIDDEN), lambda b, j: (j, 0, 1, 0, 0)),
                  pl.BlockSpec((1, G, CMP_HIDDEN, LANES), lambda b, j: (j, 0, 0, 0)),
                  pl.BlockSpec((nblk, LANES), lambda b, j: (0, 0)),
                  pl.BlockSpec((nblk, LANES), lambda b, j: (0, 0))],
        out_specs=pl.BlockSpec((1, nblk, LANES), lambda b, j: (j, b, 0)),
        out_shape=jax.ShapeDtypeStruct((2, nb * nblk, LANES), F32),
        compiler_params=_params("parallel", "parallel"),
        name="nsa_compress",
    )(p_arr, pe2, w1e, w1e, w2e, cos_c, sin_c)


def _nsa_kernel(q_ref, gate_ref, kc_ref, vc_ref, ks_ref, vs_ref, kw_ref, vw_ref, ovt_ref, o_ref):
    tq = q_ref.shape[0]
    R = NSA_REP
    seq = ks_ref.shape[0]
    n_sel = seq // SEL_BLOCK
    n_cmp = (seq - CMP_LEN) // CMP_STRIDE + 1
    g = pl.program_id(1)
    qi = pl.program_id(2)
    lane = _lane_iota((1, LANES))
    mine = (lane // HEAD_DIM) == g

    parts = []
    for h in range(R):
        x = q_ref[:, (h // 2) * LANES:(h // 2 + 1) * LANES]
        x = jnp.where(g == (h % 2), x, _swap_halves(x))
        parts.append(jnp.where(mine, x, 0.0))
    qs = jnp.concatenate(parts, 0).astype(BF16)

    t_row = qi * tq + _row_iota((tq, 1))

    nblk = kc_ref.shape[1]
    c_idx = _lane_iota((tq, nblk))
    cmask = (c_idx < n_cmp) & (c_idx * CMP_STRIDE + (CMP_LEN - 1) <= qi * tq + _row_iota((tq, nblk)))
    s_c = _dot_nt(qs, kc_ref[0]).reshape(R, tq, nblk)
    s_c = jnp.where(cmask, s_c, NEG_BIG)
    m_c = jnp.max(s_c, -1, keepdims=True)
    m_c = jnp.where(m_c > 0.5 * NEG_BIG, m_c, 0.0)
    p_c = jnp.where(cmask, jnp.exp(s_c - m_c), 0.0)
    p_c = p_c / jnp.maximum(jnp.sum(p_c, -1, keepdims=True), 1e-30)
    o_c = _dot(p_c.reshape(R * tq, nblk), vc_ref[0])

    psum = jnp.sum(p_c, 0)
    p_hi = psum.astype(BF16)
    p_lo = (psum - p_hi.astype(F32)).astype(BF16)
    imp = (_dot_nt(ovt_ref[...], p_hi) + _dot_nt(ovt_ref[...], p_lo))[:n_sel]
    n_idx = _row_iota((n_sel, tq))
    t_q = qi * tq + _lane_iota((n_sel, tq))
    cur = t_q // SEL_BLOCK
    imp = jnp.where(n_idx * SEL_BLOCK > t_q, -1.0, imp)
    forced = (n_idx == 0) | (n_idx == cur) | (n_idx == cur - 1)
    imp = jnp.where(forced, 1e9, imp)
    rank = jnp.zeros((n_sel, tq), F32)
    for m in range(n_sel):
        row = imp[m:m + 1, :]
        earlier = (n_idx > m).astype(F32)
        rank = rank + jnp.where(row > imp, 1.0, jnp.where(row == imp, earlier, 0.0))
    top_n = min(SEL_TOPN, n_sel)
    sel_t = jnp.where((rank < top_n) & (imp > -0.5), 1.0, 0.0)
    sel_t = jnp.concatenate([sel_t, jnp.zeros((LANES - n_sel, tq), F32)], 0)
    sel = sel_t.T.astype(BF16)

    k_col = _lane_iota((tq, tq))
    blk_row = _row_iota((LANES, tq))
    blk_of_key = _lane_iota((LANES, tq)) // SEL_BLOCK

    def sel_bias(j, causal):
        expand = (blk_row == (j * (tq // SEL_BLOCK) + blk_of_key)).astype(BF16)
        chosen = jnp.dot(sel, expand, preferred_element_type=F32)
        bias = (1.0 - chosen) * NEG_BIG
        if causal:
            bias = jnp.where((j * tq + k_col) <= t_row, bias, NEG_BIG)
        return bias

    def win_bias(j):
        diff = t_row - (j * tq + k_col)
        return jnp.where((diff >= 0) & (diff < NSA_WINDOW), 0.0, NEG_BIG)

    hp = R // NSA_CHAINS
    q_parts = [qs[c * hp * tq:(c + 1) * hp * tq] for c in range(NSA_CHAINS)]

    def scores(k_ref, j):
        kt = k_ref[pl.ds(pl.multiple_of(j * tq, tq), tq), :].astype(BF16)
        return [_dot_nt(q, kt).reshape(hp, tq, tq) for q in q_parts]

    def update(carry, s_parts, v_ref, j):
        vt = jnp.where(mine, v_ref[pl.ds(pl.multiple_of(j * tq, tq), tq), :], 1.0).astype(BF16)
        out = []
        for (m_run, acc), s in zip(carry, s_parts):
            m_new = jnp.maximum(m_run, jnp.max(s, -1, keepdims=True))
            alpha = jnp.exp(m_run - m_new)
            p = jnp.exp(s - m_new)
            out.append((m_new, alpha * acc + _dot(p.reshape(hp * tq, tq), vt).reshape(hp, tq, LANES)))
        return tuple(out)

    def normalise(carry):
        acc = jnp.concatenate([a for _, a in carry], 0)
        return acc / jnp.where(mine, _swap_halves(acc), 1.0)

    init = tuple((jnp.full((hp, tq, 1), NEG_BIG, F32), jnp.zeros((hp, tq, LANES), F32))
                 for _ in range(NSA_CHAINS))
    j_win = jnp.maximum(qi - NSA_WINDOW // tq, 0)

    def selected_only(j, carry):
        s_parts = scores(ks_ref, j)
        bias = sel_bias(j, False)
        return update(carry, [s + bias for s in s_parts], vs_ref, j)

    def selected_and_window(j, carry):
        s_s, s_w = scores(ks_ref, j), scores(kw_ref, j)
        b_s, b_w = sel_bias(j, True), win_bias(j)
        return (update(carry[0], [s + b_s for s in s_s], vs_ref, j),
                update(carry[1], [s + b_w for s in s_w], vw_ref, j))

    c_sel = lax.fori_loop(0, j_win, selected_only, init)
    c_sel, c_win = lax.fori_loop(j_win, qi + 1, selected_and_window, (c_sel, init))
    o_s, o_w = normalise(c_sel), normalise(c_win)
    o_c = o_c.reshape(R, tq, LANES)

    gate = jax.nn.sigmoid(gate_ref[...])
    outs = []
    for h in range(R):
        g_c = gate[:, 3 * h:3 * h + 1]
        g_s = gate[:, 3 * h + 1:3 * h + 2]
        g_w = gate[:, 3 * h + 2:3 * h + 3]
        y = g_c * o_c[h] + g_s * o_s[h] + g_w * o_w[h]
        outs.append(jnp.where(g == (h % 2), y, _swap_halves(y)))
    first = lane < HEAD_DIM
    for pair in range(R // 2):
        o_ref[:, pair * LANES:(pair + 1) * LANES] = jnp.where(first, outs[2 * pair], outs[2 * pair + 1])


def _nsa(r_arr, p_arr, cmp_kv, overlap, nb, seq, cols):
    tq = min(NSA_TILE, seq)
    nq = seq // tq
    nblk = seq // CMP_STRIDE
    G = NSA_KV_HEADS
    qw = NSA_REP * HEAD_DIM
    return pl.pallas_call(
        _nsa_kernel,
        grid=(nb, G, nq),
        in_specs=[pl.BlockSpec((tq, qw), lambda b, g, i: (b * nq + i, cols['q'] * LANES // qw + g)),
                  pl.BlockSpec((tq, LANES), lambda b, g, i: (b * nq + i, cols['gate'] + g)),
                  pl.BlockSpec((1, nblk, LANES), lambda b, g, i: (0, b, 0)),
                  pl.BlockSpec((1, nblk, LANES), lambda b, g, i: (1, b, 0)),
                  pl.BlockSpec((seq, LANES), lambda b, g, i: (b, cols['k_slc'])),
                  pl.BlockSpec((seq, LANES), lambda b, g, i: (b, cols['v_slc'])),
                  pl.BlockSpec((seq, LANES), lambda b, g, i: (b, cols['k_win'])),
                  pl.BlockSpec((seq, LANES), lambda b, g, i: (b, cols['v_win'])),
                  pl.BlockSpec((LANES, nblk), lambda b, g, i: (0, 0))],
        out_specs=pl.BlockSpec((tq, qw), lambda b, g, i: (b * nq + i, g)),
        out_shape=jax.ShapeDtypeStruct((nb * seq, G * qw), F32),
        compiler_params=_params("parallel", "parallel", "arbitrary"),
        name="nsa",
    )(r_arr, p_arr, cmp_kv, cmp_kv, r_arr, p_arr, r_arr, p_arr, overlap)


def _mix_out_kernel(ya_ref, yb_ref, yc_ref, yd_ref, ng_ref, w_ref, h_ref, g_ref, b_ref,
                    o_ref, ob_ref, *, alpha):
    ys = []
    for n, y_ref in enumerate((ya_ref, yb_ref, yc_ref, yd_ref)):
        y = y_ref[...]
        y = y * lax.rsqrt(jnp.mean(y * y, -1, keepdims=True) + RMS_EPS) * ng_ref[n:n + 1, :]
        ys.append(y.astype(BF16))
    f = jnp.dot(jnp.concatenate(ys, 1), w_ref[...], preferred_element_type=F32)
    out = _layer_norm(alpha * h_ref[...] + f, g_ref[...], b_ref[...])
    o_ref[...] = out
    ob_ref[...] = out.astype(BF16)


def _mix_out(ya_tm, yb, yc, yd, norm_g, w_out, h, ln_g, ln_b, nb, seq, alpha, tm):
    T, D = h.shape
    gw = yb.shape[1]
    ns = seq // tm
    ya_view = ya_tm.reshape(seq, nb * gw)
    tok = lambda i: (i, 0)
    const = lambda i: (0, 0)
    return pl.pallas_call(
        functools.partial(_mix_out_kernel, alpha=alpha),
        grid=(T // tm,),
        in_specs=[pl.BlockSpec((tm, gw), lambda i: (i % ns, i // ns)),
                  pl.BlockSpec((tm, gw), tok), pl.BlockSpec((tm, gw), tok), pl.BlockSpec((tm, gw), tok),
                  pl.BlockSpec((4, gw), const),
                  pl.BlockSpec((4 * gw, D), const),
                  pl.BlockSpec((tm, D), tok),
                  pl.BlockSpec((1, D), const), pl.BlockSpec((1, D), const)],
        out_specs=[pl.BlockSpec((tm, D), tok), pl.BlockSpec((tm, D), tok)],
        out_shape=[jax.ShapeDtypeStruct((T, D), F32), jax.ShapeDtypeStruct((T, D), BF16)],
        compiler_params=_params("parallel"),
        name="mix_out",
    )(ya_view, yb, yc, yd, norm_g, w_out.astype(BF16), h, ln_g.reshape(1, D), ln_b.reshape(1, D))


def _xattn_kernel(hb_ref, h_ref, wq_ref, kv_ref, wo_ref, g_ref, b_ref, o_ref, ob_ref, *, alpha):
    width = wq_ref.shape[1]
    q = jnp.dot(hb_ref[...], wq_ref[...], preferred_element_type=F32)
    heads = []
    for hd in range(XA_HEADS):
        lo = hd * XA_HEAD_DIM
        k = kv_ref[:, lo:lo + XA_HEAD_DIM]
        v = kv_ref[:, width + lo:width + lo + XA_HEAD_DIM]
        s = _dot_nt(q[:, lo:lo + XA_HEAD_DIM], k) * (XA_HEAD_DIM ** -0.5)
        p = jnp.exp(s - jnp.max(s, -1, keepdims=True))
        p = p / jnp.sum(p, -1, keepdims=True)
        heads.append(_dot(p, v).astype(BF16))
    f = jnp.dot(jnp.concatenate(heads, 1), wo_ref[...], preferred_element_type=F32)
    out = _layer_norm(alpha * h_ref[...] + f, g_ref[...], b_ref[...])
    o_ref[...] = out
    ob_ref[...] = out.astype(BF16)


def _xattn(hb, h, wq, kv, wo, ln_g, ln_b, seq, mem_len, alpha, tm):
    T, D = h.shape
    width = wq.shape[1]
    ns = seq // tm
    tok = lambda i: (i, 0)
    const = lambda i: (0, 0)
    return pl.pallas_call(
        functools.partial(_xattn_kernel, alpha=alpha),
        grid=(T // tm,),
        in_specs=[pl.BlockSpec((tm, D), tok), pl.BlockSpec((tm, D), tok),
                  pl.BlockSpec((D, width), const),
                  pl.BlockSpec((mem_len, 2 * width), lambda i: (i // ns, 0)),
                  pl.BlockSpec((width, D), const),
                  pl.BlockSpec((1, D), const), pl.BlockSpec((1, D), const)],
        out_specs=[pl.BlockSpec((tm, D), tok), pl.BlockSpec((tm, D), tok)],
        out_shape=[jax.ShapeDtypeStruct((T, D), F32), jax.ShapeDtypeStruct((T, D), BF16)],
        compiler_params=_params("parallel"),
        name="cross_attn",
    )(hb, h, wq.astype(BF16), kv, wo.astype(BF16), ln_g.reshape(1, D), ln_b.reshape(1, D))


def _ffn_kernel(hb_ref, h_ref, wg_ref, wu_ref, wd_ref, g_ref, b_ref, o_ref, ob_ref, acc_ref, *, alpha):
    j = pl.program_id(1)

    @pl.when(j == 0)
    def _():
        acc_ref[...] = alpha * h_ref[...]

    x = hb_ref[...]
    gate = jnp.dot(x, wg_ref[...], preferred_element_type=F32)
    up = jnp.dot(x, wu_ref[...], preferred_element_type=F32)
    act = (jax.nn.silu(gate) * up).astype(BF16)
    acc_ref[...] += jnp.dot(act, wd_ref[...], preferred_element_type=F32)

    @pl.when(j == pl.num_programs(1) - 1)
    def _():
        out = _layer_norm(acc_ref[...], g_ref[...], b_ref[...])
        o_ref[...] = out
        ob_ref[...] = out.astype(BF16)


def _ffn(hb, h, wg, wu, wd, ln_g, ln_b, alpha, tm, th):
    T, D = h.shape
    H = wg.shape[1]
    tok = lambda i, j: (i, 0)
    const = lambda i, j: (0, 0)
    return pl.pallas_call(
        functools.partial(_ffn_kernel, alpha=alpha),
        grid=(T // tm, H // th),
        in_specs=[pl.BlockSpec((tm, D), tok), pl.BlockSpec((tm, D), tok),
                  pl.BlockSpec((D, th), lambda i, j: (0, j)),
                  pl.BlockSpec((D, th), lambda i, j: (0, j)),
                  pl.BlockSpec((th, D), lambda i, j: (j, 0)),
                  pl.BlockSpec((1, D), const), pl.BlockSpec((1, D), const)],
        out_specs=[pl.BlockSpec((tm, D), tok), pl.BlockSpec((tm, D), tok)],
        out_shape=[jax.ShapeDtypeStruct((T, D), F32), jax.ShapeDtypeStruct((T, D), BF16)],
        scratch_shapes=[pltpu.VMEM((tm, D), F32)],
        compiler_params=_params("parallel", "arbitrary"),
        name="ffn",
    )(hb, h, wg.astype(BF16), wu.astype(BF16), wd.astype(BF16), ln_g.reshape(1, D), ln_b.reshape(1, D))


def _rope_tables(pos):
    inv_freq = ROPE_THETA ** (-jnp.arange(HALF, dtype=F32) / HALF)
    ang = pos.astype(F32)[:, None] * inv_freq[None, :]
    cos = jnp.tile(jnp.cos(ang), (1, LANES // HALF))
    sin = jnp.sin(ang)
    sin = jnp.tile(jnp.concatenate([-sin, sin], 1), (1, LANES // HEAD_DIM))
    return cos, sin


def _overlap_table(seq):
    n_cmp = (seq - CMP_LEN) // CMP_STRIDE + 1
    n_sel = seq // SEL_BLOCK
    ci = np.arange(n_cmp)[:, None] * CMP_STRIDE
    sj = np.arange(n_sel)[None, :] * SEL_BLOCK
    ov = np.clip(np.minimum(ci + CMP_LEN, sj + SEL_BLOCK) - np.maximum(ci, sj), 0, None) / CMP_LEN
    full = np.zeros((LANES, seq // CMP_STRIDE), np.float32)
    full[:n_sel, :n_cmp] = ov.T
    return jnp.asarray(full, dtype=BF16)


def _split_w_in(w_in, width):
    hw = width // 4
    kvw = NSA_KV_HEADS * HEAD_DIM
    n_gate = 3 * (hw // HEAD_DIM)
    o = np.cumsum([0, hw, hw, 6 * kvw, n_gate, 3 * hw, 3 * hw])
    u = w_in[:, o[0]:o[1]]
    nq = w_in[:, o[1]:o[2]]
    kv = [w_in[:, o[2] + j * kvw:o[2] + (j + 1) * kvw] for j in range(6)]
    gate = w_in[:, o[3]:o[4]]
    sb = w_in[:, o[4]:o[5]]
    dil = w_in[:, o[5]:o[6]]
    scale = HEAD_DIM ** -0.5
    per_group = 3 * NSA_REP
    gates = [jnp.pad(gate[:, g * per_group:(g + 1) * per_group], ((0, 0), (0, LANES - per_group)))
             for g in range(NSA_KV_HEADS)]
    w_rope = jnp.concatenate([nq * scale, kv[2], kv[4], dil[:, :hw] * scale, dil[:, hw:2 * hw]], 1)
    w_plain = jnp.concatenate([sb[:, :hw] * scale, sb[:, hw:], dil[:, 2 * hw:],
                               kv[0], kv[1], kv[3], kv[5]] + gates, 1)
    return u.astype(BF16), w_rope.astype(BF16), w_plain.astype(BF16)


R_NQ, R_KSLC, R_KWIN, R_DILQ, R_DILK = 0, 4, 5, 6, 10
P_SBQ, P_SBK, P_SBV, P_DILV, P_KCMP, P_VSLC, P_VWIN, P_GATE = 0, 4, 8, 12, 16, 18, 19, 20


def _hybrid_mixer(hb, h, nb, seq, w_in, s5_params, cmp_pe, cmp_w1, cmp_w2, norm_g, w_out,
                  ln_g, ln_b, tables, alpha, tm):
    cos, sin, cos_c, sin_c, overlap = tables
    width = w_out.shape[0]
    n_pairs = width // 4 // LANES
    w_u, w_rope, w_plain = _split_w_in(w_in, width)
    u_tm = _proj(hb, w_u, tm, w_u.shape[1], seq=seq, time_major_batch=nb)
    r_arr = _proj(hb, w_rope, tm, w_rope.shape[1] // 2, seq=seq, rope=(cos, sin))
    p_arr = _proj(hb, w_plain, tm, w_plain.shape[1] // 2)
    y_a = _s5(u_tm, nb, *s5_params)
    cmp_kv = _compress(p_arr, P_KCMP, nb, seq, cmp_pe, cmp_w1, cmp_w2, cos_c, sin_c)
    y_b = _nsa(r_arr, p_arr, cmp_kv, overlap, nb, seq,
               dict(q=R_NQ, gate=P_GATE, k_slc=R_KSLC, v_slc=P_VSLC, k_win=R_KWIN, v_win=P_VWIN))
    y_c = _stick_breaking(p_arr, nb, seq, P_SBQ, P_SBK, P_SBV, n_pairs)
    y_d = _dilated(r_arr, R_DILQ, r_arr, R_DILK, p_arr, P_DILV, nb, seq, n_pairs)
    return _mix_out(y_a, y_b, y_c, y_d, norm_g, w_out, h, ln_g, ln_b, nb, seq, alpha, tm)


def kernel(x, mem, ln_in_g, ln_in_b, w_in, s5_lambda_re, s5_lambda_im, s5_log_dt, s5_b_re, s5_b_im, s5_c_re, s5_c_im, s5_d, s5_w_glu, s5_b_glu, nsa_cmp_pe, nsa_cmp_w1, nsa_cmp_w2, mix_norm_g, w_out, ln1_g, ln1_b, xa_wq, xa_wkv, xa_wo, ln2_g, ln2_b, ffn_w_gate, ffn_w_up, ffn_w_down, ln3_g, ln3_b):
    nb, seq, d_model = x.shape
    mem_len = mem.shape[1]
    depth = w_in.shape[0]
    alpha = (2 * depth) ** 0.25
    tm = min(512, seq)
    pos = jnp.arange(seq)
    cos, sin = _rope_tables(pos)
    nblk = seq // CMP_STRIDE
    cos_c, sin_c = _rope_tables(jnp.arange(nblk) * CMP_STRIDE + CMP_LEN - 1)
    tables = (cos, sin, cos_c, sin_c, _overlap_table(seq))
    mem_b = mem.reshape(nb * mem_len, d_model).astype(BF16)

    h, hb = _ln_in(x.reshape(nb * seq, d_model), ln_in_g, ln_in_b, tm)
    for l in range(depth):
        s5_params = (s5_lambda_re[l], s5_lambda_im[l], s5_log_dt[l], s5_b_re[l], s5_b_im[l],
                     s5_c_re[l], s5_c_im[l], s5_d[l], s5_w_glu[l], s5_b_glu[l])
        h, hb = _hybrid_mixer(hb, h, nb, seq, w_in[l], s5_params, nsa_cmp_pe[l], nsa_cmp_w1[l],
                              nsa_cmp_w2[l], mix_norm_g[l], w_out[l], ln1_g[l], ln1_b[l], tables, alpha, tm)
        kv = _proj(mem_b, xa_wkv[l].astype(BF16), min(512, nb * mem_len), xa_wkv.shape[2] // 2)
        h, hb = _xattn(hb, h, xa_wq[l], kv, xa_wo[l], ln2_g[l], ln2_b[l], seq, mem_len, alpha, tm)
        h, hb = _ffn(hb, h, ffn_w_gate[l], ffn_w_up[l], ffn_w_down[l], ln3_g[l], ln3_b[l], alpha,
                     tm, 512)
    return h.reshape(nb, seq, d_model)
```

```python
import functools
import math

import numpy as np
import jax
import jax.numpy as jnp
from jax import lax
from jax.experimental import pallas as pl
from jax.experimental.pallas import tpu as pltpu

F32 = jnp.float32
BF16 = jnp.bfloat16

LANES = 128
MXU_WIDTH = 256
VMEM_LIMIT = 56 * 1024 * 1024

HEAD_DIM = 64
HALF = HEAD_DIM // 2
ROPE_THETA = 10000.0
LN_EPS = 1e-5
RMS_EPS = 1e-6
SSM_CH = 16
SSM_STATE = 64
NSA_KV_HEADS = 2
NSA_REP = 4
CMP_LEN = 32
CMP_STRIDE = 16
CMP_HIDDEN = 128
SEL_BLOCK = 64
SEL_TOPN = 8
NSA_WINDOW = 512
DIL_CONFIGS = ((128, 1), (512, 4), (2048, 16))
XA_HEADS = 4
XA_HEAD_DIM = 128
Q_TILE = 128
SB_TILE = 256
SB_PAIRS = 4
NSA_TILE = 256
NSA_CHAINS = 2
NEG_BIG = -1e30


def _params(*sem):
    return pltpu.CompilerParams(dimension_semantics=sem, vmem_limit_bytes=VMEM_LIMIT)


def _dot(a, b):
    return jnp.dot(a.astype(BF16), b.astype(BF16), preferred_element_type=F32)


def _dot_nt(a, b):
    return lax.dot_general(a.astype(BF16), b.astype(BF16), (((1,), (1,)), ((), ())),
                           preferred_element_type=F32)


def _layer_norm(x, g, b):
    mu = jnp.mean(x, -1, keepdims=True)
    xc = x - mu
    var = jnp.mean(xc * xc, -1, keepdims=True)
    return xc * lax.rsqrt(var + LN_EPS) * g + b


def _lane_iota(shape):
    return lax.broadcasted_iota(jnp.int32, shape, len(shape) - 1)


def _row_iota(shape):
    return lax.broadcasted_iota(jnp.int32, shape, len(shape) - 2)


def _swap_halves(x):
    return pltpu.roll(x, HEAD_DIM, axis=x.ndim - 1)


def _ln_in_kernel(x_ref, g_ref, b_ref, h_ref, hb_ref):
    y = _layer_norm(x_ref[...], g_ref[...], b_ref[...])
    h_ref[...] = y
    hb_ref[...] = y.astype(BF16)


def _ln_in(x2, g, b, tm):
    T, D = x2.shape
    return pl.pallas_call(
        _ln_in_kernel,
        grid=(T // tm,),
        in_specs=[pl.BlockSpec((tm, D), lambda i: (i, 0)),
                  pl.BlockSpec((1, D), lambda i: (0, 0)),
                  pl.BlockSpec((1, D), lambda i: (0, 0))],
        out_specs=[pl.BlockSpec((tm, D), lambda i: (i, 0)),
                   pl.BlockSpec((tm, D), lambda i: (i, 0))],
        out_shape=[jax.ShapeDtypeStruct((T, D), F32), jax.ShapeDtypeStruct((T, D), BF16)],
        compiler_params=_params("parallel"),
        name="ln_in",
    )(x2, g.reshape(1, D), b.reshape(1, D))


def _row_chunks(n, parts=2):
    step = n // parts
    return [slice(i * step, (i + 1) * step) for i in range(parts)]


def _column_chunks(n):
    return [(lo, min(lo + MXU_WIDTH, n)) for lo in range(0, n, MXU_WIDTH)]


def _proj_kernel(a_ref, w_ref, o_ref):
    a = a_ref[...]
    for lo, hi in _column_chunks(o_ref.shape[1]):
        o_ref[:, lo:hi] = jnp.dot(a, w_ref[:, lo:hi], preferred_element_type=F32)


def _proj_rope_kernel(a_ref, w_ref, cos_ref, sin_ref, o_ref):
    a = a_ref[...]
    cos = cos_ref[...]
    sin = sin_ref[...]
    first = (_lane_iota((1, LANES)) % HEAD_DIM) < HALF
    for lo, hi in _column_chunks(o_ref.shape[1]):
        acc = jnp.dot(a, w_ref[:, lo:hi], preferred_element_type=F32)
        for c in range((hi - lo) // LANES):
            x = acc[:, c * LANES:(c + 1) * LANES]
            partner = jnp.where(first, pltpu.roll(x, LANES - HALF, axis=1), pltpu.roll(x, HALF, axis=1))
            o_ref[:, lo + c * LANES:lo + (c + 1) * LANES] = x * cos + partner * sin


def _proj(a, w, tm, tn, seq=None, rope=None, time_major_batch=None):
    M, K = a.shape
    N = w.shape[1]
    nm, nn = M // tm, N // tn
    in_specs = [pl.BlockSpec((tm, K), lambda j, i: (i, 0)),
                pl.BlockSpec((K, tn), lambda j, i: (0, j))]
    args = [a, w]
    kern = _proj_kernel
    if rope is not None:
        ns = seq // tm
        in_specs += [pl.BlockSpec((tm, LANES), lambda j, i: (i % ns, 0))] * 2
        args += list(rope)
        kern = _proj_rope_kernel
    if time_major_batch is None:
        out_spec = pl.BlockSpec((tm, tn), lambda j, i: (i, j))
        out_shape = jax.ShapeDtypeStruct((M, N), F32)
    else:
        assert nn == 1
        ns = seq // tm
        out_spec = pl.BlockSpec((tm, N), lambda j, i: (i % ns, i // ns))
        out_shape = jax.ShapeDtypeStruct((seq, time_major_batch * N), F32)
    out = pl.pallas_call(
        kern, grid=(nn, nm), in_specs=in_specs, out_specs=out_spec, out_shape=out_shape,
        compiler_params=_params("parallel", "parallel"), name="proj",
    )(*args)
    if time_major_batch is not None:
        out = out.reshape(seq * time_major_batch, N)
    return out


def _s5_kernel(u_ref, bb_ref, cc_ref, are_ref, aim_ref, d_ref, wg_ref, bg_ref, y_ref,
               buf_ref, st_ref, *, nb, ts):
    n_chunks, cw, sw2 = bb_ref.shape
    sw = sw2 // 2

    @pl.when(pl.program_id(0) == 0)
    def _():
        st_ref[...] = jnp.zeros_like(st_ref)

    u = u_ref[...]
    ys = []
    for c in range(n_chunks):
        re0, im0 = c * sw2, c * sw2 + sw
        buf_ref[:, re0:re0 + sw2] = jnp.dot(u[:, c * cw:(c + 1) * cw].astype(BF16), bb_ref[c],
                                            preferred_element_type=F32)
        a_re = jnp.broadcast_to(are_ref[:, c * sw:(c + 1) * sw], (nb, sw))
        a_im = jnp.broadcast_to(aim_ref[:, c * sw:(c + 1) * sw], (nb, sw))

        def step(t, carry, re0=re0, im0=im0, a_re=a_re, a_im=a_im):
            x_re, x_im = carry
            r = pl.multiple_of(t * nb, nb)
            n_re = a_re * x_re - a_im * x_im + buf_ref[pl.ds(r, nb), re0:re0 + sw]
            n_im = a_re * x_im + a_im * x_re + buf_ref[pl.ds(r, nb), im0:im0 + sw]
            buf_ref[pl.ds(r, nb), re0:re0 + sw] = n_re
            buf_ref[pl.ds(r, nb), im0:im0 + sw] = n_im
            return n_re, n_im

        x_re, x_im = lax.fori_loop(0, ts, step, (st_ref[:, re0:re0 + sw], st_ref[:, im0:im0 + sw]),
                                   unroll=8)
        st_ref[:, re0:re0 + sw] = x_re
        st_ref[:, im0:im0 + sw] = x_im
        ys.append(jnp.dot(buf_ref[:, re0:re0 + sw2].astype(BF16), cc_ref[c], preferred_element_type=F32))

    y = jnp.concatenate(ys, 1) + d_ref[...] * u
    g = jax.nn.gelu(y)
    z = jnp.dot(g.astype(BF16), wg_ref[...], preferred_element_type=F32) + bg_ref[...]
    y_ref[...] = g * jax.nn.sigmoid(z)


def _s5(u_tm, nb, lam_re, lam_im, log_dt, b_re, b_im, c_re, c_im, d_skip, w_glu, b_glu, ts=64):
    R, W = u_tm.shape
    G, P = lam_re.shape
    C = SSM_CH
    lr = jnp.minimum(lam_re, -1e-4)
    li = lam_im
    dt = jnp.exp(log_dt)[:, None]
    mag = jnp.exp(lr * dt)
    a_re = mag * jnp.cos(li * dt)
    a_im = mag * jnp.sin(li * dt)
    den = lr * lr + li * li
    z_re = ((a_re - 1.0) * lr + a_im * li) / den
    z_im = (a_im * lr - (a_re - 1.0) * li) / den
    bb_re = z_re[..., None] * b_re - z_im[..., None] * b_im
    bb_im = z_re[..., None] * b_im + z_im[..., None] * b_re
    gc = LANES // C
    nc = G // gc
    eye = jnp.eye(gc, dtype=F32)

    def block_diag_in(m):
        return jnp.einsum('ngpc,gh->ngchp', m.reshape(nc, gc, P, C), eye).reshape(nc, gc * C, gc * P)

    def block_diag_out(m):
        return jnp.einsum('ngcp,gh->ngphc', m.reshape(nc, gc, C, P), eye).reshape(nc, gc * P, gc * C)

    bb = jnp.concatenate([block_diag_in(bb_re), block_diag_in(bb_im)], 2).astype(BF16)
    cc = jnp.concatenate([block_diag_out(c_re), -block_diag_out(c_im)], 1).astype(BF16)
    ns = G * P
    rows = ts * nb
    kern = functools.partial(_s5_kernel, nb=nb, ts=ts)
    const = lambda i: (0, 0)
    const3 = lambda i: (0, 0, 0)
    return pl.pallas_call(
        kern,
        grid=(R // rows,),
        in_specs=[pl.BlockSpec((rows, W), lambda i: (i, 0)),
                  pl.BlockSpec(bb.shape, const3),
                  pl.BlockSpec(cc.shape, const3),
                  pl.BlockSpec((1, ns), const),
                  pl.BlockSpec((1, ns), const),
                  pl.BlockSpec((1, W), const),
                  pl.BlockSpec((W, W), const),
                  pl.BlockSpec((1, W), const)],
        out_specs=pl.BlockSpec((rows, W), lambda i: (i, 0)),
        out_shape=jax.ShapeDtypeStruct((R, W), F32),
        scratch_shapes=[pltpu.VMEM((rows, 2 * ns), F32), pltpu.VMEM((nb, 2 * ns), F32)],
        compiler_params=_params("arbitrary"),
        name="s5",
    )(u_tm, bb, cc, a_re.reshape(1, ns), a_im.reshape(1, ns), d_skip.reshape(1, W),
      w_glu.astype(BF16), b_glu.reshape(1, W))


def _stack_pair(q):
    first = _lane_iota((1, LANES)) < HEAD_DIM
    return jnp.concatenate([jnp.where(first, q, 0.0), jnp.where(first, 0.0, q)], 0)


def _unstack_pair(x):
    t = x.shape[0] // 2
    first = _lane_iota((1, LANES)) < HEAD_DIM
    return jnp.where(first, x[:t], x[t:])


def _sb_kernel(q_ref, k_ref, v_ref, o_ref):
    tq = q_ref.shape[0]
    qi = pl.program_id(2)
    n_pairs = q_ref.shape[1] // LANES
    groups = [slice(p * LANES, (p + 1) * LANES) for p in range(n_pairs)]
    qs = [_stack_pair(q_ref[:, g]).astype(BF16) for g in groups]
    tri = (_row_iota((tq, tq)) > _lane_iota((tq, tq))).astype(BF16)
    diag = _lane_iota((2 * tq, tq)) < (_row_iota((2 * tq, tq)) % tq)

    def tile(kj, carry, masked):
        r = pl.multiple_of(kj * tq, tq)
        zs = [_dot_nt(q, k_ref[pl.ds(r, tq), g]) for q, g in zip(qs, groups)]
        stage = []
        for z in zs:
            pos = jnp.maximum(z, 0.0)
            neg = z - pos
            log_term = jnp.log2(1.0 + jnp.exp2(neg - pos))
            sp = pos + log_term
            if masked:
                sp = jnp.where(diag, sp, 0.0)
            stage.append((neg - log_term, jnp.sum(sp, -1, keepdims=True), _dot(sp, tri)))
        out = []
        for (log_beta, row_sum, after), g, (acc, tail) in zip(stage, groups, carry):
            w = jnp.exp2(log_beta - (after + tail))
            if masked:
                w = jnp.where(diag, w, 0.0)
            out.append((acc + _dot(w, v_ref[pl.ds(r, tq), g]), tail + row_sum))
        return tuple(out)

    init = tuple((jnp.zeros((2 * tq, LANES), F32), jnp.zeros((2 * tq, 1), F32)) for _ in groups)
    carry = tile(qi, init, True)
    carry = lax.fori_loop(0, qi, lambda i, c: tile(qi - 1 - i, c, False), carry)
    for g, (acc, _) in zip(groups, carry):
        o_ref[:, g] = _unstack_pair(acc)


def _stick_breaking(qkv, nb, seq, q_col, k_col, v_col, n_pairs):
    tq = min(SB_TILE, seq)
    nq = seq // tq
    per = SB_PAIRS
    w = per * LANES
    assert n_pairs % per == 0 and q_col % per == 0 and k_col % per == 0 and v_col % per == 0
    return pl.pallas_call(
        _sb_kernel,
        grid=(nb, n_pairs // per, nq),
        in_specs=[pl.BlockSpec((tq, w), lambda b, p, i: (b * nq + i, q_col // per + p)),
                  pl.BlockSpec((seq, w), lambda b, p, i: (b, k_col // per + p)),
                  pl.BlockSpec((seq, w), lambda b, p, i: (b, v_col // per + p))],
        out_specs=pl.BlockSpec((tq, w), lambda b, p, i: (b * nq + i, p)),
        out_shape=jax.ShapeDtypeStruct((nb * seq, n_pairs * LANES), F32),
        compiler_params=_params("parallel", "parallel", "arbitrary"),
        name="stick_breaking",
    )(qkv, qkv, qkv)


def _dil_kernel(q_ref, k_ref, v_ref, o_ref, m_ref, l_ref, a_ref):
    seq = q_ref.shape[0]
    tq = Q_TILE
    first = _lane_iota((1, LANES)) < HEAD_DIM

    q_in = _row_iota((2 * tq, 1)) % tq

    def band_bias(wd, n_tiles):
        if n_tiles > 1:
            lag = q_in - (_lane_iota((1, 2 * tq)) - tq)
            return jnp.where((lag >= 0) & (lag <= wd), 0.0, NEG_BIG)
        return jnp.where(_lane_iota((1, tq)) <= q_in, 0.0, NEG_BIG)

    def tile(c, dil, band, n_tiles, r, i):
        base = r + dil * tq * i

        def rows(ref, start):
            if dil == 1:
                return ref[pl.ds(pl.multiple_of(start, tq), tq), :]
            return ref[pl.ds(start, tq, stride=dil), :]

        qs = _stack_pair(rows(q_ref, base)).astype(BF16)
        if n_tiles > 1:
            prev = r + dil * tq * jnp.maximum(i - 1, 0)
            kk = jnp.concatenate([rows(k_ref, prev), rows(k_ref, base)], 0)
            vv = jnp.concatenate([rows(v_ref, prev), rows(v_ref, base)], 0)
            no_prev = jnp.where((i == 0) & (_lane_iota((1, 2 * tq)) < tq), NEG_BIG, 0.0)
            bias = band + no_prev
        else:
            kk = rows(k_ref, base)
            vv = rows(v_ref, base)
            bias = band
        s = _dot_nt(qs, kk) + bias
        m = jnp.max(s, -1, keepdims=True)
        p = jnp.exp2(s - m)
        l = jnp.sum(p, -1, keepdims=True)
        acc = _dot(p, vv)
        m2 = jnp.where(first, m[:tq], m[tq:])
        l2 = jnp.where(first, l[:tq], l[tq:])
        a2 = jnp.where(first, acc[:tq], acc[tq:])
        if dil == 1:
            sl = pl.ds(pl.multiple_of(base, tq), tq)
        else:
            sl = pl.ds(base, tq, stride=dil)
        m_ref[c, sl, :] = m2
        l_ref[c, sl, :] = l2
        a_ref[c, sl, :] = a2

    for c, (window, dil) in enumerate(DIL_CONFIGS):
        wd = window // dil
        n_tiles = seq // dil // tq

        band = band_bias(wd, n_tiles)

        def per_tile(n, _, c=c, dil=dil, band=band, n_tiles=n_tiles):
            tile(c, dil, band, n_tiles, n // n_tiles, n % n_tiles)
            return 0

        lax.fori_loop(0, dil * n_tiles, per_tile, 0, unroll=16)

    def combine(i, _):
        sl = pl.ds(pl.multiple_of(i * tq, tq), tq)
        m0, m1, m2 = m_ref[0, sl, :], m_ref[1, sl, :], m_ref[2, sl, :]
        mx = jnp.maximum(jnp.maximum(m0, m1), m2)
        e0, e1, e2 = jnp.exp2(m0 - mx), jnp.exp2(m1 - mx), jnp.exp2(m2 - mx)
        num = e0 * a_ref[0, sl, :] + e1 * a_ref[1, sl, :] + e2 * a_ref[2, sl, :]
        den = e0 * l_ref[0, sl, :] + e1 * l_ref[1, sl, :] + e2 * l_ref[2, sl, :]
        o_ref[sl, :] = num / den
        return 0

    lax.fori_loop(0, seq // tq, combine, 0)


def _dilated(q_arr, q_col, k_arr, k_col, v_arr, v_col, nb, seq, n_pairs):
    return pl.pallas_call(
        _dil_kernel,
        grid=(nb, n_pairs),
        in_specs=[pl.BlockSpec((seq, LANES), lambda b, p: (b, q_col + p)),
                  pl.BlockSpec((seq, LANES), lambda b, p: (b, k_col + p)),
                  pl.BlockSpec((seq, LANES), lambda b, p: (b, v_col + p))],
        out_specs=pl.BlockSpec((seq, LANES), lambda b, p: (b, p)),
        out_shape=jax.ShapeDtypeStruct((nb * seq, n_pairs * LANES), F32),
        scratch_shapes=[pltpu.VMEM((3, seq, LANES), F32)] * 3,
        compiler_params=_params("parallel", "parallel"),
        name="dilated",
    )(q_arr, k_arr, v_arr)


def _cmp_kernel(t_ref, pe_ref, w1a_ref, w1b_ref, w2_ref, cos_ref, sin_ref, o_ref):
    nblk = t_ref.shape[0] // CMP_STRIDE
    j = pl.program_id(1)
    out = jnp.zeros((nblk, LANES), F32)
    for g in range(NSA_KV_HEADS):
        p1 = jnp.zeros((nblk, CMP_HIDDEN), F32)
        p2 = jnp.zeros((nblk, CMP_HIDDEN), F32)
        for l in range(CMP_STRIDE):
            x = t_ref[pl.ds(l, nblk, stride=CMP_STRIDE), :]
            p1 = p1 + _dot(x + pe_ref[0, l:l + 1, :], w1a_ref[0, g, l])
            p2 = p2 + _dot(x + pe_ref[0, CMP_STRIDE + l:CMP_STRIDE + l + 1, :], w1b_ref[0, g, l])
        hidden = p1 + pltpu.roll(p2, nblk - 1, axis=0)
        out = out + _dot(jax.nn.gelu(hidden), w2_ref[0, g])
    first = (_lane_iota((1, LANES)) % HEAD_DIM) < HALF
    partner = jnp.where(first, pltpu.roll(out, LANES - HALF, axis=1), pltpu.roll(out, HALF, axis=1))
    roped = out * cos_ref[...] + partner * sin_ref[...]
    o_ref[0] = jnp.where(j == 0, roped, out)


def _compress(p_arr, col0, nb, seq, pe, w1, w2, cos_c, sin_c):
    nblk = seq // CMP_STRIDE
    G = NSA_KV_HEADS
    pe2 = jnp.tile(pe, (1, 1, G))
    w1r = w1.reshape(2, CMP_LEN, HEAD_DIM, CMP_HIDDEN)
    w1e = jnp.zeros((2, G, CMP_LEN, LANES, CMP_HIDDEN), F32)
    w2e = jnp.zeros((2, G, CMP_HIDDEN, LANES), F32)
    for g in range(G):
        w1e = w1e.at[:, g, :, g * HEAD_DIM:(g + 1) * HEAD_DIM, :].set(w1r)
        w2e = w2e.at[:, g, :, g * HEAD_DIM:(g + 1) * HEAD_DIM].set(w2)
    w1e = w1e.astype(BF16)
    w2e = w2e.astype(BF16)
    return pl.pallas_call(
        _cmp_kernel,
        grid=(nb, 2),
        in_specs=[pl.BlockSpec((seq, LANES), lambda b, j: (b, col0 + j)),
                  pl.BlockSpec((1, CMP_LEN, LANES), lambda b, j: (j, 0, 0)),
                  pl.BlockSpec((1, G, CMP_STRIDE, LANES, CMP_HIDDEN), lambda b, j: (j, 0, 0, 0, 0)),
                  pl.BlockSpec((1, G, CMP_STRIDE, LANES, CMP_HIDDEN), lambda b, j: (j, 0, 1, 0, 0)),
                  pl.BlockSpec((1, G, CMP_HIDDEN, LANES), lambda b, j: (j, 0, 0, 0)),
                  pl.BlockSpec((nblk, LANES), lambda b, j: (0, 0)),
                  pl.BlockSpec((nblk, LANES), lambda b, j: (0, 0))],
        out_specs=pl.BlockSpec((1, nblk, LANES), lambda b, j: (j, b, 0)),
        out_shape=jax.ShapeDtypeStruct((2, nb * nblk, LANES), F32),
        compiler_params=_params("parallel", "parallel"),
        name="nsa_compress",
    )(p_arr, pe2, w1e, w1e, w2e, cos_c, sin_c)


def _nsa_kernel(q_ref, gate_ref, kc_ref, vc_ref, ks_ref, vs_ref, kw_ref, vw_ref, ovt_ref, o_ref):
    tq = q_ref.shape[0]
    R = NSA_REP
    seq = ks_ref.shape[0]
    n_sel = seq // SEL_BLOCK
    n_cmp = (seq - CMP_LEN) // CMP_STRIDE + 1
    g = pl.program_id(1)
    qi = pl.program_id(2)
    lane = _lane_iota((1, LANES))
    mine = (lane // HEAD_DIM) == g

    parts = []
    for h in range(R):
        x = q_ref[:, (h // 2) * LANES:(h // 2 + 1) * LANES]
        x = jnp.where(g == (h % 2), x, _swap_halves(x))
        parts.append(jnp.where(mine, x, 0.0))
    qs = jnp.concatenate(parts, 0).astype(BF16)

    t_row = qi * tq + _row_iota((tq, 1))

    nblk = kc_ref.shape[1]
    c_idx = _lane_iota((tq, nblk))
    cmask = (c_idx < n_cmp) & (c_idx * CMP_STRIDE + (CMP_LEN - 1) <= qi * tq + _row_iota((tq, nblk)))
    s_c = _dot_nt(qs, kc_ref[0]).reshape(R, tq, nblk) + jnp.where(cmask, 0.0, NEG_BIG)
    m_c = jnp.max(s_c, -1, keepdims=True)
    m_c = jnp.where(m_c > 0.5 * NEG_BIG, m_c, 0.0)
    p_c = jnp.exp2(s_c - m_c)
    p_c = p_c * (1.0 / jnp.maximum(jnp.sum(p_c, -1, keepdims=True), 1e-30))
    vc = vc_ref[0]
    o_c = _dot(p_c.reshape(R * tq, nblk), jnp.where(mine, vc, _swap_halves(vc)))

    psum = jnp.sum(p_c, 0)
    p_hi = psum.astype(BF16)
    p_lo = (psum - p_hi.astype(F32)).astype(BF16)
    imp = (_dot_nt(ovt_ref[...], p_hi) + _dot_nt(ovt_ref[...], p_lo))[:n_sel]
    n_idx = _row_iota((n_sel, tq))
    t_q = qi * tq + _lane_iota((n_sel, tq))
    cur = t_q // SEL_BLOCK
    imp = jnp.where(n_idx * SEL_BLOCK > t_q, -1.0, imp)
    forced = (n_idx == 0) | (n_idx == cur) | (n_idx == cur - 1)
    imp = jnp.where(forced, 1e9, imp)
    rank = jnp.zeros((n_sel, tq), F32)
    for m in range(n_sel):
        row = imp[m:m + 1, :]
        earlier = (n_idx > m).astype(F32)
        rank = rank + jnp.where(row > imp, 1.0, jnp.where(row == imp, earlier, 0.0))
    top_n = min(SEL_TOPN, n_sel)
    sel_t = jnp.where((rank < top_n) & (imp > -0.5), 1.0, 0.0)
    sel_t = jnp.concatenate([sel_t, jnp.zeros((LANES - n_sel, tq), F32)], 0)
    sel = sel_t.T.astype(BF16)

    k_col = _lane_iota((tq, tq))
    blk_row = _row_iota((LANES, tq))
    blk_of_key = _lane_iota((LANES, tq)) // SEL_BLOCK

    def sel_bias(j, causal):
        expand = (blk_row == (j * (tq // SEL_BLOCK) + blk_of_key)).astype(BF16)
        chosen = jnp.dot(sel, expand, preferred_element_type=F32)
        bias = (1.0 - chosen) * NEG_BIG
        if causal:
            bias = jnp.where((j * tq + k_col) <= t_row, bias, NEG_BIG)
        return bias

    def win_bias(j):
        diff = t_row - (j * tq + k_col)
        return jnp.where((diff >= 0) & (diff < NSA_WINDOW), 0.0, NEG_BIG)

    hp = R // NSA_CHAINS
    q_parts = [qs[c * hp * tq:(c + 1) * hp * tq] for c in range(NSA_CHAINS)]

    def scores(k_ref, j):
        kt = k_ref[pl.ds(pl.multiple_of(j * tq, tq), tq), :].astype(BF16)
        return [_dot_nt(q, kt).reshape(hp, tq, tq) for q in q_parts]

    def update(carry, s_parts, v_ref, j):
        vt = jnp.where(mine, v_ref[pl.ds(pl.multiple_of(j * tq, tq), tq), :], 1.0).astype(BF16)
        out = []
        for (m_run, acc), s in zip(carry, s_parts):
            m_new = jnp.maximum(m_run, jnp.max(s, -1, keepdims=True))
            alpha = jnp.exp2(m_run - m_new)
            p = jnp.exp2(s - m_new)
            out.append((m_new, alpha * acc + _dot(p.reshape(hp * tq, tq), vt).reshape(hp, tq, LANES)))
        return tuple(out)

    def normalise(carry):
        acc = jnp.concatenate([a for _, a in carry], 0)
        swapped = _swap_halves(acc)
        return jnp.where(mine, acc, swapped) / jnp.where(mine, swapped, acc)

    init = tuple((jnp.full((hp, tq, 1), NEG_BIG, F32), jnp.zeros((hp, tq, LANES), F32))
                 for _ in range(NSA_CHAINS))
    j_win = jnp.maximum(qi - NSA_WINDOW // tq, 0)

    def selected_only(j, carry):
        s_parts = scores(ks_ref, j)
        bias = sel_bias(j, False)
        return update(carry, [s + bias for s in s_parts], vs_ref, j)

    def selected_and_window(j, carry):
        s_s, s_w = scores(ks_ref, j), scores(kw_ref, j)
        b_s, b_w = sel_bias(j, True), win_bias(j)
        return (update(carry[0], [s + b_s for s in s_s], vs_ref, j),
                update(carry[1], [s + b_w for s in s_w], vw_ref, j))

    c_sel = lax.fori_loop(0, j_win, selected_only, init)
    c_sel, c_win = lax.fori_loop(j_win, qi + 1, selected_and_window, (c_sel, init))
    o_s, o_w = normalise(c_sel), normalise(c_win)
    o_c = o_c.reshape(R, tq, LANES)

    gate = jax.nn.sigmoid(gate_ref[...])
    outs = []
    for h in range(R):
        g_c = gate[:, 3 * h:3 * h + 1]
        g_s = gate[:, 3 * h + 1:3 * h + 2]
        g_w = gate[:, 3 * h + 2:3 * h + 3]
        outs.append(g_c * o_c[h] + g_s * o_s[h] + g_w * o_w[h])
    first = lane < HEAD_DIM
    for pair in range(R // 2):
        o_ref[:, pair * LANES:(pair + 1) * LANES] = jnp.where(first, outs[2 * pair], outs[2 * pair + 1])


def _nsa(r_arr, p_arr, cmp_kv, overlap, nb, seq, cols):
    tq = min(NSA_TILE, seq)
    nq = seq // tq
    nblk = seq // CMP_STRIDE
    G = NSA_KV_HEADS
    qw = NSA_REP * HEAD_DIM
    return pl.pallas_call(
        _nsa_kernel,
        grid=(nb, G, nq),
        in_specs=[pl.BlockSpec((tq, qw), lambda b, g, i: (b * nq + i, cols['q'] * LANES // qw + g)),
                  pl.BlockSpec((tq, LANES), lambda b, g, i: (b * nq + i, cols['gate'] + g)),
                  pl.BlockSpec((1, nblk, LANES), lambda b, g, i: (0, b, 0)),
                  pl.BlockSpec((1, nblk, LANES), lambda b, g, i: (1, b, 0)),
                  pl.BlockSpec((seq, LANES), lambda b, g, i: (b, cols['k_slc'])),
                  pl.BlockSpec((seq, LANES), lambda b, g, i: (b, cols['v_slc'])),
                  pl.BlockSpec((seq, LANES), lambda b, g, i: (b, cols['k_win'])),
                  pl.BlockSpec((seq, LANES), lambda b, g, i: (b, cols['v_win'])),
                  pl.BlockSpec((LANES, nblk), lambda b, g, i: (0, 0))],
        out_specs=pl.BlockSpec((tq, qw), lambda b, g, i: (b * nq + i, g)),
        out_shape=jax.ShapeDtypeStruct((nb * seq, G * qw), F32),
        compiler_params=_params("parallel", "parallel", "arbitrary"),
        name="nsa",
    )(r_arr, p_arr, cmp_kv, cmp_kv, r_arr, p_arr, r_arr, p_arr, overlap)


def _mix_out_kernel(ya_ref, yb_ref, yc_ref, yd_ref, ng_ref, w_ref, h_ref, g_ref, b_ref,
                    o_ref, ob_ref, *, alpha):
    fs = []
    for rows in _row_chunks(o_ref.shape[0]):
        ys = []
        for n, y_ref in enumerate((ya_ref, yb_ref, yc_ref, yd_ref)):
            y = y_ref[rows, :]
            y = y * lax.rsqrt(jnp.mean(y * y, -1, keepdims=True) + RMS_EPS) * ng_ref[n:n + 1, :]
            ys.append(y.astype(BF16))
        fs.append(jnp.dot(jnp.concatenate(ys, 1), w_ref[...], preferred_element_type=F32))
    for rows, f in zip(_row_chunks(o_ref.shape[0]), fs):
        out = _layer_norm(alpha * h_ref[rows, :] + f, g_ref[...], b_ref[...])
        o_ref[rows, :] = out
        ob_ref[rows, :] = out.astype(BF16)


def _mix_out(ya_tm, yb, yc, yd, norm_g, w_out, h, ln_g, ln_b, nb, seq, alpha, tm):
    T, D = h.shape
    gw = yb.shape[1]
    ns = seq // tm
    ya_view = ya_tm.reshape(seq, nb * gw)
    tok = lambda i: (i, 0)
    const = lambda i: (0, 0)
    return pl.pallas_call(
        functools.partial(_mix_out_kernel, alpha=alpha),
        grid=(T // tm,),
        in_specs=[pl.BlockSpec((tm, gw), lambda i: (i % ns, i // ns)),
                  pl.BlockSpec((tm, gw), tok), pl.BlockSpec((tm, gw), tok), pl.BlockSpec((tm, gw), tok),
                  pl.BlockSpec((4, gw), const),
                  pl.BlockSpec((4 * gw, D), const),
                  pl.BlockSpec((tm, D), tok),
                  pl.BlockSpec((1, D), const), pl.BlockSpec((1, D), const)],
        out_specs=[pl.BlockSpec((tm, D), tok), pl.BlockSpec((tm, D), tok)],
        out_shape=[jax.ShapeDtypeStruct((T, D), F32), jax.ShapeDtypeStruct((T, D), BF16)],
        compiler_params=_params("parallel"),
        name="mix_out",
    )(ya_view, yb, yc, yd, norm_g, w_out.astype(BF16), h, ln_g.reshape(1, D), ln_b.reshape(1, D))


def _xattn_kernel(hb_ref, h_ref, wq_ref, kv_ref, wo_ref, g_ref, b_ref, o_ref, ob_ref, *, alpha):
    width = wq_ref.shape[1]
    chunks = _row_chunks(o_ref.shape[0])
    qs = [jnp.dot(hb_ref[rows, :], wq_ref[...], preferred_element_type=F32) for rows in chunks]
    fs = []
    for q in qs:
        heads = []
        for hd in range(XA_HEADS):
            lo = hd * XA_HEAD_DIM
            k = kv_ref[:, lo:lo + XA_HEAD_DIM]
            v = kv_ref[:, width + lo:width + lo + XA_HEAD_DIM]
            s = _dot_nt(q[:, lo:lo + XA_HEAD_DIM], k) * (XA_HEAD_DIM ** -0.5)
            p = jnp.exp(s - jnp.max(s, -1, keepdims=True))
            p = p * (1.0 / jnp.sum(p, -1, keepdims=True))
            heads.append(_dot(p, v).astype(BF16))
        fs.append(jnp.dot(jnp.concatenate(heads, 1), wo_ref[...], preferred_element_type=F32))
    for rows, f in zip(chunks, fs):
        out = _layer_norm(alpha * h_ref[rows, :] + f, g_ref[...], b_ref[...])
        o_ref[rows, :] = out
        ob_ref[rows, :] = out.astype(BF16)


def _xattn(hb, h, wq, kv, wo, ln_g, ln_b, seq, mem_len, alpha, tm):
    T, D = h.shape
    width = wq.shape[1]
    ns = seq // tm
    tok = lambda i: (i, 0)
    const = lambda i: (0, 0)
    return pl.pallas_call(
        functools.partial(_xattn_kernel, alpha=alpha),
        grid=(T // tm,),
        in_specs=[pl.BlockSpec((tm, D), tok), pl.BlockSpec((tm, D), tok),
                  pl.BlockSpec((D, width), const),
                  pl.BlockSpec((mem_len, 2 * width), lambda i: (i // ns, 0)),
                  pl.BlockSpec((width, D), const),
                  pl.BlockSpec((1, D), const), pl.BlockSpec((1, D), const)],
        out_specs=[pl.BlockSpec((tm, D), tok), pl.BlockSpec((tm, D), tok)],
        out_shape=[jax.ShapeDtypeStruct((T, D), F32), jax.ShapeDtypeStruct((T, D), BF16)],
        compiler_params=_params("parallel"),
        name="cross_attn",
    )(hb, h, wq.astype(BF16), kv, wo.astype(BF16), ln_g.reshape(1, D), ln_b.reshape(1, D))


def _ffn_kernel(hb_ref, h_ref, wg_ref, wu_ref, wd_ref, g_ref, b_ref, o_ref, ob_ref, acc_ref, *, alpha):
    j = pl.program_id(1)

    @pl.when(j == 0)
    def _():
        acc_ref[...] = alpha * h_ref[...]

    x = hb_ref[...]
    gate = jnp.dot(x, wg_ref[...], preferred_element_type=F32)
    up = jnp.dot(x, wu_ref[...], preferred_element_type=F32)
    act = (jax.nn.silu(gate) * up).astype(BF16)
    acc_ref[...] += jnp.dot(act, wd_ref[...], preferred_element_type=F32)

    @pl.when(j == pl.num_programs(1) - 1)
    def _():
        out = _layer_norm(acc_ref[...], g_ref[...], b_ref[...])
        o_ref[...] = out
        ob_ref[...] = out.astype(BF16)


def _ffn(hb, h, wg, wu, wd, ln_g, ln_b, alpha, tm, th):
    T, D = h.shape
    H = wg.shape[1]
    tok = lambda i, j: (i, 0)
    const = lambda i, j: (0, 0)
    return pl.pallas_call(
        functools.partial(_ffn_kernel, alpha=alpha),
        grid=(T // tm, H // th),
        in_specs=[pl.BlockSpec((tm, D), tok), pl.BlockSpec((tm, D), tok),
                  pl.BlockSpec((D, th), lambda i, j: (0, j)),
                  pl.BlockSpec((D, th), lambda i, j: (0, j)),
                  pl.BlockSpec((th, D), lambda i, j: (j, 0)),
                  pl.BlockSpec((1, D), const), pl.BlockSpec((1, D), const)],
        out_specs=[pl.BlockSpec((tm, D), tok), pl.BlockSpec((tm, D), tok)],
        out_shape=[jax.ShapeDtypeStruct((T, D), F32), jax.ShapeDtypeStruct((T, D), BF16)],
        scratch_shapes=[pltpu.VMEM((tm, D), F32)],
        compiler_params=_params("parallel", "arbitrary"),
        name="ffn",
    )(hb, h, wg.astype(BF16), wu.astype(BF16), wd.astype(BF16), ln_g.reshape(1, D), ln_b.reshape(1, D))


def _rope_tables(pos):
    inv_freq = ROPE_THETA ** (-jnp.arange(HALF, dtype=F32) / HALF)
    ang = pos.astype(F32)[:, None] * inv_freq[None, :]
    cos = jnp.tile(jnp.cos(ang), (1, LANES // HALF))
    sin = jnp.sin(ang)
    sin = jnp.tile(jnp.concatenate([-sin, sin], 1), (1, LANES // HEAD_DIM))
    return cos, sin


def _overlap_table(seq):
    n_cmp = (seq - CMP_LEN) // CMP_STRIDE + 1
    n_sel = seq // SEL_BLOCK
    ci = np.arange(n_cmp)[:, None] * CMP_STRIDE
    sj = np.arange(n_sel)[None, :] * SEL_BLOCK
    ov = np.clip(np.minimum(ci + CMP_LEN, sj + SEL_BLOCK) - np.maximum(ci, sj), 0, None) / CMP_LEN
    full = np.zeros((LANES, seq // CMP_STRIDE), np.float32)
    full[:n_sel, :n_cmp] = ov.T
    return jnp.asarray(full, dtype=BF16)


def _split_w_in(w_in, width):
    hw = width // 4
    kvw = NSA_KV_HEADS * HEAD_DIM
    n_gate = 3 * (hw // HEAD_DIM)
    o = np.cumsum([0, hw, hw, 6 * kvw, n_gate, 3 * hw, 3 * hw])
    u = w_in[:, o[0]:o[1]]
    nq = w_in[:, o[1]:o[2]]
    kv = [w_in[:, o[2] + j * kvw:o[2] + (j + 1) * kvw] for j in range(6)]
    gate = w_in[:, o[3]:o[4]]
    sb = w_in[:, o[4]:o[5]]
    dil = w_in[:, o[5]:o[6]]
    scale = HEAD_DIM ** -0.5 * math.log2(math.e)
    per_group = 3 * NSA_REP
    gates = [jnp.pad(gate[:, g * per_group:(g + 1) * per_group], ((0, 0), (0, LANES - per_group)))
             for g in range(NSA_KV_HEADS)]
    w_rope = jnp.concatenate([nq * scale, kv[2], kv[4], dil[:, :hw] * scale, dil[:, hw:2 * hw]], 1)
    w_plain = jnp.concatenate([sb[:, :hw] * scale, sb[:, hw:], dil[:, 2 * hw:],
                               kv[0], kv[1], kv[3], kv[5]] + gates, 1)
    return u.astype(BF16), w_rope.astype(BF16), w_plain.astype(BF16)


R_NQ, R_KSLC, R_KWIN, R_DILQ, R_DILK = 0, 4, 5, 6, 10
P_SBQ, P_SBK, P_SBV, P_DILV, P_KCMP, P_VSLC, P_VWIN, P_GATE = 0, 4, 8, 12, 16, 18, 19, 20


def _hybrid_mixer(hb, h, nb, seq, w_in, s5_params, cmp_pe, cmp_w1, cmp_w2, norm_g, w_out,
                  ln_g, ln_b, tables, alpha, tm):
    cos, sin, cos_c, sin_c, overlap = tables
    width = w_out.shape[0]
    n_pairs = width // 4 // LANES
    w_u, w_rope, w_plain = _split_w_in(w_in, width)
    u_tm = _proj(hb, w_u, tm, w_u.shape[1], seq=seq, time_major_batch=nb)
    r_arr = _proj(hb, w_rope, min(2 * tm, seq), w_rope.shape[1], seq=seq, rope=(cos, sin))
    p_arr = _proj(hb, w_plain, tm, w_plain.shape[1])
    y_a = _s5(u_tm, nb, *s5_params)
    cmp_kv = _compress(p_arr, P_KCMP, nb, seq, cmp_pe, cmp_w1, cmp_w2, cos_c, sin_c)
    y_b = _nsa(r_arr, p_arr, cmp_kv, overlap, nb, seq,
               dict(q=R_NQ, gate=P_GATE, k_slc=R_KSLC, v_slc=P_VSLC, k_win=R_KWIN, v_win=P_VWIN))
    y_c = _stick_breaking(p_arr, nb, seq, P_SBQ, P_SBK, P_SBV, n_pairs)
    y_d = _dilated(r_arr, R_DILQ, r_arr, R_DILK, p_arr, P_DILV, nb, seq, n_pairs)
    return _mix_out(y_a, y_b, y_c, y_d, norm_g, w_out, h, ln_g, ln_b, nb, seq, alpha, tm)


def kernel(x, mem, ln_in_g, ln_in_b, w_in, s5_lambda_re, s5_lambda_im, s5_log_dt, s5_b_re, s5_b_im, s5_c_re, s5_c_im, s5_d, s5_w_glu, s5_b_glu, nsa_cmp_pe, nsa_cmp_w1, nsa_cmp_w2, mix_norm_g, w_out, ln1_g, ln1_b, xa_wq, xa_wkv, xa_wo, ln2_g, ln2_b, ffn_w_gate, ffn_w_up, ffn_w_down, ln3_g, ln3_b):
    nb, seq, d_model = x.shape
    mem_len = mem.shape[1]
    depth = w_in.shape[0]
    alpha = (2 * depth) ** 0.25
    tm = min(512, seq)
    pos = jnp.arange(seq)
    cos, sin = _rope_tables(pos)
    nblk = seq // CMP_STRIDE
    cos_c, sin_c = _rope_tables(jnp.arange(nblk) * CMP_STRIDE + CMP_LEN - 1)
    tables = (cos, sin, cos_c, sin_c, _overlap_table(seq))
    mem_b = mem.reshape(nb * mem_len, d_model).astype(BF16)

    h, hb = _ln_in(x.reshape(nb * seq, d_model), ln_in_g, ln_in_b, tm)
    for l in range(depth):
        s5_params = (s5_lambda_re[l], s5_lambda_im[l], s5_log_dt[l], s5_b_re[l], s5_b_im[l],
                     s5_c_re[l], s5_c_im[l], s5_d[l], s5_w_glu[l], s5_b_glu[l])
        h, hb = _hybrid_mixer(hb, h, nb, seq, w_in[l], s5_params, nsa_cmp_pe[l], nsa_cmp_w1[l],
                              nsa_cmp_w2[l], mix_norm_g[l], w_out[l], ln1_g[l], ln1_b[l], tables, alpha, tm)
        kv = _proj(mem_b, xa_wkv[l].astype(BF16), min(512, nb * mem_len), xa_wkv.shape[2] // 2)
        h, hb = _xattn(hb, h, xa_wq[l], kv, xa_wo[l], ln2_g[l], ln2_b[l], seq, mem_len, alpha, tm)
        h, hb = _ffn(hb, h, ffn_w_gate[l], ffn_w_up[l], ffn_w_down[l], ln3_g[l], ln3_b[l], alpha,
                     tm, 512)
    return h.reshape(nb, seq, d_model)
```

```python
import functools
import math

import numpy as np
import jax
import jax.numpy as jnp
from jax import lax
from jax.experimental import pallas as pl
from jax.experimental.pallas import tpu as pltpu

F32 = jnp.float32
BF16 = jnp.bfloat16

LANES = 128
MXU_WIDTH = 256
CAST_ROWS = 512
VMEM_LIMIT = 56 * 1024 * 1024

HEAD_DIM = 64
HALF = HEAD_DIM // 2
ROPE_THETA = 10000.0
LN_EPS = 1e-5
RMS_EPS = 1e-6
SSM_CH = 16
SSM_STATE = 64
NSA_KV_HEADS = 2
NSA_REP = 4
CMP_LEN = 32
CMP_STRIDE = 16
CMP_HIDDEN = 128
SEL_BLOCK = 64
SEL_TOPN = 8
NSA_WINDOW = 512
DIL_CONFIGS = ((128, 1), (512, 4), (2048, 16))
XA_HEADS = 4
XA_HEAD_DIM = 128
Q_TILE = 128
SB_TILE = 256
SB_PAIRS = 4
NSA_TILE = 256
NEG_BIG = -1e30


def _params(*sem):
    return pltpu.CompilerParams(dimension_semantics=sem, vmem_limit_bytes=VMEM_LIMIT)


def _dot(a, b):
    return jnp.dot(a.astype(BF16), b.astype(BF16), preferred_element_type=F32)


def _dot_nt(a, b):
    return lax.dot_general(a.astype(BF16), b.astype(BF16), (((1,), (1,)), ((), ())),
                           preferred_element_type=F32)


def _layer_norm(x, g, b):
    mu = jnp.mean(x, -1, keepdims=True)
    xc = x - mu
    var = jnp.mean(xc * xc, -1, keepdims=True)
    return xc * lax.rsqrt(var + LN_EPS) * g + b


def _lane_iota(shape):
    return lax.broadcasted_iota(jnp.int32, shape, len(shape) - 1)


def _row_iota(shape):
    return lax.broadcasted_iota(jnp.int32, shape, len(shape) - 2)


def _swap_halves(x):
    return pltpu.roll(x, HEAD_DIM, axis=x.ndim - 1)


def _ln_in_kernel(x_ref, g_ref, b_ref, h_ref, hb_ref):
    y = _layer_norm(x_ref[...], g_ref[...], b_ref[...])
    h_ref[...] = y
    hb_ref[...] = y.astype(BF16)


def _ln_in(x2, g, b, tm):
    T, D = x2.shape
    return pl.pallas_call(
        _ln_in_kernel,
        grid=(T // tm,),
        in_specs=[pl.BlockSpec((tm, D), lambda i: (i, 0)),
                  pl.BlockSpec((1, D), lambda i: (0, 0)),
                  pl.BlockSpec((1, D), lambda i: (0, 0))],
        out_specs=[pl.BlockSpec((tm, D), lambda i: (i, 0)),
                   pl.BlockSpec((tm, D), lambda i: (i, 0))],
        out_shape=[jax.ShapeDtypeStruct((T, D), F32), jax.ShapeDtypeStruct((T, D), BF16)],
        compiler_params=_params("parallel"),
        name="ln_in",
    )(x2, g.reshape(1, D), b.reshape(1, D))


def _cast_kernel(w_ref, o_ref):
    o_ref[...] = w_ref[...].astype(BF16)


def _cast_scaled_kernel(w_ref, s_ref, o_ref):
    o_ref[...] = (w_ref[...] * s_ref[...]).astype(BF16)


def _layer_bf16(w_stack, layer, col_scale=None):
    _, K, N = w_stack.shape
    tk = min(CAST_ROWS, K)
    in_specs = [pl.BlockSpec((None, tk, N), lambda i: (layer, i, 0))]
    args = [w_stack]
    kern = _cast_kernel
    if col_scale is not None:
        in_specs.append(pl.BlockSpec((1, N), lambda i: (0, 0)))
        args.append(col_scale.reshape(1, N))
        kern = _cast_scaled_kernel
    return pl.pallas_call(
        kern, grid=(K // tk,), in_specs=in_specs,
        out_specs=pl.BlockSpec((tk, N), lambda i: (i, 0)),
        out_shape=jax.ShapeDtypeStruct((K, N), BF16),
        compiler_params=_params("parallel"), name="cast_bf16",
    )(*args)


def _row_chunks(n, parts=2):
    step = n // parts
    return [slice(i * step, (i + 1) * step) for i in range(parts)]


def _column_chunks(n):
    return [(lo, min(lo + MXU_WIDTH, n)) for lo in range(0, n, MXU_WIDTH)]


def _proj_kernel(a_ref, w_ref, o_ref):
    a = a_ref[...]
    for lo, hi in _column_chunks(o_ref.shape[1]):
        o_ref[:, lo:hi] = jnp.dot(a, w_ref[:, lo:hi], preferred_element_type=F32)


def _proj_rope_kernel(a_ref, w_ref, cos_ref, sin_ref, o_ref):
    a = a_ref[...]
    cos = cos_ref[...]
    sin = sin_ref[...]
    first = (_lane_iota((1, LANES)) % HEAD_DIM) < HALF
    for lo, hi in _column_chunks(o_ref.shape[1]):
        acc = jnp.dot(a, w_ref[:, lo:hi], preferred_element_type=F32)
        for c in range((hi - lo) // LANES):
            x = acc[:, c * LANES:(c + 1) * LANES]
            partner = jnp.where(first, pltpu.roll(x, LANES - HALF, axis=1), pltpu.roll(x, HALF, axis=1))
            o_ref[:, lo + c * LANES:lo + (c + 1) * LANES] = x * cos + partner * sin


def _proj(a, w, tm, tn, seq=None, rope=None, time_major_batch=None):
    M, K = a.shape
    N = w.shape[1]
    nm, nn = M // tm, N // tn
    in_specs = [pl.BlockSpec((tm, K), lambda j, i: (i, 0)),
                pl.BlockSpec((K, tn), lambda j, i: (0, j))]
    args = [a, w]
    kern = _proj_kernel
    if rope is not None:
        ns = seq // tm
        in_specs += [pl.BlockSpec((tm, LANES), lambda j, i: (i % ns, 0))] * 2
        args += list(rope)
        kern = _proj_rope_kernel
    if time_major_batch is None:
        out_spec = pl.BlockSpec((tm, tn), lambda j, i: (i, j))
        out_shape = jax.ShapeDtypeStruct((M, N), F32)
    else:
        assert nn == 1
        ns = seq // tm
        out_spec = pl.BlockSpec((tm, N), lambda j, i: (i % ns, i // ns))
        out_shape = jax.ShapeDtypeStruct((seq, time_major_batch * N), F32)
    out = pl.pallas_call(
        kern, grid=(nn, nm), in_specs=in_specs, out_specs=out_spec, out_shape=out_shape,
        compiler_params=_params("parallel", "parallel"), name="proj",
    )(*args)
    if time_major_batch is not None:
        out = out.reshape(seq * time_major_batch, N)
    return out


def _s5_kernel(u_ref, bb_ref, cc_ref, are_ref, aim_ref, d_ref, wg_ref, bg_ref, y_ref,
               buf_ref, st_ref, *, nb, ts):
    n_chunks, cw, sw2 = bb_ref.shape
    sw = sw2 // 2

    @pl.when(pl.program_id(0) == 0)
    def _():
        st_ref[...] = jnp.zeros_like(st_ref)

    u = u_ref[...]
    ys = []
    for c in range(n_chunks):
        re0, im0 = c * sw2, c * sw2 + sw
        buf_ref[:, re0:re0 + sw2] = jnp.dot(u[:, c * cw:(c + 1) * cw].astype(BF16), bb_ref[c],
                                            preferred_element_type=F32)
        a_re = jnp.broadcast_to(are_ref[:, c * sw:(c + 1) * sw], (nb, sw))
        a_im = jnp.broadcast_to(aim_ref[:, c * sw:(c + 1) * sw], (nb, sw))

        def step(t, carry, re0=re0, im0=im0, a_re=a_re, a_im=a_im):
            x_re, x_im = carry
            r = pl.multiple_of(t * nb, nb)
            n_re = a_re * x_re - a_im * x_im + buf_ref[pl.ds(r, nb), re0:re0 + sw]
            n_im = a_re * x_im + a_im * x_re + buf_ref[pl.ds(r, nb), im0:im0 + sw]
            buf_ref[pl.ds(r, nb), re0:re0 + sw] = n_re
            buf_ref[pl.ds(r, nb), im0:im0 + sw] = n_im
            return n_re, n_im

        x_re, x_im = lax.fori_loop(0, ts, step, (st_ref[:, re0:re0 + sw], st_ref[:, im0:im0 + sw]),
                                   unroll=8)
        st_ref[:, re0:re0 + sw] = x_re
        st_ref[:, im0:im0 + sw] = x_im
        ys.append(jnp.dot(buf_ref[:, re0:re0 + sw2].astype(BF16), cc_ref[c], preferred_element_type=F32))

    y = jnp.concatenate(ys, 1) + d_ref[...] * u
    g = jax.nn.gelu(y)
    z = jnp.dot(g.astype(BF16), wg_ref[...], preferred_element_type=F32) + bg_ref[...]
    y_ref[...] = g * jax.nn.sigmoid(z)


def _s5(u_tm, nb, lam_re, lam_im, log_dt, b_re, b_im, c_re, c_im, d_skip, w_glu, b_glu, ts=64):
    R, W = u_tm.shape
    G, P = lam_re.shape
    C = SSM_CH
    lr = jnp.minimum(lam_re, -1e-4)
    li = lam_im
    dt = jnp.exp(log_dt)[:, None]
    mag = jnp.exp(lr * dt)
    a_re = mag * jnp.cos(li * dt)
    a_im = mag * jnp.sin(li * dt)
    den = lr * lr + li * li
    z_re = ((a_re - 1.0) * lr + a_im * li) / den
    z_im = (a_im * lr - (a_re - 1.0) * li) / den
    bb_re = z_re[..., None] * b_re - z_im[..., None] * b_im
    bb_im = z_re[..., None] * b_im + z_im[..., None] * b_re
    gc = LANES // C
    nc = G // gc
    eye = jnp.eye(gc, dtype=F32)

    def block_diag_in(m):
        return jnp.einsum('ngpc,gh->ngchp', m.reshape(nc, gc, P, C), eye).reshape(nc, gc * C, gc * P)

    def block_diag_out(m):
        return jnp.einsum('ngcp,gh->ngphc', m.reshape(nc, gc, C, P), eye).reshape(nc, gc * P, gc * C)

    bb = jnp.concatenate([block_diag_in(bb_re), block_diag_in(bb_im)], 2).astype(BF16)
    cc = jnp.concatenate([block_diag_out(c_re), -block_diag_out(c_im)], 1).astype(BF16)
    ns = G * P
    rows = ts * nb
    kern = functools.partial(_s5_kernel, nb=nb, ts=ts)
    const = lambda i: (0, 0)
    const3 = lambda i: (0, 0, 0)
    return pl.pallas_call(
        kern,
        grid=(R // rows,),
        in_specs=[pl.BlockSpec((rows, W), lambda i: (i, 0)),
                  pl.BlockSpec(bb.shape, const3),
                  pl.BlockSpec(cc.shape, const3),
                  pl.BlockSpec((1, ns), const),
                  pl.BlockSpec((1, ns), const),
                  pl.BlockSpec((1, W), const),
                  pl.BlockSpec((W, W), const),
                  pl.BlockSpec((1, W), const)],
        out_specs=pl.BlockSpec((rows, W), lambda i: (i, 0)),
        out_shape=jax.ShapeDtypeStruct((R, W), F32),
        scratch_shapes=[pltpu.VMEM((rows, 2 * ns), F32), pltpu.VMEM((nb, 2 * ns), F32)],
        compiler_params=_params("arbitrary"),
        name="s5",
    )(u_tm, bb, cc, a_re.reshape(1, ns), a_im.reshape(1, ns), d_skip.reshape(1, W),
      w_glu.astype(BF16), b_glu.reshape(1, W))


def _stack_pair(q):
    first = _lane_iota((1, LANES)) < HEAD_DIM
    return jnp.concatenate([jnp.where(first, q, 0.0), jnp.where(first, 0.0, q)], 0)


def _unstack_pair(x):
    t = x.shape[0] // 2
    first = _lane_iota((1, LANES)) < HEAD_DIM
    return jnp.where(first, x[:t], x[t:])


def _sb_kernel(q_ref, k_ref, v_ref, o_ref):
    tq = q_ref.shape[0]
    qi = pl.program_id(2)
    n_pairs = q_ref.shape[1] // LANES
    groups = [slice(p * LANES, (p + 1) * LANES) for p in range(n_pairs)]
    qs = [_stack_pair(q_ref[:, g]).astype(BF16) for g in groups]
    tri = (_row_iota((tq, tq)) > _lane_iota((tq, tq))).astype(BF16)
    diag = _lane_iota((2 * tq, tq)) < (_row_iota((2 * tq, tq)) % tq)

    def tile(kj, carry, masked):
        r = pl.multiple_of(kj * tq, tq)
        zs = [_dot_nt(q, k_ref[pl.ds(r, tq), g]) for q, g in zip(qs, groups)]
        stage = []
        for z in zs:
            pos = jnp.maximum(z, 0.0)
            neg = z - pos
            log_term = jnp.log2(1.0 + jnp.exp2(neg - pos))
            sp = pos + log_term
            if masked:
                sp = jnp.where(diag, sp, 0.0)
            stage.append((neg - log_term, jnp.sum(sp, -1, keepdims=True), _dot(sp, tri)))
        out = []
        for (log_beta, row_sum, after), g, (acc, tail) in zip(stage, groups, carry):
            w = jnp.exp2(log_beta - (after + tail))
            if masked:
                w = jnp.where(diag, w, 0.0)
            out.append((acc + _dot(w, v_ref[pl.ds(r, tq), g]), tail + row_sum))
        return tuple(out)

    init = tuple((jnp.zeros((2 * tq, LANES), F32), jnp.zeros((2 * tq, 1), F32)) for _ in groups)
    carry = tile(qi, init, True)
    carry = lax.fori_loop(0, qi, lambda i, c: tile(qi - 1 - i, c, False), carry)
    for g, (acc, _) in zip(groups, carry):
        o_ref[:, g] = _unstack_pair(acc)


def _stick_breaking(qkv, nb, seq, q_col, k_col, v_col, n_pairs):
    tq = min(SB_TILE, seq)
    nq = seq // tq
    per = SB_PAIRS
    w = per * LANES
    assert n_pairs % per == 0 and q_col % per == 0 and k_col % per == 0 and v_col % per == 0
    return pl.pallas_call(
        _sb_kernel,
        grid=(nb, n_pairs // per, nq),
        in_specs=[pl.BlockSpec((tq, w), lambda b, p, i: (b * nq + i, q_col // per + p)),
                  pl.BlockSpec((seq, w), lambda b, p, i: (b, k_col // per + p)),
                  pl.BlockSpec((seq, w), lambda b, p, i: (b, v_col // per + p))],
        out_specs=pl.BlockSpec((tq, w), lambda b, p, i: (b * nq + i, p)),
        out_shape=jax.ShapeDtypeStruct((nb * seq, n_pairs * LANES), F32),
        compiler_params=_params("parallel", "parallel", "arbitrary"),
        name="stick_breaking",
    )(qkv, qkv, qkv)


def _dil_kernel(q_ref, k_ref, v_ref, o_ref, m_ref, l_ref, a_ref):
    seq = q_ref.shape[0]
    tq = Q_TILE
    first = _lane_iota((1, LANES)) < HEAD_DIM

    q_in = _row_iota((2 * tq, 1)) % tq

    def band_bias(wd, n_tiles):
        if n_tiles > 1:
            lag = q_in - (_lane_iota((1, 2 * tq)) - tq)
            return jnp.where((lag >= 0) & (lag <= wd), 0.0, NEG_BIG)
        return jnp.where(_lane_iota((1, tq)) <= q_in, 0.0, NEG_BIG)

    def tile(c, dil, band, n_tiles, r, i):
        base = r + dil * tq * i

        def rows(ref, start):
            if dil == 1:
                return ref[pl.ds(pl.multiple_of(start, tq), tq), :]
            return ref[pl.ds(start, tq, stride=dil), :]

        qs = _stack_pair(rows(q_ref, base)).astype(BF16)
        if n_tiles > 1:
            prev = r + dil * tq * jnp.maximum(i - 1, 0)
            kk = jnp.concatenate([rows(k_ref, prev), rows(k_ref, base)], 0)
            vv = jnp.concatenate([rows(v_ref, prev), rows(v_ref, base)], 0)
            no_prev = jnp.where((i == 0) & (_lane_iota((1, 2 * tq)) < tq), NEG_BIG, 0.0)
            bias = band + no_prev
        else:
            kk = rows(k_ref, base)
            vv = rows(v_ref, base)
            bias = band
        s = _dot_nt(qs, kk) + bias
        m = jnp.max(s, -1, keepdims=True)
        p = jnp.exp2(s - m)
        l = jnp.sum(p, -1, keepdims=True)
        acc = _dot(p, vv)
        m2 = jnp.where(first, m[:tq], m[tq:])
        l2 = jnp.where(first, l[:tq], l[tq:])
        a2 = jnp.where(first, acc[:tq], acc[tq:])
        if dil == 1:
            sl = pl.ds(pl.multiple_of(base, tq), tq)
        else:
            sl = pl.ds(base, tq, stride=dil)
        m_ref[c, sl, :] = m2
        l_ref[c, sl, :] = l2
        a_ref[c, sl, :] = a2

    for c, (window, dil) in enumerate(DIL_CONFIGS):
        wd = window // dil
        n_tiles = seq // dil // tq

        band = band_bias(wd, n_tiles)

        def per_tile(n, _, c=c, dil=dil, band=band, n_tiles=n_tiles):
            tile(c, dil, band, n_tiles, n // n_tiles, n % n_tiles)
            return 0

        lax.fori_loop(0, dil * n_tiles, per_tile, 0, unroll=16)

    def combine(i, _):
        sl = pl.ds(pl.multiple_of(i * tq, tq), tq)
        m0, m1, m2 = m_ref[0, sl, :], m_ref[1, sl, :], m_ref[2, sl, :]
        mx = jnp.maximum(jnp.maximum(m0, m1), m2)
        e0, e1, e2 = jnp.exp2(m0 - mx), jnp.exp2(m1 - mx), jnp.exp2(m2 - mx)
        num = e0 * a_ref[0, sl, :] + e1 * a_ref[1, sl, :] + e2 * a_ref[2, sl, :]
        den = e0 * l_ref[0, sl, :] + e1 * l_ref[1, sl, :] + e2 * l_ref[2, sl, :]
        o_ref[sl, :] = num / den
        return 0

    lax.fori_loop(0, seq // tq, combine, 0)


def _dilated(q_arr, q_col, k_arr, k_col, v_arr, v_col, nb, seq, n_pairs):
    return pl.pallas_call(
        _dil_kernel,
        grid=(nb, n_pairs),
        in_specs=[pl.BlockSpec((seq, LANES), lambda b, p: (b, q_col + p)),
                  pl.BlockSpec((seq, LANES), lambda b, p: (b, k_col + p)),
                  pl.BlockSpec((seq, LANES), lambda b, p: (b, v_col + p))],
        out_specs=pl.BlockSpec((seq, LANES), lambda b, p: (b, p)),
        out_shape=jax.ShapeDtypeStruct((nb * seq, n_pairs * LANES), F32),
        scratch_shapes=[pltpu.VMEM((3, seq, LANES), F32)] * 3,
        compiler_params=_params("parallel", "parallel"),
        name="dilated",
    )(q_arr, k_arr, v_arr)


def _cmp_kernel(t_ref, pe_ref, w1a_ref, w1b_ref, w2_ref, cos_ref, sin_ref, o_ref):
    nblk = t_ref.shape[0] // CMP_STRIDE
    j = pl.program_id(1)
    out = jnp.zeros((nblk, LANES), F32)
    for g in range(NSA_KV_HEADS):
        p1 = jnp.zeros((nblk, CMP_HIDDEN), F32)
        p2 = jnp.zeros((nblk, CMP_HIDDEN), F32)
        for l in range(CMP_STRIDE):
            x = t_ref[pl.ds(l, nblk, stride=CMP_STRIDE), :]
            p1 = p1 + _dot(x + pe_ref[0, l:l + 1, :], w1a_ref[0, g, l])
            p2 = p2 + _dot(x + pe_ref[0, CMP_STRIDE + l:CMP_STRIDE + l + 1, :], w1b_ref[0, g, l])
        hidden = p1 + pltpu.roll(p2, nblk - 1, axis=0)
        out = out + _dot(jax.nn.gelu(hidden), w2_ref[0, g])
    first = (_lane_iota((1, LANES)) % HEAD_DIM) < HALF
    partner = jnp.where(first, pltpu.roll(out, LANES - HALF, axis=1), pltpu.roll(out, HALF, axis=1))
    roped = out * cos_ref[...] + partner * sin_ref[...]
    o_ref[0] = jnp.where(j == 0, roped, out)


def _compress(p_arr, col0, nb, seq, pe, w1, w2, cos_c, sin_c):
    nblk = seq // CMP_STRIDE
    G = NSA_KV_HEADS
    pe2 = jnp.tile(pe, (1, 1, G))
    w1r = w1.reshape(2, CMP_LEN, HEAD_DIM, CMP_HIDDEN)
    w1e = jnp.zeros((2, G, CMP_LEN, LANES, CMP_HIDDEN), F32)
    w2e = jnp.zeros((2, G, CMP_HIDDEN, LANES), F32)
    for g in range(G):
        w1e = w1e.at[:, g, :, g * HEAD_DIM:(g + 1) * HEAD_DIM, :].set(w1r)
        w2e = w2e.at[:, g, :, g * HEAD_DIM:(g + 1) * HEAD_DIM].set(w2)
    w1e = w1e.astype(BF16)
    w2e = w2e.astype(BF16)
    return pl.pallas_call(
        _cmp_kernel,
        grid=(nb, 2),
        in_specs=[pl.BlockSpec((seq, LANES), lambda b, j: (b, col0 + j)),
                  pl.BlockSpec((1, CMP_LEN, LANES), lambda b, j: (j, 0, 0)),
                  pl.BlockSpec((1, G, CMP_STRIDE, LANES, CMP_HIDDEN), lambda b, j: (j, 0, 0, 0, 0)),
                  pl.BlockSpec((1, G, CMP_STRIDE, LANES, CMP_HIDDEN), lambda b, j: (j, 0, 1, 0, 0)),
                  pl.BlockSpec((1, G, CMP_HIDDEN, LANES), lambda b, j: (j, 0, 0, 0)),
                  pl.BlockSpec((nblk, LANES), lambda b, j: (0, 0)),
                  pl.BlockSpec((nblk, LANES), lambda b, j: (0, 0))],
        out_specs=pl.BlockSpec((1, nblk, LANES), lambda b, j: (j, b, 0)),
        out_shape=jax.ShapeDtypeStruct((2, nb * nblk, LANES), F32),
        compiler_params=_params("parallel", "parallel"),
        name="nsa_compress",
    )(p_arr, pe2, w1e, w1e, w2e, cos_c, sin_c)


def _nsa_kernel(q_ref, gate_ref, kc_ref, vc_ref, ks_ref, vs_ref, kw_ref, vw_ref, ovt_ref, o_ref):
    tq = q_ref.shape[0]
    R, G = NSA_REP, NSA_KV_HEADS
    seq = ks_ref.shape[0]
    n_sel = seq // SEL_BLOCK
    n_cmp = (seq - CMP_LEN) // CMP_STRIDE + 1
    nblk = kc_ref.shape[1]
    top_n = min(SEL_TOPN, n_sel)
    qi = pl.program_id(1)
    lane = _lane_iota((1, LANES))
    first = lane < HEAD_DIM
    mine = [first, jnp.logical_not(first)]
    t_row = qi * tq + _row_iota((tq, 1))

    qs = []
    for g in range(G):
        parts = []
        for r in range(R):
            h = g * R + r
            x = q_ref[:, (h // 2) * LANES:(h // 2 + 1) * LANES]
            parts.append(jnp.where(mine[g], x if h % 2 == g else _swap_halves(x), 0.0))
        qs.append(jnp.concatenate(parts, 0).astype(BF16))

    c_idx = _lane_iota((tq, nblk))
    visible = (c_idx < n_cmp) & (c_idx * CMP_STRIDE + (CMP_LEN - 1) <= qi * tq + _row_iota((tq, nblk)))
    c_bias = jnp.where(visible, 0.0, NEG_BIG)
    kc, vc = kc_ref[0], vc_ref[0]
    vc_swapped = _swap_halves(vc)
    s_cs = [_dot_nt(q, kc).reshape(R, tq, nblk) + c_bias for q in qs]
    p_cs, o_cs = [], []
    for g, s_c in enumerate(s_cs):
        m_c = jnp.max(s_c, -1, keepdims=True)
        m_c = jnp.where(m_c > 0.5 * NEG_BIG, m_c, 0.0)
        p_c = jnp.exp2(s_c - m_c)
        p_c = p_c * (1.0 / jnp.maximum(jnp.sum(p_c, -1, keepdims=True), 1e-30))
        p_cs.append(p_c)
        o_cs.append(_dot(p_c.reshape(R * tq, nblk), jnp.where(mine[g], vc, vc_swapped)).reshape(R, tq, LANES))

    n_idx = _row_iota((n_sel, tq))
    t_q = qi * tq + _lane_iota((n_sel, tq))
    cur = t_q // SEL_BLOCK
    future = n_idx * SEL_BLOCK > t_q
    forced = (n_idx == 0) | (n_idx == cur) | (n_idx == cur - 1)
    sels = []
    for p_c in p_cs:
        psum = jnp.sum(p_c, 0)
        p_hi = psum.astype(BF16)
        p_lo = (psum - p_hi.astype(F32)).astype(BF16)
        imp = (_dot_nt(ovt_ref[...], p_hi) + _dot_nt(ovt_ref[...], p_lo))[:n_sel]
        imp = jnp.where(forced, 1e9, jnp.where(future, -1.0, imp))
        rank = jnp.zeros((n_sel, tq), F32)
        for m in range(n_sel):
            row = imp[m:m + 1, :]
            earlier = (n_idx > m).astype(F32)
            rank = rank + jnp.where(row > imp, 1.0, jnp.where(row == imp, earlier, 0.0))
        sel_t = jnp.where((rank < top_n) & (imp > -0.5), 1.0, 0.0)
        sel_t = jnp.concatenate([sel_t, jnp.zeros((LANES - n_sel, tq), F32)], 0)
        sels.append(sel_t.T.astype(BF16))

    k_col = _lane_iota((tq, tq))
    blk_row = _row_iota((LANES, tq))
    blk_of_key = _lane_iota((LANES, tq)) // SEL_BLOCK

    def sel_biases(j, causal):
        expand = (blk_row == (j * (tq // SEL_BLOCK) + blk_of_key)).astype(BF16)
        out = []
        for sel in sels:
            chosen = jnp.dot(sel, expand, preferred_element_type=F32)
            bias = (1.0 - chosen) * NEG_BIG
            if causal:
                bias = jnp.where((j * tq + k_col) <= t_row, bias, NEG_BIG)
            out.append(bias)
        return out

    def win_bias(j):
        diff = t_row - (j * tq + k_col)
        return jnp.where((diff >= 0) & (diff < NSA_WINDOW), 0.0, NEG_BIG)

    def tile_of(ref, j):
        return ref[pl.ds(pl.multiple_of(j * tq, tq), tq), :]

    def scores(k_ref, j, biases):
        kt = tile_of(k_ref, j).astype(BF16)
        return [_dot_nt(q, kt).reshape(R, tq, tq) + b for q, b in zip(qs, biases)]

    def update(carry, s_groups, v_ref, j):
        v = tile_of(v_ref, j)
        out = []
        for g, ((m_run, acc), s) in enumerate(zip(carry, s_groups)):
            m_new = jnp.maximum(m_run, jnp.max(s, -1, keepdims=True))
            alpha = jnp.exp2(m_run - m_new)
            p = jnp.exp2(s - m_new)
            pv = _dot(p.reshape(R * tq, tq), jnp.where(mine[g], v, 1.0))
            out.append((m_new, alpha * acc + pv.reshape(R, tq, LANES)))
        return tuple(out)

    def normalise(carry):
        out = []
        for g, (_, acc) in enumerate(carry):
            swapped = _swap_halves(acc)
            out.append(jnp.where(mine[g], acc, swapped) / jnp.where(mine[g], swapped, acc))
        return out

    init = tuple((jnp.full((R, tq, 1), NEG_BIG, F32), jnp.zeros((R, tq, LANES), F32)) for _ in range(G))
    j_win = jnp.maximum(qi - NSA_WINDOW // tq, 0)

    def selected_only(j, carry):
        return update(carry, scores(ks_ref, j, sel_biases(j, False)), vs_ref, j)

    def selected_and_window(j, carry):
        b_w = win_bias(j)
        s_s, s_w = scores(ks_ref, j, sel_biases(j, True)), scores(kw_ref, j, [b_w] * G)
        return update(carry[0], s_s, vs_ref, j), update(carry[1], s_w, vw_ref, j)

    c_sel = lax.fori_loop(0, j_win, selected_only, init)
    c_sel, c_win = lax.fori_loop(j_win, qi + 1, selected_and_window, (c_sel, init))
    o_ss, o_ws = normalise(c_sel), normalise(c_win)

    gate = jax.nn.sigmoid(gate_ref[...])
    outs = []
    for g in range(G):
        for r in range(R):
            c0 = g * LANES + 3 * r
            outs.append(gate[:, c0:c0 + 1] * o_cs[g][r] + gate[:, c0 + 1:c0 + 2] * o_ss[g][r]
                        + gate[:, c0 + 2:c0 + 3] * o_ws[g][r])
    for pair in range(G * R // 2):
        o_ref[:, pair * LANES:(pair + 1) * LANES] = jnp.where(first, outs[2 * pair], outs[2 * pair + 1])


def _nsa(r_arr, p_arr, cmp_kv, overlap, nb, seq, cols):
    tq = min(NSA_TILE, seq)
    nq = seq // tq
    nblk = seq // CMP_STRIDE
    G = NSA_KV_HEADS
    qw = G * NSA_REP * HEAD_DIM
    gw = G * LANES
    assert (cols['q'] * LANES) % qw == 0 and (cols['gate'] * LANES) % gw == 0
    return pl.pallas_call(
        _nsa_kernel,
        grid=(nb, nq),
        in_specs=[pl.BlockSpec((tq, qw), lambda b, i: (b * nq + i, cols['q'] * LANES // qw)),
                  pl.BlockSpec((tq, gw), lambda b, i: (b * nq + i, cols['gate'] * LANES // gw)),
                  pl.BlockSpec((1, nblk, LANES), lambda b, i: (0, b, 0)),
                  pl.BlockSpec((1, nblk, LANES), lambda b, i: (1, b, 0)),
                  pl.BlockSpec((seq, LANES), lambda b, i: (b, cols['k_slc'])),
                  pl.BlockSpec((seq, LANES), lambda b, i: (b, cols['v_slc'])),
                  pl.BlockSpec((seq, LANES), lambda b, i: (b, cols['k_win'])),
                  pl.BlockSpec((seq, LANES), lambda b, i: (b, cols['v_win'])),
                  pl.BlockSpec((LANES, nblk), lambda b, i: (0, 0))],
        out_specs=pl.BlockSpec((tq, qw), lambda b, i: (b * nq + i, 0)),
        out_shape=jax.ShapeDtypeStruct((nb * seq, qw), F32),
        compiler_params=_params("parallel", "arbitrary"),
        name="nsa",
    )(r_arr, p_arr, cmp_kv, cmp_kv, r_arr, p_arr, r_arr, p_arr, overlap)


def _mix_out_kernel(ya_ref, yb_ref, yc_ref, yd_ref, ng_ref, w_ref, h_ref, g_ref, b_ref,
                    o_ref, ob_ref, *, alpha):
    fs = []
    for rows in _row_chunks(o_ref.shape[0]):
        ys = []
        for n, y_ref in enumerate((ya_ref, yb_ref, yc_ref, yd_ref)):
            y = y_ref[rows, :]
            y = y * lax.rsqrt(jnp.mean(y * y, -1, keepdims=True) + RMS_EPS) * ng_ref[n:n + 1, :]
            ys.append(y.astype(BF16))
        fs.append(jnp.dot(jnp.concatenate(ys, 1), w_ref[...], preferred_element_type=F32))
    for rows, f in zip(_row_chunks(o_ref.shape[0]), fs):
        out = _layer_norm(alpha * h_ref[rows, :] + f, g_ref[...], b_ref[...])
        o_ref[rows, :] = out
        ob_ref[rows, :] = out.astype(BF16)


def _mix_out(ya_tm, yb, yc, yd, norm_g, w_out, h, ln_g, ln_b, nb, seq, alpha, tm):
    T, D = h.shape
    gw = yb.shape[1]
    ns = seq // tm
    ya_view = ya_tm.reshape(seq, nb * gw)
    tok = lambda i: (i, 0)
    const = lambda i: (0, 0)
    return pl.pallas_call(
        functools.partial(_mix_out_kernel, alpha=alpha),
        grid=(T // tm,),
        in_specs=[pl.BlockSpec((tm, gw), lambda i: (i % ns, i // ns)),
                  pl.BlockSpec((tm, gw), tok), pl.BlockSpec((tm, gw), tok), pl.BlockSpec((tm, gw), tok),
                  pl.BlockSpec((4, gw), const),
                  pl.BlockSpec((4 * gw, D), const),
                  pl.BlockSpec((tm, D), tok),
                  pl.BlockSpec((1, D), const), pl.BlockSpec((1, D), const)],
        out_specs=[pl.BlockSpec((tm, D), tok), pl.BlockSpec((tm, D), tok)],
        out_shape=[jax.ShapeDtypeStruct((T, D), F32), jax.ShapeDtypeStruct((T, D), BF16)],
        compiler_params=_params("parallel"),
        name="mix_out",
    )(ya_view, yb, yc, yd, norm_g, w_out, h, ln_g.reshape(1, D), ln_b.reshape(1, D))


def _xattn_kernel(hb_ref, h_ref, wq_ref, kv_ref, wo_ref, g_ref, b_ref, o_ref, ob_ref, *, alpha):
    width = wq_ref.shape[1]
    chunks = _row_chunks(o_ref.shape[0])
    qs = [jnp.dot(hb_ref[rows, :], wq_ref[...], preferred_element_type=F32) for rows in chunks]
    fs = []
    for q in qs:
        heads = []
        for hd in range(XA_HEADS):
            lo = hd * XA_HEAD_DIM
            k = kv_ref[:, lo:lo + XA_HEAD_DIM]
            v = kv_ref[:, width + lo:width + lo + XA_HEAD_DIM]
            s = _dot_nt(q[:, lo:lo + XA_HEAD_DIM], k) * (XA_HEAD_DIM ** -0.5)
            p = jnp.exp(s - jnp.max(s, -1, keepdims=True))
            p = p * (1.0 / jnp.sum(p, -1, keepdims=True))
            heads.append(_dot(p, v).astype(BF16))
        fs.append(jnp.dot(jnp.concatenate(heads, 1), wo_ref[...], preferred_element_type=F32))
    for rows, f in zip(chunks, fs):
        out = _layer_norm(alpha * h_ref[rows, :] + f, g_ref[...], b_ref[...])
        o_ref[rows, :] = out
        ob_ref[rows, :] = out.astype(BF16)


def _xattn(hb, h, wq, kv, wo, ln_g, ln_b, seq, mem_len, alpha, tm):
    T, D = h.shape
    width = wq.shape[1]
    ns = seq // tm
    tok = lambda i: (i, 0)
    const = lambda i: (0, 0)
    return pl.pallas_call(
        functools.partial(_xattn_kernel, alpha=alpha),
        grid=(T // tm,),
        in_specs=[pl.BlockSpec((tm, D), tok), pl.BlockSpec((tm, D), tok),
                  pl.BlockSpec((D, width), const),
                  pl.BlockSpec((mem_len, 2 * width), lambda i: (i // ns, 0)),
                  pl.BlockSpec((width, D), const),
                  pl.BlockSpec((1, D), const), pl.BlockSpec((1, D), const)],
        out_specs=[pl.BlockSpec((tm, D), tok), pl.BlockSpec((tm, D), tok)],
        out_shape=[jax.ShapeDtypeStruct((T, D), F32), jax.ShapeDtypeStruct((T, D), BF16)],
        compiler_params=_params("parallel"),
        name="cross_attn",
    )(hb, h, wq, kv, wo, ln_g.reshape(1, D), ln_b.reshape(1, D))


def _ffn_kernel(hb_ref, h_ref, wg_ref, wu_ref, wd_ref, g_ref, b_ref, o_ref, ob_ref, acc_ref, *, alpha):
    j = pl.program_id(1)

    @pl.when(j == 0)
    def _():
        acc_ref[...] = alpha * h_ref[...]

    x = hb_ref[...]
    gate = jnp.dot(x, wg_ref[...], preferred_element_type=F32)
    up = jnp.dot(x, wu_ref[...], preferred_element_type=F32)
    act = (jax.nn.silu(gate) * up).astype(BF16)
    acc_ref[...] += jnp.dot(act, wd_ref[...], preferred_element_type=F32)

    @pl.when(j == pl.num_programs(1) - 1)
    def _():
        out = _layer_norm(acc_ref[...], g_ref[...], b_ref[...])
        o_ref[...] = out
        ob_ref[...] = out.astype(BF16)


def _ffn(hb, h, wg, wu, wd, ln_g, ln_b, alpha, tm, th):
    T, D = h.shape
    H = wg.shape[1]
    tok = lambda i, j: (i, 0)
    const = lambda i, j: (0, 0)
    return pl.pallas_call(
        functools.partial(_ffn_kernel, alpha=alpha),
        grid=(T // tm, H // th),
        in_specs=[pl.BlockSpec((tm, D), tok), pl.BlockSpec((tm, D), tok),
                  pl.BlockSpec((D, th), lambda i, j: (0, j)),
                  pl.BlockSpec((D, th), lambda i, j: (0, j)),
                  pl.BlockSpec((th, D), lambda i, j: (j, 0)),
                  pl.BlockSpec((1, D), const), pl.BlockSpec((1, D), const)],
        out_specs=[pl.BlockSpec((tm, D), tok), pl.BlockSpec((tm, D), tok)],
        out_shape=[jax.ShapeDtypeStruct((T, D), F32), jax.ShapeDtypeStruct((T, D), BF16)],
        scratch_shapes=[pltpu.VMEM((tm, D), F32)],
        compiler_params=_params("parallel", "arbitrary"),
        name="ffn",
    )(hb, h, wg, wu, wd, ln_g.reshape(1, D), ln_b.reshape(1, D))


def _rope_tables(pos):
    inv_freq = ROPE_THETA ** (-jnp.arange(HALF, dtype=F32) / HALF)
    ang = pos.astype(F32)[:, None] * inv_freq[None, :]
    cos = jnp.tile(jnp.cos(ang), (1, LANES // HALF))
    sin = jnp.sin(ang)
    sin = jnp.tile(jnp.concatenate([-sin, sin], 1), (1, LANES // HEAD_DIM))
    return cos, sin


def _overlap_table(seq):
    n_cmp = (seq - CMP_LEN) // CMP_STRIDE + 1
    n_sel = seq // SEL_BLOCK
    ci = np.arange(n_cmp)[:, None] * CMP_STRIDE
    sj = np.arange(n_sel)[None, :] * SEL_BLOCK
    ov = np.clip(np.minimum(ci + CMP_LEN, sj + SEL_BLOCK) - np.maximum(ci, sj), 0, None) / CMP_LEN
    full = np.zeros((LANES, seq // CMP_STRIDE), np.float32)
    full[:n_sel, :n_cmp] = ov.T
    return jnp.asarray(full, dtype=BF16)


def _split_w_in(w_in_stack, layer, width):
    hw = width // 4
    kvw = NSA_KV_HEADS * HEAD_DIM
    n_gate = 3 * (hw // HEAD_DIM)
    o = np.cumsum([0, hw, hw, 6 * kvw, n_gate, 3 * hw, 3 * hw])
    col_scale = np.ones((o[-1],), np.float32)
    for q_lo in (o[1], o[4], o[5]):
        col_scale[q_lo:q_lo + hw] = HEAD_DIM ** -0.5 * math.log2(math.e)
    w_in = _layer_bf16(w_in_stack, layer, jnp.asarray(col_scale))
    u = w_in[:, o[0]:o[1]]
    nq = w_in[:, o[1]:o[2]]
    kv = [w_in[:, o[2] + j * kvw:o[2] + (j + 1) * kvw] for j in range(6)]
    gate = w_in[:, o[3]:o[4]]
    sb = w_in[:, o[4]:o[5]]
    dil = w_in[:, o[5]:o[6]]
    per_group = 3 * NSA_REP
    gates = [jnp.pad(gate[:, g * per_group:(g + 1) * per_group], ((0, 0), (0, LANES - per_group)))
             for g in range(NSA_KV_HEADS)]
    w_rope = jnp.concatenate([nq, kv[2], kv[4], dil[:, :2 * hw]], 1)
    w_plain = jnp.concatenate([sb, dil[:, 2 * hw:], kv[0], kv[1], kv[3], kv[5]] + gates, 1)
    return u, w_rope, w_plain


R_NQ, R_KSLC, R_KWIN, R_DILQ, R_DILK = 0, 4, 5, 6, 10
P_SBQ, P_SBK, P_SBV, P_DILV, P_KCMP, P_VSLC, P_VWIN, P_GATE = 0, 4, 8, 12, 16, 18, 19, 20


def _hybrid_mixer(hb, h, nb, seq, w_in_stack, layer, s5_params, cmp_pe, cmp_w1, cmp_w2, norm_g, w_out,
                  ln_g, ln_b, tables, alpha, tm):
    cos, sin, cos_c, sin_c, overlap = tables
    width = w_out.shape[0]
    n_pairs = width // 4 // LANES
    w_u, w_rope, w_plain = _split_w_in(w_in_stack, layer, width)
    u_tm = _proj(hb, w_u, tm, w_u.shape[1], seq=seq, time_major_batch=nb)
    r_arr = _proj(hb, w_rope, min(2 * tm, seq), w_rope.shape[1], seq=seq, rope=(cos, sin))
    p_arr = _proj(hb, w_plain, tm, w_plain.shape[1])
    y_a = _s5(u_tm, nb, *s5_params)
    cmp_kv = _compress(p_arr, P_KCMP, nb, seq, cmp_pe, cmp_w1, cmp_w2, cos_c, sin_c)
    y_b = _nsa(r_arr, p_arr, cmp_kv, overlap, nb, seq,
               dict(q=R_NQ, gate=P_GATE, k_slc=R_KSLC, v_slc=P_VSLC, k_win=R_KWIN, v_win=P_VWIN))
    y_c = _stick_breaking(p_arr, nb, seq, P_SBQ, P_SBK, P_SBV, n_pairs)
    y_d = _dilated(r_arr, R_DILQ, r_arr, R_DILK, p_arr, P_DILV, nb, seq, n_pairs)
    return _mix_out(y_a, y_b, y_c, y_d, norm_g, w_out, h, ln_g, ln_b, nb, seq, alpha, tm)


def kernel(x, mem, ln_in_g, ln_in_b, w_in, s5_lambda_re, s5_lambda_im, s5_log_dt, s5_b_re, s5_b_im, s5_c_re, s5_c_im, s5_d, s5_w_glu, s5_b_glu, nsa_cmp_pe, nsa_cmp_w1, nsa_cmp_w2, mix_norm_g, w_out, ln1_g, ln1_b, xa_wq, xa_wkv, xa_wo, ln2_g, ln2_b, ffn_w_gate, ffn_w_up, ffn_w_down, ln3_g, ln3_b):
    nb, seq, d_model = x.shape
    mem_len = mem.shape[1]
    depth = w_in.shape[0]
    alpha = (2 * depth) ** 0.25
    tm = min(512, seq)
    pos = jnp.arange(seq)
    cos, sin = _rope_tables(pos)
    nblk = seq // CMP_STRIDE
    cos_c, sin_c = _rope_tables(jnp.arange(nblk) * CMP_STRIDE + CMP_LEN - 1)
    tables = (cos, sin, cos_c, sin_c, _overlap_table(seq))
    mem_b = mem.reshape(nb * mem_len, d_model).astype(BF16)

    h, hb = _ln_in(x.reshape(nb * seq, d_model), ln_in_g, ln_in_b, tm)
    for l in range(depth):
        s5_params = (s5_lambda_re[l], s5_lambda_im[l], s5_log_dt[l], s5_b_re[l], s5_b_im[l],
                     s5_c_re[l], s5_c_im[l], s5_d[l], s5_w_glu[l], s5_b_glu[l])
        h, hb = _hybrid_mixer(hb, h, nb, seq, w_in, l, s5_params, nsa_cmp_pe[l], nsa_cmp_w1[l],
                              nsa_cmp_w2[l], mix_norm_g[l], _layer_bf16(w_out, l), ln1_g[l], ln1_b[l],
                              tables, alpha, tm)
        kv = _proj(mem_b, _layer_bf16(xa_wkv, l), min(512, nb * mem_len), xa_wkv.shape[2] // 2)
        h, hb = _xattn(hb, h, _layer_bf16(xa_wq, l), kv, _layer_bf16(xa_wo, l), ln2_g[l], ln2_b[l],
                       seq, mem_len, alpha, tm)
        h, hb = _ffn(hb, h, _layer_bf16(ffn_w_gate, l), _layer_bf16(ffn_w_up, l),
                     _layer_bf16(ffn_w_down, l), ln3_g[l], ln3_b[l], alpha, tm, 512)
    return h.reshape(nb, seq, d_model)
```

```python
import functools
import math

import numpy as np
import jax
import jax.numpy as jnp
from jax import lax
from jax.experimental import pallas as pl
from jax.experimental.pallas import tpu as pltpu

F32 = jnp.float32
BF16 = jnp.bfloat16

LANES = 128
MXU_WIDTH = 256
CAST_ROWS = 512
VMEM_LIMIT = 56 * 1024 * 1024

HEAD_DIM = 64
HALF = HEAD_DIM // 2
ROPE_THETA = 10000.0
LN_EPS = 1e-5
RMS_EPS = 1e-6
SSM_CH = 16
SSM_STATE = 64
NSA_KV_HEADS = 2
NSA_REP = 4
CMP_LEN = 32
CMP_STRIDE = 16
CMP_HIDDEN = 128
SEL_BLOCK = 64
SEL_TOPN = 8
NSA_WINDOW = 512
DIL_CONFIGS = ((128, 1), (512, 4), (2048, 16))
XA_HEADS = 4
XA_HEAD_DIM = 128
Q_TILE = 128
SB_TILE = 256
SB_PAIRS = 4
NSA_TILE = 256
NEG_BIG = -1e30


def _params(*sem):
    return pltpu.CompilerParams(dimension_semantics=sem, vmem_limit_bytes=VMEM_LIMIT)


def _dot(a, b):
    return jnp.dot(a.astype(BF16), b.astype(BF16), preferred_element_type=F32)


def _dot_nt(a, b):
    return lax.dot_general(a.astype(BF16), b.astype(BF16), (((1,), (1,)), ((), ())),
                           preferred_element_type=F32)


def _layer_norm(x, g, b):
    mu = jnp.mean(x, -1, keepdims=True)
    xc = x - mu
    var = jnp.mean(xc * xc, -1, keepdims=True)
    return xc * lax.rsqrt(var + LN_EPS) * g + b


def _lane_iota(shape):
    return lax.broadcasted_iota(jnp.int32, shape, len(shape) - 1)


def _row_iota(shape):
    return lax.broadcasted_iota(jnp.int32, shape, len(shape) - 2)


def _swap_halves(x):
    return pltpu.roll(x, HEAD_DIM, axis=x.ndim - 1)


def _ln_in_kernel(x_ref, g_ref, b_ref, h_ref, hb_ref):
    y = _layer_norm(x_ref[...], g_ref[...], b_ref[...])
    h_ref[...] = y
    hb_ref[...] = y.astype(BF16)


def _ln_in(x2, g, b, tm):
    T, D = x2.shape
    return pl.pallas_call(
        _ln_in_kernel,
        grid=(T // tm,),
        in_specs=[pl.BlockSpec((tm, D), lambda i: (i, 0)),
                  pl.BlockSpec((1, D), lambda i: (0, 0)),
                  pl.BlockSpec((1, D), lambda i: (0, 0))],
        out_specs=[pl.BlockSpec((tm, D), lambda i: (i, 0)),
                   pl.BlockSpec((tm, D), lambda i: (i, 0))],
        out_shape=[jax.ShapeDtypeStruct((T, D), F32), jax.ShapeDtypeStruct((T, D), BF16)],
        compiler_params=_params("parallel"),
        name="ln_in",
    )(x2, g.reshape(1, D), b.reshape(1, D))


def _cast_kernel(w_ref, o_ref):
    o_ref[...] = w_ref[...].astype(BF16)


def _cast_scaled_kernel(w_ref, s_ref, o_ref):
    o_ref[...] = (w_ref[...] * s_ref[...]).astype(BF16)


def _layer_bf16(w_stack, layer, col_scale=None):
    _, K, N = w_stack.shape
    tk = min(CAST_ROWS, K)
    in_specs = [pl.BlockSpec((None, tk, N), lambda i: (layer, i, 0))]
    args = [w_stack]
    kern = _cast_kernel
    if col_scale is not None:
        in_specs.append(pl.BlockSpec((1, N), lambda i: (0, 0)))
        args.append(col_scale.reshape(1, N))
        kern = _cast_scaled_kernel
    return pl.pallas_call(
        kern, grid=(K // tk,), in_specs=in_specs,
        out_specs=pl.BlockSpec((tk, N), lambda i: (i, 0)),
        out_shape=jax.ShapeDtypeStruct((K, N), BF16),
        compiler_params=_params("parallel"), name="cast_bf16",
    )(*args)


def _row_chunks(n, parts=2):
    step = n // parts
    return [slice(i * step, (i + 1) * step) for i in range(parts)]


def _column_chunks(n):
    return [(lo, min(lo + MXU_WIDTH, n)) for lo in range(0, n, MXU_WIDTH)]


def _proj_kernel(a_ref, w_ref, o_ref):
    a = a_ref[...]
    for lo, hi in _column_chunks(o_ref.shape[1]):
        o_ref[:, lo:hi] = jnp.dot(a, w_ref[:, lo:hi], preferred_element_type=F32)


def _proj_rope_kernel(a_ref, w_ref, cos_ref, sin_ref, o_ref):
    a = a_ref[...]
    cos = cos_ref[...]
    sin = sin_ref[...]
    first = (_lane_iota((1, LANES)) % HEAD_DIM) < HALF
    for lo, hi in _column_chunks(o_ref.shape[1]):
        acc = jnp.dot(a, w_ref[:, lo:hi], preferred_element_type=F32)
        for c in range((hi - lo) // LANES):
            x = acc[:, c * LANES:(c + 1) * LANES]
            partner = jnp.where(first, pltpu.roll(x, LANES - HALF, axis=1), pltpu.roll(x, HALF, axis=1))
            o_ref[:, lo + c * LANES:lo + (c + 1) * LANES] = x * cos + partner * sin


def _proj(a, w, tm, tn, seq=None, rope=None):
    M, K = a.shape
    N = w.shape[1]
    nm, nn = M // tm, N // tn
    in_specs = [pl.BlockSpec((tm, K), lambda j, i: (i, 0)),
                pl.BlockSpec((K, tn), lambda j, i: (0, j))]
    args = [a, w]
    kern = _proj_kernel
    if rope is not None:
        ns = seq // tm
        in_specs += [pl.BlockSpec((tm, LANES), lambda j, i: (i % ns, 0))] * 2
        args += list(rope)
        kern = _proj_rope_kernel
    return pl.pallas_call(
        kern, grid=(nn, nm), in_specs=in_specs,
        out_specs=pl.BlockSpec((tm, tn), lambda j, i: (i, j)),
        out_shape=jax.ShapeDtypeStruct((M, N), F32),
        compiler_params=_params("parallel", "parallel"), name="proj",
    )(*args)


def _s5_kernel(h_ref, wu_ref, bb_ref, cc_ref, are_ref, aim_ref, d_ref, wg_ref, bg_ref, y_ref,
               buf_ref, st_ref, tm_ref, *, nb, ts):
    n_chunks, cw, sw2 = bb_ref.shape
    sw = sw2 // 2
    assert cw == tm_ref.shape[2]

    @pl.when(pl.program_id(0) == 0)
    def _():
        st_ref[...] = jnp.zeros_like(st_ref)

    u_bt = jnp.dot(h_ref[...].reshape(nb * ts, h_ref.shape[2]), wu_ref[...],
                   preferred_element_type=F32)
    for b in range(nb):
        for c in range(n_chunks):
            tm_ref[c, pl.ds(b, ts, stride=nb), :] = u_bt[b * ts:(b + 1) * ts, c * cw:(c + 1) * cw]
    u = jnp.concatenate([tm_ref[c] for c in range(n_chunks)], 1)
    ys = []
    for c in range(n_chunks):
        re0, im0 = c * sw2, c * sw2 + sw
        buf_ref[:, re0:re0 + sw2] = jnp.dot(u[:, c * cw:(c + 1) * cw].astype(BF16), bb_ref[c],
                                            preferred_element_type=F32)
        a_re = jnp.broadcast_to(are_ref[:, c * sw:(c + 1) * sw], (nb, sw))
        a_im = jnp.broadcast_to(aim_ref[:, c * sw:(c + 1) * sw], (nb, sw))

        def step(t, carry, re0=re0, im0=im0, a_re=a_re, a_im=a_im):
            x_re, x_im = carry
            r = pl.multiple_of(t * nb, nb)
            n_re = a_re * x_re - a_im * x_im + buf_ref[pl.ds(r, nb), re0:re0 + sw]
            n_im = a_re * x_im + a_im * x_re + buf_ref[pl.ds(r, nb), im0:im0 + sw]
            buf_ref[pl.ds(r, nb), re0:re0 + sw] = n_re
            buf_ref[pl.ds(r, nb), im0:im0 + sw] = n_im
            return n_re, n_im

        x_re, x_im = lax.fori_loop(0, ts, step, (st_ref[:, re0:re0 + sw], st_ref[:, im0:im0 + sw]),
                                   unroll=8)
        st_ref[:, re0:re0 + sw] = x_re
        st_ref[:, im0:im0 + sw] = x_im
        ys.append(jnp.dot(buf_ref[:, re0:re0 + sw2].astype(BF16), cc_ref[c], preferred_element_type=F32))

    y = jnp.concatenate(ys, 1) + d_ref[...] * u
    g = jax.nn.gelu(y)
    z = jnp.dot(g.astype(BF16), wg_ref[...], preferred_element_type=F32) + bg_ref[...]
    out = g * jax.nn.sigmoid(z)
    for c in range(n_chunks):
        tm_ref[c] = out[:, c * cw:(c + 1) * cw]
    for b in range(nb):
        for c in range(n_chunks):
            y_ref[b, :, c * cw:(c + 1) * cw] = tm_ref[c, pl.ds(b, ts, stride=nb), :]


def _s5(hb, w_u, nb, seq, lam_re, lam_im, log_dt, b_re, b_im, c_re, c_im, d_skip, w_glu, b_glu, ts=64):
    D, W = w_u.shape
    G, P = lam_re.shape
    C = SSM_CH
    lr = jnp.minimum(lam_re, -1e-4)
    li = lam_im
    dt = jnp.exp(log_dt)[:, None]
    mag = jnp.exp(lr * dt)
    a_re = mag * jnp.cos(li * dt)
    a_im = mag * jnp.sin(li * dt)
    den = lr * lr + li * li
    z_re = ((a_re - 1.0) * lr + a_im * li) / den
    z_im = (a_im * lr - (a_re - 1.0) * li) / den
    bb_re = z_re[..., None] * b_re - z_im[..., None] * b_im
    bb_im = z_re[..., None] * b_im + z_im[..., None] * b_re
    gc = LANES // C
    nc = G // gc
    eye = jnp.eye(gc, dtype=F32)

    def block_diag_in(m):
        return jnp.einsum('ngpc,gh->ngchp', m.reshape(nc, gc, P, C), eye).reshape(nc, gc * C, gc * P)

    def block_diag_out(m):
        return jnp.einsum('ngcp,gh->ngphc', m.reshape(nc, gc, C, P), eye).reshape(nc, gc * P, gc * C)

    bb = jnp.concatenate([block_diag_in(bb_re), block_diag_in(bb_im)], 2).astype(BF16)
    cc = jnp.concatenate([block_diag_out(c_re), -block_diag_out(c_im)], 1).astype(BF16)
    ns = G * P
    rows = ts * nb
    kern = functools.partial(_s5_kernel, nb=nb, ts=ts)
    const = lambda i: (0, 0)
    const3 = lambda i: (0, 0, 0)
    y = pl.pallas_call(
        kern,
        grid=(seq // ts,),
        in_specs=[pl.BlockSpec((nb, ts, D), lambda i: (0, i, 0)),
                  pl.BlockSpec((D, W), const),
                  pl.BlockSpec(bb.shape, const3),
                  pl.BlockSpec(cc.shape, const3),
                  pl.BlockSpec((1, ns), const),
                  pl.BlockSpec((1, ns), const),
                  pl.BlockSpec((1, W), const),
                  pl.BlockSpec((W, W), const),
                  pl.BlockSpec((1, W), const)],
        out_specs=pl.BlockSpec((nb, ts, W), lambda i: (0, i, 0)),
        out_shape=jax.ShapeDtypeStruct((nb, seq, W), F32),
        scratch_shapes=[pltpu.VMEM((rows, 2 * ns), F32), pltpu.VMEM((nb, 2 * ns), F32),
                        pltpu.VMEM((nc, rows, LANES), F32)],
        compiler_params=_params("arbitrary"),
        name="s5",
    )(hb.reshape(nb, seq, D), w_u, bb, cc, a_re.reshape(1, ns), a_im.reshape(1, ns),
      d_skip.reshape(1, W), w_glu.astype(BF16), b_glu.reshape(1, W))
    return y.reshape(nb * seq, W)


def _stack_pair(q):
    first = _lane_iota((1, LANES)) < HEAD_DIM
    return jnp.concatenate([jnp.where(first, q, 0.0), jnp.where(first, 0.0, q)], 0)


def _unstack_pair(x):
    t = x.shape[0] // 2
    first = _lane_iota((1, LANES)) < HEAD_DIM
    return jnp.where(first, x[:t], x[t:])


def _sb_kernel(q_ref, k_ref, v_ref, o_ref):
    tq = q_ref.shape[0]
    qi = pl.program_id(2)
    n_pairs = q_ref.shape[1] // LANES
    groups = [slice(p * LANES, (p + 1) * LANES) for p in range(n_pairs)]
    qs = [_stack_pair(q_ref[:, g]).astype(BF16) for g in groups]
    tri = (_row_iota((tq, tq)) > _lane_iota((tq, tq))).astype(BF16)
    diag = _lane_iota((2 * tq, tq)) < (_row_iota((2 * tq, tq)) % tq)

    def tile(kj, carry, masked):
        r = pl.multiple_of(kj * tq, tq)
        zs = [_dot_nt(q, k_ref[pl.ds(r, tq), g]) for q, g in zip(qs, groups)]
        stage = []
        for z in zs:
            pos = jnp.maximum(z, 0.0)
            neg = z - pos
            log_term = jnp.log2(1.0 + jnp.exp2(neg - pos))
            sp = pos + log_term
            if masked:
                sp = jnp.where(diag, sp, 0.0)
            stage.append((neg - log_term, jnp.sum(sp, -1, keepdims=True), _dot(sp, tri)))
        out = []
        for (log_beta, row_sum, after), g, (acc, tail) in zip(stage, groups, carry):
            w = jnp.exp2(log_beta - (after + tail))
            if masked:
                w = jnp.where(diag, w, 0.0)
            out.append((acc + _dot(w, v_ref[pl.ds(r, tq), g]), tail + row_sum))
        return tuple(out)

    init = tuple((jnp.zeros((2 * tq, LANES), F32), jnp.zeros((2 * tq, 1), F32)) for _ in groups)
    carry = tile(qi, init, True)
    carry = lax.fori_loop(0, qi, lambda i, c: tile(qi - 1 - i, c, False), carry)
    for g, (acc, _) in zip(groups, carry):
        o_ref[:, g] = _unstack_pair(acc)


def _stick_breaking(qkv, nb, seq, q_col, k_col, v_col, n_pairs):
    tq = min(SB_TILE, seq)
    nq = seq // tq
    per = SB_PAIRS
    w = per * LANES
    assert n_pairs % per == 0 and q_col % per == 0 and k_col % per == 0 and v_col % per == 0
    return pl.pallas_call(
        _sb_kernel,
        grid=(nb, n_pairs // per, nq),
        in_specs=[pl.BlockSpec((tq, w), lambda b, p, i: (b * nq + i, q_col // per + p)),
                  pl.BlockSpec((seq, w), lambda b, p, i: (b, k_col // per + p)),
                  pl.BlockSpec((seq, w), lambda b, p, i: (b, v_col // per + p))],
        out_specs=pl.BlockSpec((tq, w), lambda b, p, i: (b * nq + i, p)),
        out_shape=jax.ShapeDtypeStruct((nb * seq, n_pairs * LANES), F32),
        compiler_params=_params("parallel", "parallel", "arbitrary"),
        name="stick_breaking",
    )(qkv, qkv, qkv)


def _dil_kernel(q_ref, k_ref, v_ref, o_ref, m_ref, l_ref, a_ref):
    seq = q_ref.shape[0]
    tq = Q_TILE
    first = _lane_iota((1, LANES)) < HEAD_DIM

    q_in = _row_iota((2 * tq, 1)) % tq

    def band_bias(wd, n_tiles):
        if n_tiles > 1:
            lag = q_in - (_lane_iota((1, 2 * tq)) - tq)
            return jnp.where((lag >= 0) & (lag <= wd), 0.0, NEG_BIG)
        return jnp.where(_lane_iota((1, tq)) <= q_in, 0.0, NEG_BIG)

    def tile(c, dil, band, n_tiles, r, i):
        base = r + dil * tq * i

        def rows(ref, start):
            if dil == 1:
                return ref[pl.ds(pl.multiple_of(start, tq), tq), :]
            return ref[pl.ds(start, tq, stride=dil), :]

        qs = _stack_pair(rows(q_ref, base)).astype(BF16)
        if n_tiles > 1:
            prev = r + dil * tq * jnp.maximum(i - 1, 0)
            kk = jnp.concatenate([rows(k_ref, prev), rows(k_ref, base)], 0)
            vv = jnp.concatenate([rows(v_ref, prev), rows(v_ref, base)], 0)
            no_prev = jnp.where((i == 0) & (_lane_iota((1, 2 * tq)) < tq), NEG_BIG, 0.0)
            bias = band + no_prev
        else:
            kk = rows(k_ref, base)
            vv = rows(v_ref, base)
            bias = band
        s = _dot_nt(qs, kk) + bias
        m = jnp.max(s, -1, keepdims=True)
        p = jnp.exp2(s - m)
        l = jnp.sum(p, -1, keepdims=True)
        acc = _dot(p, vv)
        m2 = jnp.where(first, m[:tq], m[tq:])
        l2 = jnp.where(first, l[:tq], l[tq:])
        a2 = jnp.where(first, acc[:tq], acc[tq:])
        if dil == 1:
            sl = pl.ds(pl.multiple_of(base, tq), tq)
        else:
            sl = pl.ds(base, tq, stride=dil)
        m_ref[c, sl, :] = m2
        l_ref[c, sl, :] = l2
        a_ref[c, sl, :] = a2

    for c, (window, dil) in enumerate(DIL_CONFIGS):
        wd = window // dil
        n_tiles = seq // dil // tq

        band = band_bias(wd, n_tiles)

        def per_tile(n, _, c=c, dil=dil, band=band, n_tiles=n_tiles):
            tile(c, dil, band, n_tiles, n // n_tiles, n % n_tiles)
            return 0

        lax.fori_loop(0, dil * n_tiles, per_tile, 0, unroll=16)

    def combine(i, _):
        sl = pl.ds(pl.multiple_of(i * tq, tq), tq)
        m0, m1, m2 = m_ref[0, sl, :], m_ref[1, sl, :], m_ref[2, sl, :]
        mx = jnp.maximum(jnp.maximum(m0, m1), m2)
        e0, e1, e2 = jnp.exp2(m0 - mx), jnp.exp2(m1 - mx), jnp.exp2(m2 - mx)
        num = e0 * a_ref[0, sl, :] + e1 * a_ref[1, sl, :] + e2 * a_ref[2, sl, :]
        den = e0 * l_ref[0, sl, :] + e1 * l_ref[1, sl, :] + e2 * l_ref[2, sl, :]
        o_ref[sl, :] = num / den
        return 0

    lax.fori_loop(0, seq // tq, combine, 0)


def _dilated(q_arr, q_col, k_arr, k_col, v_arr, v_col, nb, seq, n_pairs):
    return pl.pallas_call(
        _dil_kernel,
        grid=(nb, n_pairs),
        in_specs=[pl.BlockSpec((seq, LANES), lambda b, p: (b, q_col + p)),
                  pl.BlockSpec((seq, LANES), lambda b, p: (b, k_col + p)),
                  pl.BlockSpec((seq, LANES), lambda b, p: (b, v_col + p))],
        out_specs=pl.BlockSpec((seq, LANES), lambda b, p: (b, p)),
        out_shape=jax.ShapeDtypeStruct((nb * seq, n_pairs * LANES), F32),
        scratch_shapes=[pltpu.VMEM((3, seq, LANES), F32)] * 3,
        compiler_params=_params("parallel", "parallel"),
        name="dilated",
    )(q_arr, k_arr, v_arr)


def _cmp_kernel(t_ref, pe_ref, w1a_ref, w1b_ref, w2_ref, cos_ref, sin_ref, o_ref):
    nblk = t_ref.shape[0] // CMP_STRIDE
    j = pl.program_id(1)
    out = jnp.zeros((nblk, LANES), F32)
    for g in range(NSA_KV_HEADS):
        p1 = jnp.zeros((nblk, CMP_HIDDEN), F32)
        p2 = jnp.zeros((nblk, CMP_HIDDEN), F32)
        for l in range(CMP_STRIDE):
            x = t_ref[pl.ds(l, nblk, stride=CMP_STRIDE), :]
            p1 = p1 + _dot(x + pe_ref[0, l:l + 1, :], w1a_ref[0, g, l])
            p2 = p2 + _dot(x + pe_ref[0, CMP_STRIDE + l:CMP_STRIDE + l + 1, :], w1b_ref[0, g, l])
        hidden = p1 + pltpu.roll(p2, nblk - 1, axis=0)
        out = out + _dot(jax.nn.gelu(hidden), w2_ref[0, g])
    first = (_lane_iota((1, LANES)) % HEAD_DIM) < HALF
    partner = jnp.where(first, pltpu.roll(out, LANES - HALF, axis=1), pltpu.roll(out, HALF, axis=1))
    roped = out * cos_ref[...] + partner * sin_ref[...]
    o_ref[0] = jnp.where(j == 0, roped, out)


def _compress(p_arr, col0, nb, seq, pe, w1, w2, cos_c, sin_c):
    nblk = seq // CMP_STRIDE
    G = NSA_KV_HEADS
    pe2 = jnp.tile(pe, (1, 1, G))
    w1r = w1.reshape(2, CMP_LEN, HEAD_DIM, CMP_HIDDEN)
    w1e = jnp.zeros((2, G, CMP_LEN, LANES, CMP_HIDDEN), F32)
    w2e = jnp.zeros((2, G, CMP_HIDDEN, LANES), F32)
    for g in range(G):
        w1e = w1e.at[:, g, :, g * HEAD_DIM:(g + 1) * HEAD_DIM, :].set(w1r)
        w2e = w2e.at[:, g, :, g * HEAD_DIM:(g + 1) * HEAD_DIM].set(w2)
    w1e = w1e.astype(BF16)
    w2e = w2e.astype(BF16)
    return pl.pallas_call(
        _cmp_kernel,
        grid=(nb, 2),
        in_specs=[pl.BlockSpec((seq, LANES), lambda b, j: (b, col0 + j)),
                  pl.BlockSpec((1, CMP_LEN, LANES), lambda b, j: (j, 0, 0)),
                  pl.BlockSpec((1, G, CMP_STRIDE, LANES, CMP_HIDDEN), lambda b, j: (j, 0, 0, 0, 0)),
                  pl.BlockSpec((1, G, CMP_STRIDE, LANES, CMP_HIDDEN), lambda b, j: (j, 0, 1, 0, 0)),
                  pl.BlockSpec((1, G, CMP_HIDDEN, LANES), lambda b, j: (j, 0, 0, 0)),
                  pl.BlockSpec((nblk, LANES), lambda b, j: (0, 0)),
                  pl.BlockSpec((nblk, LANES), lambda b, j: (0, 0))],
        out_specs=pl.BlockSpec((1, nblk, LANES), lambda b, j: (j, b, 0)),
        out_shape=jax.ShapeDtypeStruct((2, nb * nblk, LANES), F32),
        compiler_params=_params("parallel", "parallel"),
        name="nsa_compress",
    )(p_arr, pe2, w1e, w1e, w2e, cos_c, sin_c)


def _nsa_kernel(q_ref, gate_ref, kc_ref, vc_ref, ks_ref, vs_ref, kw_ref, vw_ref, ovt_ref, o_ref):
    tq = q_ref.shape[0]
    R, G = NSA_REP, NSA_KV_HEADS
    seq = ks_ref.shape[0]
    n_sel = seq // SEL_BLOCK
    n_cmp = (seq - CMP_LEN) // CMP_STRIDE + 1
    nblk = kc_ref.shape[1]
    top_n = min(SEL_TOPN, n_sel)
    qi = pl.program_id(1)
    lane = _lane_iota((1, LANES))
    first = lane < HEAD_DIM
    mine = [first, jnp.logical_not(first)]
    t_row = qi * tq + _row_iota((tq, 1))

    qs = []
    for g in range(G):
        parts = []
        for r in range(R):
            h = g * R + r
            x = q_ref[:, (h // 2) * LANES:(h // 2 + 1) * LANES]
            parts.append(jnp.where(mine[g], x if h % 2 == g else _swap_halves(x), 0.0))
        qs.append(jnp.concatenate(parts, 0).astype(BF16))

    c_idx = _lane_iota((tq, nblk))
    visible = (c_idx < n_cmp) & (c_idx * CMP_STRIDE + (CMP_LEN - 1) <= qi * tq + _row_iota((tq, nblk)))
    c_bias = jnp.where(visible, 0.0, NEG_BIG)
    kc, vc = kc_ref[0], vc_ref[0]
    vc_swapped = _swap_halves(vc)
    s_cs = [_dot_nt(q, kc).reshape(R, tq, nblk) + c_bias for q in qs]
    p_cs, o_cs = [], []
    for g, s_c in enumerate(s_cs):
        m_c = jnp.max(s_c, -1, keepdims=True)
        m_c = jnp.where(m_c > 0.5 * NEG_BIG, m_c, 0.0)
        p_c = jnp.exp2(s_c - m_c)
        p_c = p_c * (1.0 / jnp.maximum(jnp.sum(p_c, -1, keepdims=True), 1e-30))
        p_cs.append(p_c)
        o_cs.append(_dot(p_c.reshape(R * tq, nblk), jnp.where(mine[g], vc, vc_swapped)).reshape(R, tq, LANES))

    n_idx = _row_iota((n_sel, tq))
    t_q = qi * tq + _lane_iota((n_sel, tq))
    cur = t_q // SEL_BLOCK
    future = n_idx * SEL_BLOCK > t_q
    forced = (n_idx == 0) | (n_idx == cur) | (n_idx == cur - 1)
    sels = []
    for p_c in p_cs:
        psum = jnp.sum(p_c, 0)
        p_hi = psum.astype(BF16)
        p_lo = (psum - p_hi.astype(F32)).astype(BF16)
        imp = (_dot_nt(ovt_ref[...], p_hi) + _dot_nt(ovt_ref[...], p_lo))[:n_sel]
        imp = jnp.where(forced, 1e9, jnp.where(future, -1.0, imp))
        rank = jnp.zeros((n_sel, tq), F32)
        for m in range(n_sel):
            row = imp[m:m + 1, :]
            earlier = (n_idx > m).astype(F32)
            rank = rank + jnp.where(row > imp, 1.0, jnp.where(row == imp, earlier, 0.0))
        sel_t = jnp.where((rank < top_n) & (imp > -0.5), 1.0, 0.0)
        sel_t = jnp.concatenate([sel_t, jnp.zeros((LANES - n_sel, tq), F32)], 0)
        sels.append(sel_t.T.astype(BF16))

    k_col = _lane_iota((tq, tq))
    blk_row = _row_iota((LANES, tq))
    blk_of_key = _lane_iota((LANES, tq)) // SEL_BLOCK

    def sel_biases(j, causal):
        expand = (blk_row == (j * (tq // SEL_BLOCK) + blk_of_key)).astype(BF16)
        out = []
        for sel in sels:
            chosen = jnp.dot(sel, expand, preferred_element_type=F32)
            bias = (1.0 - chosen) * NEG_BIG
            if causal:
                bias = jnp.where((j * tq + k_col) <= t_row, bias, NEG_BIG)
            out.append(bias)
        return out

    def win_bias(j):
        diff = t_row - (j * tq + k_col)
        return jnp.where((diff >= 0) & (diff < NSA_WINDOW), 0.0, NEG_BIG)

    def tile_of(ref, j):
        return ref[pl.ds(pl.multiple_of(j * tq, tq), tq), :]

    def scores(k_ref, j, biases):
        kt = tile_of(k_ref, j).astype(BF16)
        return [_dot_nt(q, kt).reshape(R, tq, tq) + b for q, b in zip(qs, biases)]

    def update(carry, s_groups, v_ref, j):
        v = tile_of(v_ref, j)
        out = []
        for g, ((m_run, acc), s) in enumerate(zip(carry, s_groups)):
            m_new = jnp.maximum(m_run, jnp.max(s, -1, keepdims=True))
            alpha = jnp.exp2(m_run - m_new)
            p = jnp.exp2(s - m_new)
            pv = _dot(p.reshape(R * tq, tq), jnp.where(mine[g], v, 1.0))
            out.append((m_new, alpha * acc + pv.reshape(R, tq, LANES)))
        return tuple(out)

    def normalise(carry):
        out = []
        for g, (_, acc) in enumerate(carry):
            swapped = _swap_halves(acc)
            out.append(jnp.where(mine[g], acc, swapped) / jnp.where(mine[g], swapped, acc))
        return out

    init = tuple((jnp.full((R, tq, 1), NEG_BIG, F32), jnp.zeros((R, tq, LANES), F32)) for _ in range(G))
    j_win = jnp.maximum(qi - NSA_WINDOW // tq, 0)

    def selected_only(j, carry):
        return update(carry, scores(ks_ref, j, sel_biases(j, False)), vs_ref, j)

    def selected_and_window(j, carry):
        b_w = win_bias(j)
        s_s, s_w = scores(ks_ref, j, sel_biases(j, True)), scores(kw_ref, j, [b_w] * G)
        return update(carry[0], s_s, vs_ref, j), update(carry[1], s_w, vw_ref, j)

    c_sel = lax.fori_loop(0, j_win, selected_only, init)
    c_sel, c_win = lax.fori_loop(j_win, qi + 1, selected_and_window, (c_sel, init))
    o_ss, o_ws = normalise(c_sel), normalise(c_win)

    gate = jax.nn.sigmoid(gate_ref[...])
    outs = []
    for g in range(G):
        for r in range(R):
            c0 = g * LANES + 3 * r
            outs.append(gate[:, c0:c0 + 1] * o_cs[g][r] + gate[:, c0 + 1:c0 + 2] * o_ss[g][r]
                        + gate[:, c0 + 2:c0 + 3] * o_ws[g][r])
    for pair in range(G * R // 2):
        o_ref[:, pair * LANES:(pair + 1) * LANES] = jnp.where(first, outs[2 * pair], outs[2 * pair + 1])


def _nsa(r_arr, p_arr, cmp_kv, overlap, nb, seq, cols):
    tq = min(NSA_TILE, seq)
    nq = seq // tq
    nblk = seq // CMP_STRIDE
    G = NSA_KV_HEADS
    qw = G * NSA_REP * HEAD_DIM
    gw = G * LANES
    assert (cols['q'] * LANES) % qw == 0 and (cols['gate'] * LANES) % gw == 0
    return pl.pallas_call(
        _nsa_kernel,
        grid=(nb, nq),
        in_specs=[pl.BlockSpec((tq, qw), lambda b, i: (b * nq + i, cols['q'] * LANES // qw)),
                  pl.BlockSpec((tq, gw), lambda b, i: (b * nq + i, cols['gate'] * LANES // gw)),
                  pl.BlockSpec((1, nblk, LANES), lambda b, i: (0, b, 0)),
                  pl.BlockSpec((1, nblk, LANES), lambda b, i: (1, b, 0)),
                  pl.BlockSpec((seq, LANES), lambda b, i: (b, cols['k_slc'])),
                  pl.BlockSpec((seq, LANES), lambda b, i: (b, cols['v_slc'])),
                  pl.BlockSpec((seq, LANES), lambda b, i: (b, cols['k_win'])),
                  pl.BlockSpec((seq, LANES), lambda b, i: (b, cols['v_win'])),
                  pl.BlockSpec((LANES, nblk), lambda b, i: (0, 0))],
        out_specs=pl.BlockSpec((tq, qw), lambda b, i: (b * nq + i, 0)),
        out_shape=jax.ShapeDtypeStruct((nb * seq, qw), F32),
        compiler_params=_params("parallel", "arbitrary"),
        name="nsa",
    )(r_arr, p_arr, cmp_kv, cmp_kv, r_arr, p_arr, r_arr, p_arr, overlap)


def _mix_out_kernel(ya_ref, yb_ref, yc_ref, yd_ref, ng_ref, w_ref, h_ref, g_ref, b_ref,
                    o_ref, ob_ref, *, alpha):
    fs = []
    for rows in _row_chunks(o_ref.shape[0]):
        ys = []
        for n, y_ref in enumerate((ya_ref, yb_ref, yc_ref, yd_ref)):
            y = y_ref[rows, :]
            y = y * lax.rsqrt(jnp.mean(y * y, -1, keepdims=True) + RMS_EPS) * ng_ref[n:n + 1, :]
            ys.append(y.astype(BF16))
        fs.append(jnp.dot(jnp.concatenate(ys, 1), w_ref[...], preferred_element_type=F32))
    for rows, f in zip(_row_chunks(o_ref.shape[0]), fs):
        out = _layer_norm(alpha * h_ref[rows, :] + f, g_ref[...], b_ref[...])
        o_ref[rows, :] = out
        ob_ref[rows, :] = out.astype(BF16)


def _mix_out(ya, yb, yc, yd, norm_g, w_out, h, ln_g, ln_b, alpha, tm):
    T, D = h.shape
    gw = yb.shape[1]
    tok = lambda i: (i, 0)
    const = lambda i: (0, 0)
    return pl.pallas_call(
        functools.partial(_mix_out_kernel, alpha=alpha),
        grid=(T // tm,),
        in_specs=[pl.BlockSpec((tm, gw), tok),
                  pl.BlockSpec((tm, gw), tok), pl.BlockSpec((tm, gw), tok), pl.BlockSpec((tm, gw), tok),
                  pl.BlockSpec((4, gw), const),
                  pl.BlockSpec((4 * gw, D), const),
                  pl.BlockSpec((tm, D), tok),
                  pl.BlockSpec((1, D), const), pl.BlockSpec((1, D), const)],
        out_specs=[pl.BlockSpec((tm, D), tok), pl.BlockSpec((tm, D), tok)],
        out_shape=[jax.ShapeDtypeStruct((T, D), F32), jax.ShapeDtypeStruct((T, D), BF16)],
        compiler_params=_params("parallel"),
        name="mix_out",
    )(ya, yb, yc, yd, norm_g, w_out, h, ln_g.reshape(1, D), ln_b.reshape(1, D))


def _xattn_kernel(hb_ref, h_ref, wq_ref, kv_ref, wo_ref, g_ref, b_ref, o_ref, ob_ref, *, alpha):
    width = wq_ref.shape[1]
    chunks = _row_chunks(o_ref.shape[0])
    qs = [jnp.dot(hb_ref[rows, :], wq_ref[...], preferred_element_type=F32) for rows in chunks]
    fs = []
    for q in qs:
        heads = []
        for hd in range(XA_HEADS):
            lo = hd * XA_HEAD_DIM
            k = kv_ref[:, lo:lo + XA_HEAD_DIM]
            v = kv_ref[:, width + lo:width + lo + XA_HEAD_DIM]
            s = _dot_nt(q[:, lo:lo + XA_HEAD_DIM], k) * (XA_HEAD_DIM ** -0.5)
            p = jnp.exp(s - jnp.max(s, -1, keepdims=True))
            p = p * (1.0 / jnp.sum(p, -1, keepdims=True))
            heads.append(_dot(p, v).astype(BF16))
        fs.append(jnp.dot(jnp.concatenate(heads, 1), wo_ref[...], preferred_element_type=F32))
    for rows, f in zip(chunks, fs):
        out = _layer_norm(alpha * h_ref[rows, :] + f, g_ref[...], b_ref[...])
        o_ref[rows, :] = out
        ob_ref[rows, :] = out.astype(BF16)


def _xattn(hb, h, wq, kv, wo, ln_g, ln_b, seq, mem_len, alpha, tm):
    T, D = h.shape
    width = wq.shape[1]
    ns = seq // tm
    tok = lambda i: (i, 0)
    const = lambda i: (0, 0)
    return pl.pallas_call(
        functools.partial(_xattn_kernel, alpha=alpha),
        grid=(T // tm,),
        in_specs=[pl.BlockSpec((tm, D), tok), pl.BlockSpec((tm, D), tok),
                  pl.BlockSpec((D, width), const),
                  pl.BlockSpec((mem_len, 2 * width), lambda i: (i // ns, 0)),
                  pl.BlockSpec((width, D), const),
                  pl.BlockSpec((1, D), const), pl.BlockSpec((1, D), const)],
        out_specs=[pl.BlockSpec((tm, D), tok), pl.BlockSpec((tm, D), tok)],
        out_shape=[jax.ShapeDtypeStruct((T, D), F32), jax.ShapeDtypeStruct((T, D), BF16)],
        compiler_params=_params("parallel"),
        name="cross_attn",
    )(hb, h, wq, kv, wo, ln_g.reshape(1, D), ln_b.reshape(1, D))


def _ffn_kernel(hb_ref, h_ref, wg_ref, wu_ref, wd_ref, g_ref, b_ref, o_ref, *rest, alpha):
    acc_ref = rest[-1]
    j = pl.program_id(1)

    @pl.when(j == 0)
    def _():
        acc_ref[...] = alpha * h_ref[...]

    x = hb_ref[...]
    gate = jnp.dot(x, wg_ref[...], preferred_element_type=F32)
    up = jnp.dot(x, wu_ref[...], preferred_element_type=F32)
    act = (jax.nn.silu(gate) * up).astype(BF16)
    acc_ref[...] += jnp.dot(act, wd_ref[...], preferred_element_type=F32)

    @pl.when(j == pl.num_programs(1) - 1)
    def _():
        out = _layer_norm(acc_ref[...], g_ref[...], b_ref[...])
        o_ref[...] = out
        if len(rest) == 2:
            rest[0][...] = out.astype(BF16)


def _ffn(hb, h, wg, wu, wd, ln_g, ln_b, alpha, tm, th, with_bf16_copy):
    T, D = h.shape
    H = wg.shape[1]
    tok = lambda i, j: (i, 0)
    const = lambda i, j: (0, 0)
    n_out = 2 if with_bf16_copy else 1
    return pl.pallas_call(
        functools.partial(_ffn_kernel, alpha=alpha),
        grid=(T // tm, H // th),
        in_specs=[pl.BlockSpec((tm, D), tok), pl.BlockSpec((tm, D), tok),
                  pl.BlockSpec((D, th), lambda i, j: (0, j)),
                  pl.BlockSpec((D, th), lambda i, j: (0, j)),
                  pl.BlockSpec((th, D), lambda i, j: (j, 0)),
                  pl.BlockSpec((1, D), const), pl.BlockSpec((1, D), const)],
        out_specs=[pl.BlockSpec((tm, D), tok), pl.BlockSpec((tm, D), tok)][:n_out],
        out_shape=[jax.ShapeDtypeStruct((T, D), F32), jax.ShapeDtypeStruct((T, D), BF16)][:n_out],
        scratch_shapes=[pltpu.VMEM((tm, D), F32)],
        compiler_params=_params("parallel", "arbitrary"),
        name="ffn",
    )(hb, h, wg, wu, wd, ln_g.reshape(1, D), ln_b.reshape(1, D))


def _rope_tables(pos):
    inv_freq = ROPE_THETA ** (-jnp.arange(HALF, dtype=F32) / HALF)
    ang = pos.astype(F32)[:, None] * inv_freq[None, :]
    cos = jnp.tile(jnp.cos(ang), (1, LANES // HALF))
    sin = jnp.sin(ang)
    sin = jnp.tile(jnp.concatenate([-sin, sin], 1), (1, LANES // HEAD_DIM))
    return cos, sin


def _overlap_table(seq):
    n_cmp = (seq - CMP_LEN) // CMP_STRIDE + 1
    n_sel = seq // SEL_BLOCK
    ci = np.arange(n_cmp)[:, None] * CMP_STRIDE
    sj = np.arange(n_sel)[None, :] * SEL_BLOCK
    ov = np.clip(np.minimum(ci + CMP_LEN, sj + SEL_BLOCK) - np.maximum(ci, sj), 0, None) / CMP_LEN
    full = np.zeros((LANES, seq // CMP_STRIDE), np.float32)
    full[:n_sel, :n_cmp] = ov.T
    return jnp.asarray(full, dtype=BF16)


def _split_w_in(w_in_stack, layer, width):
    hw = width // 4
    kvw = NSA_KV_HEADS * HEAD_DIM
    n_gate = 3 * (hw // HEAD_DIM)
    o = np.cumsum([0, hw, hw, 6 * kvw, n_gate, 3 * hw, 3 * hw])
    col_scale = np.ones((o[-1],), np.float32)
    for q_lo in (o[1], o[4], o[5]):
        col_scale[q_lo:q_lo + hw] = HEAD_DIM ** -0.5 * math.log2(math.e)
    w_in = _layer_bf16(w_in_stack, layer, jnp.asarray(col_scale))
    u = w_in[:, o[0]:o[1]]
    nq = w_in[:, o[1]:o[2]]
    kv = [w_in[:, o[2] + j * kvw:o[2] + (j + 1) * kvw] for j in range(6)]
    gate = w_in[:, o[3]:o[4]]
    sb = w_in[:, o[4]:o[5]]
    dil = w_in[:, o[5]:o[6]]
    per_group = 3 * NSA_REP
    gates = [jnp.pad(gate[:, g * per_group:(g + 1) * per_group], ((0, 0), (0, LANES - per_group)))
             for g in range(NSA_KV_HEADS)]
    w_rope = jnp.concatenate([nq, kv[2], kv[4], dil[:, :2 * hw]], 1)
    w_plain = jnp.concatenate([sb, dil[:, 2 * hw:], kv[0], kv[1], kv[3], kv[5]] + gates, 1)
    return u, w_rope, w_plain


R_NQ, R_KSLC, R_KWIN, R_DILQ, R_DILK = 0, 4, 5, 6, 10
P_SBQ, P_SBK, P_SBV, P_DILV, P_KCMP, P_VSLC, P_VWIN, P_GATE = 0, 4, 8, 12, 16, 18, 19, 20


def _hybrid_mixer(hb, h, nb, seq, w_in_stack, layer, s5_params, cmp_pe, cmp_w1, cmp_w2, norm_g, w_out,
                  ln_g, ln_b, tables, alpha, tm):
    cos, sin, cos_c, sin_c, overlap = tables
    width = w_out.shape[0]
    n_pairs = width // 4 // LANES
    w_u, w_rope, w_plain = _split_w_in(w_in_stack, layer, width)
    r_arr = _proj(hb, w_rope, min(2 * tm, seq), w_rope.shape[1], seq=seq, rope=(cos, sin))
    p_arr = _proj(hb, w_plain, tm, w_plain.shape[1])
    y_a = _s5(hb, w_u, nb, seq, *s5_params)
    cmp_kv = _compress(p_arr, P_KCMP, nb, seq, cmp_pe, cmp_w1, cmp_w2, cos_c, sin_c)
    y_b = _nsa(r_arr, p_arr, cmp_kv, overlap, nb, seq,
               dict(q=R_NQ, gate=P_GATE, k_slc=R_KSLC, v_slc=P_VSLC, k_win=R_KWIN, v_win=P_VWIN))
    y_c = _stick_breaking(p_arr, nb, seq, P_SBQ, P_SBK, P_SBV, n_pairs)
    y_d = _dilated(r_arr, R_DILQ, r_arr, R_DILK, p_arr, P_DILV, nb, seq, n_pairs)
    return _mix_out(y_a, y_b, y_c, y_d, norm_g, w_out, h, ln_g, ln_b, alpha, tm)


def kernel(x, mem, ln_in_g, ln_in_b, w_in, s5_lambda_re, s5_lambda_im, s5_log_dt, s5_b_re, s5_b_im, s5_c_re, s5_c_im, s5_d, s5_w_glu, s5_b_glu, nsa_cmp_pe, nsa_cmp_w1, nsa_cmp_w2, mix_norm_g, w_out, ln1_g, ln1_b, xa_wq, xa_wkv, xa_wo, ln2_g, ln2_b, ffn_w_gate, ffn_w_up, ffn_w_down, ln3_g, ln3_b):
    nb, seq, d_model = x.shape
    mem_len = mem.shape[1]
    depth = w_in.shape[0]
    alpha = (2 * depth) ** 0.25
    tm = min(512, seq)
    pos = jnp.arange(seq)
    cos, sin = _rope_tables(pos)
    nblk = seq // CMP_STRIDE
    cos_c, sin_c = _rope_tables(jnp.arange(nblk) * CMP_STRIDE + CMP_LEN - 1)
    tables = (cos, sin, cos_c, sin_c, _overlap_table(seq))
    mem_b = mem.reshape(nb * mem_len, d_model).astype(BF16)

    h, hb = _ln_in(x.reshape(nb * seq, d_model), ln_in_g, ln_in_b, tm)
    for l in range(depth):
        s5_params = (s5_lambda_re[l], s5_lambda_im[l], s5_log_dt[l], s5_b_re[l], s5_b_im[l],
                     s5_c_re[l], s5_c_im[l], s5_d[l], s5_w_glu[l], s5_b_glu[l])
        h, hb = _hybrid_mixer(hb, h, nb, seq, w_in, l, s5_params, nsa_cmp_pe[l], nsa_cmp_w1[l],
                              nsa_cmp_w2[l], mix_norm_g[l], _layer_bf16(w_out, l), ln1_g[l], ln1_b[l],
                              tables, alpha, tm)
        kv = _proj(mem_b, _layer_bf16(xa_wkv, l), min(512, nb * mem_len), xa_wkv.shape[2] // 2)
        h, hb = _xattn(hb, h, _layer_bf16(xa_wq, l), kv, _layer_bf16(xa_wo, l), ln2_g[l], ln2_b[l],
                       seq, mem_len, alpha, tm)
        last = l == depth - 1
        outs = _ffn(hb, h, _layer_bf16(ffn_w_gate, l), _layer_bf16(ffn_w_up, l),
                    _layer_bf16(ffn_w_down, l), ln3_g[l], ln3_b[l], alpha, tm, 512, not last)
        h, hb = (outs[0], None) if last else outs
    return h.reshape(nb, seq, d_model)
```

```python
import functools
import math

import numpy as np
import jax
import jax.numpy as jnp
from jax import lax
from jax.experimental import pallas as pl
from jax.experimental.pallas import tpu as pltpu

F32 = jnp.float32
BF16 = jnp.bfloat16

LANES = 128
MXU_WIDTH = 256
CAST_ROWS = 512
VMEM_LIMIT = 56 * 1024 * 1024

HEAD_DIM = 64
HALF = HEAD_DIM // 2
ROPE_THETA = 10000.0
LN_EPS = 1e-5
RMS_EPS = 1e-6
SSM_CH = 16
SSM_STATE = 64
NSA_KV_HEADS = 2
NSA_REP = 4
CMP_LEN = 32
CMP_STRIDE = 16
CMP_HIDDEN = 128
SEL_BLOCK = 64
SEL_TOPN = 8
NSA_WINDOW = 512
DIL_CONFIGS = ((128, 1), (512, 4), (2048, 16))
XA_HEADS = 4
XA_HEAD_DIM = 128
Q_TILE = 128
SB_TILE = 256
SB_PAIRS = 4
NSA_TILE = 256
NEG_BIG = -1e30


def _params(*sem):
    return pltpu.CompilerParams(dimension_semantics=sem, vmem_limit_bytes=VMEM_LIMIT)


def _dot(a, b):
    return jnp.dot(a.astype(BF16), b.astype(BF16), preferred_element_type=F32)


def _dot_nt(a, b):
    return lax.dot_general(a.astype(BF16), b.astype(BF16), (((1,), (1,)), ((), ())),
                           preferred_element_type=F32)


def _layer_norm(x, g, b):
    mu = jnp.mean(x, -1, keepdims=True)
    xc = x - mu
    var = jnp.mean(xc * xc, -1, keepdims=True)
    return xc * lax.rsqrt(var + LN_EPS) * g + b


def _lane_iota(shape):
    return lax.broadcasted_iota(jnp.int32, shape, len(shape) - 1)


def _row_iota(shape):
    return lax.broadcasted_iota(jnp.int32, shape, len(shape) - 2)


def _swap_halves(x):
    return pltpu.roll(x, HEAD_DIM, axis=x.ndim - 1)


def _ln_in_kernel(x_ref, g_ref, b_ref, h_ref, hb_ref):
    y = _layer_norm(x_ref[...], g_ref[...], b_ref[...])
    h_ref[...] = y
    hb_ref[...] = y.astype(BF16)


def _ln_in(x2, g, b, tm):
    T, D = x2.shape
    return pl.pallas_call(
        _ln_in_kernel,
        grid=(T // tm,),
        in_specs=[pl.BlockSpec((tm, D), lambda i: (i, 0)),
                  pl.BlockSpec((1, D), lambda i: (0, 0)),
                  pl.BlockSpec((1, D), lambda i: (0, 0))],
        out_specs=[pl.BlockSpec((tm, D), lambda i: (i, 0)),
                   pl.BlockSpec((tm, D), lambda i: (i, 0))],
        out_shape=[jax.ShapeDtypeStruct((T, D), F32), jax.ShapeDtypeStruct((T, D), BF16)],
        compiler_params=_params("parallel"),
        name="ln_in",
    )(x2, g.reshape(1, D), b.reshape(1, D))


def _cast_kernel(w_ref, o_ref):
    o_ref[...] = w_ref[...].astype(BF16)


def _cast_scaled_kernel(w_ref, s_ref, o_ref):
    o_ref[...] = (w_ref[...] * s_ref[...]).astype(BF16)


def _layer_bf16(w_stack, layer, col_scale=None):
    _, K, N = w_stack.shape
    tk = min(CAST_ROWS, K)
    in_specs = [pl.BlockSpec((None, tk, N), lambda i: (layer, i, 0))]
    args = [w_stack]
    kern = _cast_kernel
    if col_scale is not None:
        in_specs.append(pl.BlockSpec((1, N), lambda i: (0, 0)))
        args.append(col_scale.reshape(1, N))
        kern = _cast_scaled_kernel
    return pl.pallas_call(
        kern, grid=(K // tk,), in_specs=in_specs,
        out_specs=pl.BlockSpec((tk, N), lambda i: (i, 0)),
        out_shape=jax.ShapeDtypeStruct((K, N), BF16),
        compiler_params=_params("parallel"), name="cast_bf16",
    )(*args)


def _row_chunks(n, parts=2):
    step = n // parts
    return [slice(i * step, (i + 1) * step) for i in range(parts)]


def _column_chunks(n):
    return [(lo, min(lo + MXU_WIDTH, n)) for lo in range(0, n, MXU_WIDTH)]


def _proj_kernel(a_ref, w_ref, o_ref):
    a = a_ref[...]
    for lo, hi in _column_chunks(o_ref.shape[1]):
        o_ref[:, lo:hi] = jnp.dot(a, w_ref[:, lo:hi], preferred_element_type=F32)


def _proj_rope_kernel(a_ref, w_ref, cos_ref, sin_ref, o_ref):
    a = a_ref[...]
    cos = cos_ref[...]
    sin = sin_ref[...]
    first = (_lane_iota((1, LANES)) % HEAD_DIM) < HALF
    for lo, hi in _column_chunks(o_ref.shape[1]):
        acc = jnp.dot(a, w_ref[:, lo:hi], preferred_element_type=F32)
        for c in range((hi - lo) // LANES):
            x = acc[:, c * LANES:(c + 1) * LANES]
            partner = jnp.where(first, pltpu.roll(x, LANES - HALF, axis=1), pltpu.roll(x, HALF, axis=1))
            o_ref[:, lo + c * LANES:lo + (c + 1) * LANES] = x * cos + partner * sin


def _proj(a, w, tm, tn, seq=None, rope=None):
    M, K = a.shape
    N = w.shape[1]
    nm, nn = M // tm, N // tn
    in_specs = [pl.BlockSpec((tm, K), lambda j, i: (i, 0)),
                pl.BlockSpec((K, tn), lambda j, i: (0, j))]
    args = [a, w]
    kern = _proj_kernel
    if rope is not None:
        ns = seq // tm
        in_specs += [pl.BlockSpec((tm, LANES), lambda j, i: (i % ns, 0))] * 2
        args += list(rope)
        kern = _proj_rope_kernel
    return pl.pallas_call(
        kern, grid=(nn, nm), in_specs=in_specs,
        out_specs=pl.BlockSpec((tm, tn), lambda j, i: (i, j)),
        out_shape=jax.ShapeDtypeStruct((M, N), F32),
        compiler_params=_params("parallel", "parallel"), name="proj",
    )(*args)


def _s5_kernel(h_ref, wu_ref, bb_ref, cc_ref, are_ref, aim_ref, d_ref, wg_ref, bg_ref, y_ref,
               buf_ref, st_ref, tm_ref, *, nb, ts):
    n_chunks, cw, sw2 = bb_ref.shape
    sw = sw2 // 2
    assert cw == tm_ref.shape[2]

    @pl.when(pl.program_id(0) == 0)
    def _():
        st_ref[...] = jnp.zeros_like(st_ref)

    u_bt = jnp.dot(h_ref[...].reshape(nb * ts, h_ref.shape[2]), wu_ref[...],
                   preferred_element_type=F32)
    for b in range(nb):
        for c in range(n_chunks):
            tm_ref[c, pl.ds(b, ts, stride=nb), :] = u_bt[b * ts:(b + 1) * ts, c * cw:(c + 1) * cw]
    u = jnp.concatenate([tm_ref[c] for c in range(n_chunks)], 1)
    ys = []
    for c in range(n_chunks):
        re0, im0 = c * sw2, c * sw2 + sw
        buf_ref[:, re0:re0 + sw2] = jnp.dot(u[:, c * cw:(c + 1) * cw].astype(BF16), bb_ref[c],
                                            preferred_element_type=F32)
        a_re = jnp.broadcast_to(are_ref[:, c * sw:(c + 1) * sw], (nb, sw))
        a_im = jnp.broadcast_to(aim_ref[:, c * sw:(c + 1) * sw], (nb, sw))

        def step(t, carry, re0=re0, im0=im0, a_re=a_re, a_im=a_im):
            x_re, x_im = carry
            r = pl.multiple_of(t * nb, nb)
            n_re = a_re * x_re - a_im * x_im + buf_ref[pl.ds(r, nb), re0:re0 + sw]
            n_im = a_re * x_im + a_im * x_re + buf_ref[pl.ds(r, nb), im0:im0 + sw]
            buf_ref[pl.ds(r, nb), re0:re0 + sw] = n_re
            buf_ref[pl.ds(r, nb), im0:im0 + sw] = n_im
            return n_re, n_im

        x_re, x_im = lax.fori_loop(0, ts, step, (st_ref[:, re0:re0 + sw], st_ref[:, im0:im0 + sw]),
                                   unroll=True)
        st_ref[:, re0:re0 + sw] = x_re
        st_ref[:, im0:im0 + sw] = x_im
        ys.append(jnp.dot(buf_ref[:, re0:re0 + sw2].astype(BF16), cc_ref[c], preferred_element_type=F32))

    y = jnp.concatenate(ys, 1) + d_ref[...] * u
    g = jax.nn.gelu(y)
    z = jnp.dot(g.astype(BF16), wg_ref[...], preferred_element_type=F32) + bg_ref[...]
    out = g * jax.nn.sigmoid(z)
    for c in range(n_chunks):
        tm_ref[c] = out[:, c * cw:(c + 1) * cw]
    for b in range(nb):
        for c in range(n_chunks):
            y_ref[b, :, c * cw:(c + 1) * cw] = tm_ref[c, pl.ds(b, ts, stride=nb), :]


def _s5(hb, w_u, nb, seq, lam_re, lam_im, log_dt, b_re, b_im, c_re, c_im, d_skip, w_glu, b_glu, ts=64):
    D, W = w_u.shape
    G, P = lam_re.shape
    C = SSM_CH
    lr = jnp.minimum(lam_re, -1e-4)
    li = lam_im
    dt = jnp.exp(log_dt)[:, None]
    mag = jnp.exp(lr * dt)
    a_re = mag * jnp.cos(li * dt)
    a_im = mag * jnp.sin(li * dt)
    den = lr * lr + li * li
    z_re = ((a_re - 1.0) * lr + a_im * li) / den
    z_im = (a_im * lr - (a_re - 1.0) * li) / den
    bb_re = z_re[..., None] * b_re - z_im[..., None] * b_im
    bb_im = z_re[..., None] * b_im + z_im[..., None] * b_re
    gc = LANES // C
    nc = G // gc
    eye = jnp.eye(gc, dtype=F32)

    def block_diag_in(m):
        return jnp.einsum('ngpc,gh->ngchp', m.reshape(nc, gc, P, C), eye).reshape(nc, gc * C, gc * P)

    def block_diag_out(m):
        return jnp.einsum('ngcp,gh->ngphc', m.reshape(nc, gc, C, P), eye).reshape(nc, gc * P, gc * C)

    bb = jnp.concatenate([block_diag_in(bb_re), block_diag_in(bb_im)], 2).astype(BF16)
    cc = jnp.concatenate([block_diag_out(c_re), -block_diag_out(c_im)], 1).astype(BF16)
    ns = G * P
    rows = ts * nb
    kern = functools.partial(_s5_kernel, nb=nb, ts=ts)
    const = lambda i: (0, 0)
    const3 = lambda i: (0, 0, 0)
    y = pl.pallas_call(
        kern,
        grid=(seq // ts,),
        in_specs=[pl.BlockSpec((nb, ts, D), lambda i: (0, i, 0)),
                  pl.BlockSpec((D, W), const),
                  pl.BlockSpec(bb.shape, const3),
                  pl.BlockSpec(cc.shape, const3),
                  pl.BlockSpec((1, ns), const),
                  pl.BlockSpec((1, ns), const),
                  pl.BlockSpec((1, W), const),
                  pl.BlockSpec((W, W), const),
                  pl.BlockSpec((1, W), const)],
        out_specs=pl.BlockSpec((nb, ts, W), lambda i: (0, i, 0)),
        out_shape=jax.ShapeDtypeStruct((nb, seq, W), F32),
        scratch_shapes=[pltpu.VMEM((rows, 2 * ns), F32), pltpu.VMEM((nb, 2 * ns), F32),
                        pltpu.VMEM((nc, rows, LANES), F32)],
        compiler_params=_params("arbitrary"),
        name="s5",
    )(hb.reshape(nb, seq, D), w_u, bb, cc, a_re.reshape(1, ns), a_im.reshape(1, ns),
      d_skip.reshape(1, W), w_glu.astype(BF16), b_glu.reshape(1, W))
    return y.reshape(nb * seq, W)


def _stack_pair(q):
    first = _lane_iota((1, LANES)) < HEAD_DIM
    return jnp.concatenate([jnp.where(first, q, 0.0), jnp.where(first, 0.0, q)], 0)


def _unstack_pair(x):
    t = x.shape[0] // 2
    first = _lane_iota((1, LANES)) < HEAD_DIM
    return jnp.where(first, x[:t], x[t:])


def _sb_kernel(q_ref, k_ref, v_ref, o_ref):
    tq = q_ref.shape[0]
    qi = pl.program_id(2)
    n_pairs = q_ref.shape[1] // LANES
    groups = [slice(p * LANES, (p + 1) * LANES) for p in range(n_pairs)]
    qs = [_stack_pair(q_ref[:, g]).astype(BF16) for g in groups]
    tri = (_row_iota((tq, tq)) > _lane_iota((tq, tq))).astype(BF16)
    diag = _lane_iota((2 * tq, tq)) < (_row_iota((2 * tq, tq)) % tq)

    def tile(kj, carry, masked):
        r = pl.multiple_of(kj * tq, tq)
        zs = [_dot_nt(q, k_ref[pl.ds(r, tq), g]) for q, g in zip(qs, groups)]
        stage = []
        for z in zs:
            pos = jnp.maximum(z, 0.0)
            neg = z - pos
            log_term = jnp.log2(1.0 + jnp.exp2(neg - pos))
            sp = pos + log_term
            if masked:
                sp = jnp.where(diag, sp, 0.0)
            stage.append((neg - log_term, jnp.sum(sp, -1, keepdims=True), _dot(sp, tri)))
        out = []
        for (log_beta, row_sum, after), g, (acc, tail) in zip(stage, groups, carry):
            w = jnp.exp2(log_beta - (after + tail))
            if masked:
                w = jnp.where(diag, w, 0.0)
            out.append((acc + _dot(w, v_ref[pl.ds(r, tq), g]), tail + row_sum))
        return tuple(out)

    init = tuple((jnp.zeros((2 * tq, LANES), F32), jnp.zeros((2 * tq, 1), F32)) for _ in groups)
    carry = tile(qi, init, True)
    carry = lax.fori_loop(0, qi, lambda i, c: tile(qi - 1 - i, c, False), carry)
    for g, (acc, _) in zip(groups, carry):
        o_ref[:, g] = _unstack_pair(acc)


def _stick_breaking(qkv, nb, seq, q_col, k_col, v_col, n_pairs):
    tq = min(SB_TILE, seq)
    nq = seq // tq
    per = SB_PAIRS
    w = per * LANES
    assert n_pairs % per == 0 and q_col % per == 0 and k_col % per == 0 and v_col % per == 0
    return pl.pallas_call(
        _sb_kernel,
        grid=(nb, n_pairs // per, nq),
        in_specs=[pl.BlockSpec((tq, w), lambda b, p, i: (b * nq + i, q_col // per + p)),
                  pl.BlockSpec((seq, w), lambda b, p, i: (b, k_col // per + p)),
                  pl.BlockSpec((seq, w), lambda b, p, i: (b, v_col // per + p))],
        out_specs=pl.BlockSpec((tq, w), lambda b, p, i: (b * nq + i, p)),
        out_shape=jax.ShapeDtypeStruct((nb * seq, n_pairs * LANES), F32),
        compiler_params=_params("parallel", "parallel", "arbitrary"),
        name="stick_breaking",
    )(qkv, qkv, qkv)


def _dil_kernel(q_ref, k_ref, v_ref, o_ref, m_ref, l_ref, a_ref):
    seq = q_ref.shape[0]
    tq = Q_TILE
    first = _lane_iota((1, LANES)) < HEAD_DIM

    q_in = _row_iota((2 * tq, 1)) % tq

    def band_bias(wd, n_tiles):
        if n_tiles > 1:
            lag = q_in - (_lane_iota((1, 2 * tq)) - tq)
            return jnp.where((lag >= 0) & (lag <= wd), 0.0, NEG_BIG)
        return jnp.where(_lane_iota((1, tq)) <= q_in, 0.0, NEG_BIG)

    def tile(c, dil, band, n_tiles, r, i):
        base = r + dil * tq * i

        def rows(ref, start):
            if dil == 1:
                return ref[pl.ds(pl.multiple_of(start, tq), tq), :]
            return ref[pl.ds(start, tq, stride=dil), :]

        qs = _stack_pair(rows(q_ref, base)).astype(BF16)
        if n_tiles > 1:
            prev = r + dil * tq * jnp.maximum(i - 1, 0)
            kk = jnp.concatenate([rows(k_ref, prev), rows(k_ref, base)], 0)
            vv = jnp.concatenate([rows(v_ref, prev), rows(v_ref, base)], 0)
            no_prev = jnp.where((i == 0) & (_lane_iota((1, 2 * tq)) < tq), NEG_BIG, 0.0)
            bias = band + no_prev
        else:
            kk = rows(k_ref, base)
            vv = rows(v_ref, base)
            bias = band
        s = _dot_nt(qs, kk) + bias
        m = jnp.max(s, -1, keepdims=True)
        p = jnp.exp2(s - m)
        l = jnp.sum(p, -1, keepdims=True)
        acc = _dot(p, vv)
        m2 = jnp.where(first, m[:tq], m[tq:])
        l2 = jnp.where(first, l[:tq], l[tq:])
        a2 = jnp.where(first, acc[:tq], acc[tq:])
        if dil == 1:
            sl = pl.ds(pl.multiple_of(base, tq), tq)
        else:
            sl = pl.ds(base, tq, stride=dil)
        m_ref[c, sl, :] = m2
        l_ref[c, sl, :] = l2
        a_ref[c, sl, :] = a2

    for c, (window, dil) in enumerate(DIL_CONFIGS):
        wd = window // dil
        n_tiles = seq // dil // tq

        band = band_bias(wd, n_tiles)

        def per_tile(n, _, c=c, dil=dil, band=band, n_tiles=n_tiles):
            tile(c, dil, band, n_tiles, n // n_tiles, n % n_tiles)
            return 0

        lax.fori_loop(0, dil * n_tiles, per_tile, 0, unroll=16)

    def combine(i, _):
        sl = pl.ds(pl.multiple_of(i * tq, tq), tq)
        m0, m1, m2 = m_ref[0, sl, :], m_ref[1, sl, :], m_ref[2, sl, :]
        mx = jnp.maximum(jnp.maximum(m0, m1), m2)
        e0, e1, e2 = jnp.exp2(m0 - mx), jnp.exp2(m1 - mx), jnp.exp2(m2 - mx)
        num = e0 * a_ref[0, sl, :] + e1 * a_ref[1, sl, :] + e2 * a_ref[2, sl, :]
        den = e0 * l_ref[0, sl, :] + e1 * l_ref[1, sl, :] + e2 * l_ref[2, sl, :]
        o_ref[sl, :] = num / den
        return 0

    lax.fori_loop(0, seq // tq, combine, 0)


def _dilated(q_arr, q_col, k_arr, k_col, v_arr, v_col, nb, seq, n_pairs):
    return pl.pallas_call(
        _dil_kernel,
        grid=(nb, n_pairs),
        in_specs=[pl.BlockSpec((seq, LANES), lambda b, p: (b, q_col + p)),
                  pl.BlockSpec((seq, LANES), lambda b, p: (b, k_col + p)),
                  pl.BlockSpec((seq, LANES), lambda b, p: (b, v_col + p))],
        out_specs=pl.BlockSpec((seq, LANES), lambda b, p: (b, p)),
        out_shape=jax.ShapeDtypeStruct((nb * seq, n_pairs * LANES), F32),
        scratch_shapes=[pltpu.VMEM((3, seq, LANES), F32)] * 3,
        compiler_params=_params("parallel", "parallel"),
        name="dilated",
    )(q_arr, k_arr, v_arr)


def _cmp_kernel(t_ref, pe_ref, w1a_ref, w1b_ref, w2_ref, cos_ref, sin_ref, o_ref):
    nblk = t_ref.shape[0] // CMP_STRIDE
    j = pl.program_id(1)
    out = jnp.zeros((nblk, LANES), F32)
    for g in range(NSA_KV_HEADS):
        p1 = jnp.zeros((nblk, CMP_HIDDEN), F32)
        p2 = jnp.zeros((nblk, CMP_HIDDEN), F32)
        for l in range(CMP_STRIDE):
            x = t_ref[pl.ds(l, nblk, stride=CMP_STRIDE), :]
            p1 = p1 + _dot(x + pe_ref[0, l:l + 1, :], w1a_ref[0, g, l])
            p2 = p2 + _dot(x + pe_ref[0, CMP_STRIDE + l:CMP_STRIDE + l + 1, :], w1b_ref[0, g, l])
        hidden = p1 + pltpu.roll(p2, nblk - 1, axis=0)
        out = out + _dot(jax.nn.gelu(hidden), w2_ref[0, g])
    first = (_lane_iota((1, LANES)) % HEAD_DIM) < HALF
    partner = jnp.where(first, pltpu.roll(out, LANES - HALF, axis=1), pltpu.roll(out, HALF, axis=1))
    roped = out * cos_ref[...] + partner * sin_ref[...]
    o_ref[0] = jnp.where(j == 0, roped, out)


def _compress(p_arr, col0, nb, seq, pe, w1, w2, cos_c, sin_c):
    nblk = seq // CMP_STRIDE
    G = NSA_KV_HEADS
    pe2 = jnp.tile(pe, (1, 1, G))
    w1r = w1.reshape(2, CMP_LEN, HEAD_DIM, CMP_HIDDEN)
    w1e = jnp.zeros((2, G, CMP_LEN, LANES, CMP_HIDDEN), F32)
    w2e = jnp.zeros((2, G, CMP_HIDDEN, LANES), F32)
    for g in range(G):
        w1e = w1e.at[:, g, :, g * HEAD_DIM:(g + 1) * HEAD_DIM, :].set(w1r)
        w2e = w2e.at[:, g, :, g * HEAD_DIM:(g + 1) * HEAD_DIM].set(w2)
    w1e = w1e.astype(BF16)
    w2e = w2e.astype(BF16)
    return pl.pallas_call(
        _cmp_kernel,
        grid=(nb, 2),
        in_specs=[pl.BlockSpec((seq, LANES), lambda b, j: (b, col0 + j)),
                  pl.BlockSpec((1, CMP_LEN, LANES), lambda b, j: (j, 0, 0)),
                  pl.BlockSpec((1, G, CMP_STRIDE, LANES, CMP_HIDDEN), lambda b, j: (j, 0, 0, 0, 0)),
                  pl.BlockSpec((1, G, CMP_STRIDE, LANES, CMP_HIDDEN), lambda b, j: (j, 0, 1, 0, 0)),
                  pl.BlockSpec((1, G, CMP_HIDDEN, LANES), lambda b, j: (j, 0, 0, 0)),
                  pl.BlockSpec((nblk, LANES), lambda b, j: (0, 0)),
                  pl.BlockSpec((nblk, LANES), lambda b, j: (0, 0))],
        out_specs=pl.BlockSpec((1, nblk, LANES), lambda b, j: (j, b, 0)),
        out_shape=jax.ShapeDtypeStruct((2, nb * nblk, LANES), F32),
        compiler_params=_params("parallel", "parallel"),
        name="nsa_compress",
    )(p_arr, pe2, w1e, w1e, w2e, cos_c, sin_c)


def _nsa_kernel(q_ref, gate_ref, kc_ref, vc_ref, ks_ref, vs_ref, kw_ref, vw_ref, ovt_ref, o_ref):
    tq = q_ref.shape[0]
    R, G = NSA_REP, NSA_KV_HEADS
    seq = ks_ref.shape[0]
    n_sel = seq // SEL_BLOCK
    n_cmp = (seq - CMP_LEN) // CMP_STRIDE + 1
    nblk = kc_ref.shape[1]
    top_n = min(SEL_TOPN, n_sel)
    qi = pl.program_id(1)
    lane = _lane_iota((1, LANES))
    first = lane < HEAD_DIM
    mine = [first, jnp.logical_not(first)]
    t_row = qi * tq + _row_iota((tq, 1))

    qs = []
    for g in range(G):
        parts = []
        for r in range(R):
            h = g * R + r
            x = q_ref[:, (h // 2) * LANES:(h // 2 + 1) * LANES]
            parts.append(jnp.where(mine[g], x if h % 2 == g else _swap_halves(x), 0.0))
        qs.append(jnp.concatenate(parts, 0).astype(BF16))

    c_idx = _lane_iota((tq, nblk))
    visible = (c_idx < n_cmp) & (c_idx * CMP_STRIDE + (CMP_LEN - 1) <= qi * tq + _row_iota((tq, nblk)))
    c_bias = jnp.where(visible, 0.0, NEG_BIG)
    kc, vc = kc_ref[0], vc_ref[0]
    vc_swapped = _swap_halves(vc)
    s_cs = [_dot_nt(q, kc).reshape(R, tq, nblk) + c_bias for q in qs]
    p_cs, o_cs = [], []
    for g, s_c in enumerate(s_cs):
        m_c = jnp.max(s_c, -1, keepdims=True)
        m_c = jnp.where(m_c > 0.5 * NEG_BIG, m_c, 0.0)
        p_c = jnp.exp2(s_c - m_c)
        p_c = p_c * (1.0 / jnp.maximum(jnp.sum(p_c, -1, keepdims=True), 1e-30))
        p_cs.append(p_c)
        o_cs.append(_dot(p_c.reshape(R * tq, nblk), jnp.where(mine[g], vc, vc_swapped)).reshape(R, tq, LANES))

    n_idx = _row_iota((n_sel, tq))
    t_q = qi * tq + _lane_iota((n_sel, tq))
    cur = t_q // SEL_BLOCK
    future = n_idx * SEL_BLOCK > t_q
    forced = (n_idx == 0) | (n_idx == cur) | (n_idx == cur - 1)
    sels = []
    for p_c in p_cs:
        psum = jnp.sum(p_c, 0)
        p_hi = psum.astype(BF16)
        p_lo = (psum - p_hi.astype(F32)).astype(BF16)
        imp = (_dot_nt(ovt_ref[...], p_hi) + _dot_nt(ovt_ref[...], p_lo))[:n_sel]
        imp = jnp.where(forced, 1e9, jnp.where(future, -1.0, imp))
        rank = jnp.zeros((n_sel, tq), F32)
        for m in range(n_sel):
            row = imp[m:m + 1, :]
            earlier = (n_idx > m).astype(F32)
            rank = rank + jnp.where(row > imp, 1.0, jnp.where(row == imp, earlier, 0.0))
        sel_t = jnp.where((rank < top_n) & (imp > -0.5), 1.0, 0.0)
        sel_t = jnp.concatenate([sel_t, jnp.zeros((LANES - n_sel, tq), F32)], 0)
        sels.append(sel_t.T.astype(BF16))

    k_col = _lane_iota((tq, tq))
    blk_row = _row_iota((LANES, tq))
    blk_of_key = _lane_iota((LANES, tq)) // SEL_BLOCK

    def sel_biases(j, causal):
        expand = (blk_row == (j * (tq // SEL_BLOCK) + blk_of_key)).astype(BF16)
        out = []
        for sel in sels:
            chosen = jnp.dot(sel, expand, preferred_element_type=F32)
            bias = (1.0 - chosen) * NEG_BIG
            if causal:
                bias = jnp.where((j * tq + k_col) <= t_row, bias, NEG_BIG)
            out.append(bias)
        return out

    def win_bias(j):
        diff = t_row - (j * tq + k_col)
        return jnp.where((diff >= 0) & (diff < NSA_WINDOW), 0.0, NEG_BIG)

    def tile_of(ref, j):
        return ref[pl.ds(pl.multiple_of(j * tq, tq), tq), :]

    def scores(k_ref, j, biases):
        kt = tile_of(k_ref, j).astype(BF16)
        return [_dot_nt(q, kt).reshape(R, tq, tq) + b for q, b in zip(qs, biases)]

    def update(carry, s_groups, v_ref, j):
        v = tile_of(v_ref, j)
        out = []
        for g, ((m_run, acc), s) in enumerate(zip(carry, s_groups)):
            m_new = jnp.maximum(m_run, jnp.max(s, -1, keepdims=True))
            alpha = jnp.exp2(m_run - m_new)
            p = jnp.exp2(s - m_new)
            pv = _dot(p.reshape(R * tq, tq), jnp.where(mine[g], v, 1.0))
            out.append((m_new, alpha * acc + pv.reshape(R, tq, LANES)))
        return tuple(out)

    def normalise(carry):
        out = []
        for g, (_, acc) in enumerate(carry):
            swapped = _swap_halves(acc)
            out.append(jnp.where(mine[g], acc, swapped) / jnp.where(mine[g], swapped, acc))
        return out

    init = tuple((jnp.full((R, tq, 1), NEG_BIG, F32), jnp.zeros((R, tq, LANES), F32)) for _ in range(G))
    j_win = jnp.maximum(qi - NSA_WINDOW // tq, 0)

    def selected_only(j, carry):
        return update(carry, scores(ks_ref, j, sel_biases(j, False)), vs_ref, j)

    def selected_and_window(j, carry):
        b_w = win_bias(j)
        s_s, s_w = scores(ks_ref, j, sel_biases(j, True)), scores(kw_ref, j, [b_w] * G)
        return update(carry[0], s_s, vs_ref, j), update(carry[1], s_w, vw_ref, j)

    c_sel = lax.fori_loop(0, j_win, selected_only, init)
    c_sel, c_win = lax.fori_loop(j_win, qi + 1, selected_and_window, (c_sel, init))
    o_ss, o_ws = normalise(c_sel), normalise(c_win)

    gate = jax.nn.sigmoid(gate_ref[...])
    outs = []
    for g in range(G):
        for r in range(R):
            c0 = g * LANES + 3 * r
            outs.append(gate[:, c0:c0 + 1] * o_cs[g][r] + gate[:, c0 + 1:c0 + 2] * o_ss[g][r]
                        + gate[:, c0 + 2:c0 + 3] * o_ws[g][r])
    for pair in range(G * R // 2):
        o_ref[:, pair * LANES:(pair + 1) * LANES] = jnp.where(first, outs[2 * pair], outs[2 * pair + 1])


def _nsa(r_arr, p_arr, cmp_kv, overlap, nb, seq, cols):
    tq = min(NSA_TILE, seq)
    nq = seq // tq
    nblk = seq // CMP_STRIDE
    G = NSA_KV_HEADS
    qw = G * NSA_REP * HEAD_DIM
    gw = G * LANES
    assert (cols['q'] * LANES) % qw == 0 and (cols['gate'] * LANES) % gw == 0
    return pl.pallas_call(
        _nsa_kernel,
        grid=(nb, nq),
        in_specs=[pl.BlockSpec((tq, qw), lambda b, i: (b * nq + i, cols['q'] * LANES // qw)),
                  pl.BlockSpec((tq, gw), lambda b, i: (b * nq + i, cols['gate'] * LANES // gw)),
                  pl.BlockSpec((1, nblk, LANES), lambda b, i: (0, b, 0)),
                  pl.BlockSpec((1, nblk, LANES), lambda b, i: (1, b, 0)),
                  pl.BlockSpec((seq, LANES), lambda b, i: (b, cols['k_slc'])),
                  pl.BlockSpec((seq, LANES), lambda b, i: (b, cols['v_slc'])),
                  pl.BlockSpec((seq, LANES), lambda b, i: (b, cols['k_win'])),
                  pl.BlockSpec((seq, LANES), lambda b, i: (b, cols['v_win'])),
                  pl.BlockSpec((LANES, nblk), lambda b, i: (0, 0))],
        out_specs=pl.BlockSpec((tq, qw), lambda b, i: (b * nq + i, 0)),
        out_shape=jax.ShapeDtypeStruct((nb * seq, qw), F32),
        compiler_params=_params("parallel", "arbitrary"),
        name="nsa",
    )(r_arr, p_arr, cmp_kv, cmp_kv, r_arr, p_arr, r_arr, p_arr, overlap)


def _mix_out_kernel(ya_ref, yb_ref, yc_ref, yd_ref, ng_ref, w_ref, h_ref, g_ref, b_ref,
                    o_ref, ob_ref, *, alpha):
    fs = []
    for rows in _row_chunks(o_ref.shape[0]):
        ys = []
        for n, y_ref in enumerate((ya_ref, yb_ref, yc_ref, yd_ref)):
            y = y_ref[rows, :]
            y = y * lax.rsqrt(jnp.mean(y * y, -1, keepdims=True) + RMS_EPS) * ng_ref[n:n + 1, :]
            ys.append(y.astype(BF16))
        fs.append(jnp.dot(jnp.concatenate(ys, 1), w_ref[...], preferred_element_type=F32))
    for rows, f in zip(_row_chunks(o_ref.shape[0]), fs):
        out = _layer_norm(alpha * h_ref[rows, :] + f, g_ref[...], b_ref[...])
        o_ref[rows, :] = out
        ob_ref[rows, :] = out.astype(BF16)


def _mix_out(ya, yb, yc, yd, norm_g, w_out, h, ln_g, ln_b, alpha, tm):
    T, D = h.shape
    gw = yb.shape[1]
    tok = lambda i: (i, 0)
    const = lambda i: (0, 0)
    return pl.pallas_call(
        functools.partial(_mix_out_kernel, alpha=alpha),
        grid=(T // tm,),
        in_specs=[pl.BlockSpec((tm, gw), tok),
                  pl.BlockSpec((tm, gw), tok), pl.BlockSpec((tm, gw), tok), pl.BlockSpec((tm, gw), tok),
                  pl.BlockSpec((4, gw), const),
                  pl.BlockSpec((4 * gw, D), const),
                  pl.BlockSpec((tm, D), tok),
                  pl.BlockSpec((1, D), const), pl.BlockSpec((1, D), const)],
        out_specs=[pl.BlockSpec((tm, D), tok), pl.BlockSpec((tm, D), tok)],
        out_shape=[jax.ShapeDtypeStruct((T, D), F32), jax.ShapeDtypeStruct((T, D), BF16)],
        compiler_params=_params("parallel"),
        name="mix_out",
    )(ya, yb, yc, yd, norm_g, w_out, h, ln_g.reshape(1, D), ln_b.reshape(1, D))


def _xattn_kernel(hb_ref, h_ref, wq_ref, kv_ref, wo_ref, g_ref, b_ref, o_ref, ob_ref, *, alpha):
    width = wq_ref.shape[1]
    chunks = _row_chunks(o_ref.shape[0])
    qs = [jnp.dot(hb_ref[rows, :], wq_ref[...], preferred_element_type=F32) for rows in chunks]
    fs = []
    for q in qs:
        heads = []
        for hd in range(XA_HEADS):
            lo = hd * XA_HEAD_DIM
            k = kv_ref[:, lo:lo + XA_HEAD_DIM]
            v = kv_ref[:, width + lo:width + lo + XA_HEAD_DIM]
            s = _dot_nt(q[:, lo:lo + XA_HEAD_DIM], k) * (XA_HEAD_DIM ** -0.5)
            p = jnp.exp(s - jnp.max(s, -1, keepdims=True))
            p = p * (1.0 / jnp.sum(p, -1, keepdims=True))
            heads.append(_dot(p, v).astype(BF16))
        fs.append(jnp.dot(jnp.concatenate(heads, 1), wo_ref[...], preferred_element_type=F32))
    for rows, f in zip(chunks, fs):
        out = _layer_norm(alpha * h_ref[rows, :] + f, g_ref[...], b_ref[...])
        o_ref[rows, :] = out
        ob_ref[rows, :] = out.astype(BF16)


def _xattn(hb, h, wq, kv, wo, ln_g, ln_b, seq, mem_len, alpha, tm):
    T, D = h.shape
    width = wq.shape[1]
    ns = seq // tm
    tok = lambda i: (i, 0)
    const = lambda i: (0, 0)
    return pl.pallas_call(
        functools.partial(_xattn_kernel, alpha=alpha),
        grid=(T // tm,),
        in_specs=[pl.BlockSpec((tm, D), tok), pl.BlockSpec((tm, D), tok),
                  pl.BlockSpec((D, width), const),
                  pl.BlockSpec((mem_len, 2 * width), lambda i: (i // ns, 0)),
                  pl.BlockSpec((width, D), const),
                  pl.BlockSpec((1, D), const), pl.BlockSpec((1, D), const)],
        out_specs=[pl.BlockSpec((tm, D), tok), pl.BlockSpec((tm, D), tok)],
        out_shape=[jax.ShapeDtypeStruct((T, D), F32), jax.ShapeDtypeStruct((T, D), BF16)],
        compiler_params=_params("parallel"),
        name="cross_attn",
    )(hb, h, wq, kv, wo, ln_g.reshape(1, D), ln_b.reshape(1, D))


def _ffn_kernel(hb_ref, h_ref, wg_ref, wu_ref, wd_ref, g_ref, b_ref, o_ref, *rest, alpha):
    acc_ref = rest[-1]
    j = pl.program_id(1)

    @pl.when(j == 0)
    def _():
        acc_ref[...] = alpha * h_ref[...]

    x = hb_ref[...]
    gate = jnp.dot(x, wg_ref[...], preferred_element_type=F32)
    up = jnp.dot(x, wu_ref[...], preferred_element_type=F32)
    act = (jax.nn.silu(gate) * up).astype(BF16)
    acc_ref[...] += jnp.dot(act, wd_ref[...], preferred_element_type=F32)

    @pl.when(j == pl.num_programs(1) - 1)
    def _():
        out = _layer_norm(acc_ref[...], g_ref[...], b_ref[...])
        o_ref[...] = out
        if len(rest) == 2:
            rest[0][...] = out.astype(BF16)


def _ffn(hb, h, wg, wu, wd, ln_g, ln_b, alpha, tm, th, with_bf16_copy):
    T, D = h.shape
    H = wg.shape[1]
    tok = lambda i, j: (i, 0)
    const = lambda i, j: (0, 0)
    n_out = 2 if with_bf16_copy else 1
    return pl.pallas_call(
        functools.partial(_ffn_kernel, alpha=alpha),
        grid=(T // tm, H // th),
        in_specs=[pl.BlockSpec((tm, D), tok), pl.BlockSpec((tm, D), tok),
                  pl.BlockSpec((D, th), lambda i, j: (0, j)),
                  pl.BlockSpec((D, th), lambda i, j: (0, j)),
                  pl.BlockSpec((th, D), lambda i, j: (j, 0)),
                  pl.BlockSpec((1, D), const), pl.BlockSpec((1, D), const)],
        out_specs=[pl.BlockSpec((tm, D), tok), pl.BlockSpec((tm, D), tok)][:n_out],
        out_shape=[jax.ShapeDtypeStruct((T, D), F32), jax.ShapeDtypeStruct((T, D), BF16)][:n_out],
        scratch_shapes=[pltpu.VMEM((tm, D), F32)],
        compiler_params=_params("parallel", "arbitrary"),
        name="ffn",
    )(hb, h, wg, wu, wd, ln_g.reshape(1, D), ln_b.reshape(1, D))


def _rope_tables(pos):
    inv_freq = ROPE_THETA ** (-jnp.arange(HALF, dtype=F32) / HALF)
    ang = pos.astype(F32)[:, None] * inv_freq[None, :]
    cos = jnp.tile(jnp.cos(ang), (1, LANES // HALF))
    sin = jnp.sin(ang)
    sin = jnp.tile(jnp.concatenate([-sin, sin], 1), (1, LANES // HEAD_DIM))
    return cos, sin


def _overlap_table(seq):
    n_cmp = (seq - CMP_LEN) // CMP_STRIDE + 1
    n_sel = seq // SEL_BLOCK
    ci = np.arange(n_cmp)[:, None] * CMP_STRIDE
    sj = np.arange(n_sel)[None, :] * SEL_BLOCK
    ov = np.clip(np.minimum(ci + CMP_LEN, sj + SEL_BLOCK) - np.maximum(ci, sj), 0, None) / CMP_LEN
    full = np.zeros((LANES, seq // CMP_STRIDE), np.float32)
    full[:n_sel, :n_cmp] = ov.T
    return jnp.asarray(full, dtype=BF16)


def _split_w_in(w_in_stack, layer, width):
    hw = width // 4
    kvw = NSA_KV_HEADS * HEAD_DIM
    n_gate = 3 * (hw // HEAD_DIM)
    o = np.cumsum([0, hw, hw, 6 * kvw, n_gate, 3 * hw, 3 * hw])
    col_scale = np.ones((o[-1],), np.float32)
    for q_lo in (o[1], o[4], o[5]):
        col_scale[q_lo:q_lo + hw] = HEAD_DIM ** -0.5 * math.log2(math.e)
    w_in = _layer_bf16(w_in_stack, layer, jnp.asarray(col_scale))
    u = w_in[:, o[0]:o[1]]
    nq = w_in[:, o[1]:o[2]]
    kv = [w_in[:, o[2] + j * kvw:o[2] + (j + 1) * kvw] for j in range(6)]
    gate = w_in[:, o[3]:o[4]]
    sb = w_in[:, o[4]:o[5]]
    dil = w_in[:, o[5]:o[6]]
    per_group = 3 * NSA_REP
    gates = [jnp.pad(gate[:, g * per_group:(g + 1) * per_group], ((0, 0), (0, LANES - per_group)))
             for g in range(NSA_KV_HEADS)]
    w_rope = jnp.concatenate([nq, kv[2], kv[4], dil[:, :2 * hw]], 1)
    w_plain = jnp.concatenate([sb, dil[:, 2 * hw:], kv[0], kv[1], kv[3], kv[5]] + gates, 1)
    return u, w_rope, w_plain


R_NQ, R_KSLC, R_KWIN, R_DILQ, R_DILK = 0, 4, 5, 6, 10
P_SBQ, P_SBK, P_SBV, P_DILV, P_KCMP, P_VSLC, P_VWIN, P_GATE = 0, 4, 8, 12, 16, 18, 19, 20


def _hybrid_mixer(hb, h, nb, seq, w_in_stack, layer, s5_params, cmp_pe, cmp_w1, cmp_w2, norm_g, w_out,
                  ln_g, ln_b, tables, alpha, tm):
    cos, sin, cos_c, sin_c, overlap = tables
    width = w_out.shape[0]
    n_pairs = width // 4 // LANES
    w_u, w_rope, w_plain = _split_w_in(w_in_stack, layer, width)
    r_arr = _proj(hb, w_rope, min(2 * tm, seq), w_rope.shape[1], seq=seq, rope=(cos, sin))
    p_arr = _proj(hb, w_plain, tm, w_plain.shape[1])
    y_a = _s5(hb, w_u, nb, seq, *s5_params)
    cmp_kv = _compress(p_arr, P_KCMP, nb, seq, cmp_pe, cmp_w1, cmp_w2, cos_c, sin_c)
    y_b = _nsa(r_arr, p_arr, cmp_kv, overlap, nb, seq,
               dict(q=R_NQ, gate=P_GATE, k_slc=R_KSLC, v_slc=P_VSLC, k_win=R_KWIN, v_win=P_VWIN))
    y_c = _stick_breaking(p_arr, nb, seq, P_SBQ, P_SBK, P_SBV, n_pairs)
    y_d = _dilated(r_arr, R_DILQ, r_arr, R_DILK, p_arr, P_DILV, nb, seq, n_pairs)
    return _mix_out(y_a, y_b, y_c, y_d, norm_g, w_out, h, ln_g, ln_b, alpha, tm)


def kernel(x, mem, ln_in_g, ln_in_b, w_in, s5_lambda_re, s5_lambda_im, s5_log_dt, s5_b_re, s5_b_im, s5_c_re, s5_c_im, s5_d, s5_w_glu, s5_b_glu, nsa_cmp_pe, nsa_cmp_w1, nsa_cmp_w2, mix_norm_g, w_out, ln1_g, ln1_b, xa_wq, xa_wkv, xa_wo, ln2_g, ln2_b, ffn_w_gate, ffn_w_up, ffn_w_down, ln3_g, ln3_b):
    nb, seq, d_model = x.shape
    mem_len = mem.shape[1]
    depth = w_in.shape[0]
    alpha = (2 * depth) ** 0.25
    tm = min(512, seq)
    pos = jnp.arange(seq)
    cos, sin = _rope_tables(pos)
    nblk = seq // CMP_STRIDE
    cos_c, sin_c = _rope_tables(jnp.arange(nblk) * CMP_STRIDE + CMP_LEN - 1)
    tables = (cos, sin, cos_c, sin_c, _overlap_table(seq))
    mem_b = mem.reshape(nb * mem_len, d_model).astype(BF16)

    h, hb = _ln_in(x.reshape(nb * seq, d_model), ln_in_g, ln_in_b, tm)
    for l in range(depth):
        s5_params = (s5_lambda_re[l], s5_lambda_im[l], s5_log_dt[l], s5_b_re[l], s5_b_im[l],
                     s5_c_re[l], s5_c_im[l], s5_d[l], s5_w_glu[l], s5_b_glu[l])
        h, hb = _hybrid_mixer(hb, h, nb, seq, w_in, l, s5_params, nsa_cmp_pe[l], nsa_cmp_w1[l],
                              nsa_cmp_w2[l], mix_norm_g[l], _layer_bf16(w_out, l), ln1_g[l], ln1_b[l],
                              tables, alpha, tm)
        kv = _proj(mem_b, _layer_bf16(xa_wkv, l), min(512, nb * mem_len), xa_wkv.shape[2] // 2)
        h, hb = _xattn(hb, h, _layer_bf16(xa_wq, l), kv, _layer_bf16(xa_wo, l), ln2_g[l], ln2_b[l],
                       seq, mem_len, alpha, tm)
        last = l == depth - 1
        outs = _ffn(hb, h, _layer_bf16(ffn_w_gate, l), _layer_bf16(ffn_w_up, l),
                    _layer_bf16(ffn_w_down, l), ln3_g[l], ln3_b[l], alpha, tm, 512, not last)
        h, hb = (outs[0], None) if last else outs
    return h.reshape(nb, seq, d_model)
```

```python
import functools
import math

import numpy as np
import jax
import jax.numpy as jnp
from jax import lax
from jax.experimental import pallas as pl
from jax.experimental.pallas import tpu as pltpu

F32 = jnp.float32
BF16 = jnp.bfloat16

LANES = 128
MXU_WIDTH = 256
CAST_ROWS = 512
VMEM_LIMIT = 56 * 1024 * 1024

HEAD_DIM = 64
HALF = HEAD_DIM // 2
ROPE_THETA = 10000.0
LN_EPS = 1e-5
RMS_EPS = 1e-6
SSM_CH = 16
SSM_STATE = 64
NSA_KV_HEADS = 2
NSA_REP = 4
CMP_LEN = 32
CMP_STRIDE = 16
CMP_HIDDEN = 128
SEL_BLOCK = 64
SEL_TOPN = 8
NSA_WINDOW = 512
DIL_CONFIGS = ((128, 1), (512, 4), (2048, 16))
XA_HEADS = 4
XA_HEAD_DIM = 128
Q_TILE = 128
SB_TILE = 256
SB_PAIRS = 4
NSA_TILE = 256
NEG_BIG = -1e30


def _params(*sem):
    return pltpu.CompilerParams(dimension_semantics=sem, vmem_limit_bytes=VMEM_LIMIT)


def _dot(a, b):
    return jnp.dot(a.astype(BF16), b.astype(BF16), preferred_element_type=F32)


def _dot_nt(a, b):
    return lax.dot_general(a.astype(BF16), b.astype(BF16), (((1,), (1,)), ((), ())),
                           preferred_element_type=F32)


def _layer_norm(x, g, b):
    mu = jnp.mean(x, -1, keepdims=True)
    xc = x - mu
    var = jnp.mean(xc * xc, -1, keepdims=True)
    return xc * lax.rsqrt(var + LN_EPS) * g + b


def _lane_iota(shape):
    return lax.broadcasted_iota(jnp.int32, shape, len(shape) - 1)


def _row_iota(shape):
    return lax.broadcasted_iota(jnp.int32, shape, len(shape) - 2)


def _swap_halves(x):
    return pltpu.roll(x, HEAD_DIM, axis=x.ndim - 1)


def _ln_in_kernel(x_ref, g_ref, b_ref, h_ref, hb_ref):
    y = _layer_norm(x_ref[...], g_ref[...], b_ref[...])
    h_ref[...] = y
    hb_ref[...] = y.astype(BF16)


def _ln_in(x2, g, b, tm):
    T, D = x2.shape
    return pl.pallas_call(
        _ln_in_kernel,
        grid=(T // tm,),
        in_specs=[pl.BlockSpec((tm, D), lambda i: (i, 0)),
                  pl.BlockSpec((1, D), lambda i: (0, 0)),
                  pl.BlockSpec((1, D), lambda i: (0, 0))],
        out_specs=[pl.BlockSpec((tm, D), lambda i: (i, 0)),
                   pl.BlockSpec((tm, D), lambda i: (i, 0))],
        out_shape=[jax.ShapeDtypeStruct((T, D), F32), jax.ShapeDtypeStruct((T, D), BF16)],
        compiler_params=_params("parallel"),
        name="ln_in",
    )(x2, g.reshape(1, D), b.reshape(1, D))


def _cast_kernel(w_ref, o_ref):
    o_ref[...] = w_ref[...].astype(BF16)


def _cast_scaled_kernel(w_ref, s_ref, o_ref):
    o_ref[...] = (w_ref[...] * s_ref[...]).astype(BF16)


def _layer_bf16(w_stack, layer, col_scale=None):
    _, K, N = w_stack.shape
    tk = min(CAST_ROWS, K)
    in_specs = [pl.BlockSpec((None, tk, N), lambda i: (layer, i, 0))]
    args = [w_stack]
    kern = _cast_kernel
    if col_scale is not None:
        in_specs.append(pl.BlockSpec((1, N), lambda i: (0, 0)))
        args.append(col_scale.reshape(1, N))
        kern = _cast_scaled_kernel
    return pl.pallas_call(
        kern, grid=(K // tk,), in_specs=in_specs,
        out_specs=pl.BlockSpec((tk, N), lambda i: (i, 0)),
        out_shape=jax.ShapeDtypeStruct((K, N), BF16),
        compiler_params=_params("parallel"), name="cast_bf16",
    )(*args)


def _row_chunks(n, parts=2):
    step = n // parts
    return [slice(i * step, (i + 1) * step) for i in range(parts)]


def _column_chunks(n):
    return [(lo, min(lo + MXU_WIDTH, n)) for lo in range(0, n, MXU_WIDTH)]


def _proj_kernel(a_ref, w_ref, o_ref):
    a = a_ref[...]
    for lo, hi in _column_chunks(o_ref.shape[1]):
        o_ref[:, lo:hi] = jnp.dot(a, w_ref[:, lo:hi], preferred_element_type=F32)


def _proj_rope_kernel(a_ref, w_ref, cos_ref, sin_ref, o_ref):
    a = a_ref[...]
    cos = cos_ref[...]
    sin = sin_ref[...]
    first = (_lane_iota((1, LANES)) % HEAD_DIM) < HALF
    for lo, hi in _column_chunks(o_ref.shape[1]):
        acc = jnp.dot(a, w_ref[:, lo:hi], preferred_element_type=F32)
        for c in range((hi - lo) // LANES):
            x = acc[:, c * LANES:(c + 1) * LANES]
            partner = jnp.where(first, pltpu.roll(x, LANES - HALF, axis=1), pltpu.roll(x, HALF, axis=1))
            o_ref[:, lo + c * LANES:lo + (c + 1) * LANES] = x * cos + partner * sin


def _proj(a, w, tm, tn, seq=None, rope=None):
    M, K = a.shape
    N = w.shape[1]
    nm, nn = M // tm, N // tn
    in_specs = [pl.BlockSpec((tm, K), lambda j, i: (i, 0)),
                pl.BlockSpec((K, tn), lambda j, i: (0, j))]
    args = [a, w]
    kern = _proj_kernel
    if rope is not None:
        ns = seq // tm
        in_specs += [pl.BlockSpec((tm, LANES), lambda j, i: (i % ns, 0))] * 2
        args += list(rope)
        kern = _proj_rope_kernel
    return pl.pallas_call(
        kern, grid=(nn, nm), in_specs=in_specs,
        out_specs=pl.BlockSpec((tm, tn), lambda j, i: (i, j)),
        out_shape=jax.ShapeDtypeStruct((M, N), F32),
        compiler_params=_params("parallel", "parallel"), name="proj",
    )(*args)


def _s5_kernel(h_ref, wu_ref, bb_ref, cc_ref, are_ref, aim_ref, d_ref, wg_ref, bg_ref, y_ref,
               buf_ref, st_ref, tm_ref, *, nb, ts):
    n_chunks, cw, sw2 = bb_ref.shape
    sw = sw2 // 2
    assert cw == tm_ref.shape[2]

    @pl.when(pl.program_id(0) == 0)
    def _():
        st_ref[...] = jnp.zeros_like(st_ref)

    u_bt = jnp.dot(h_ref[...].reshape(nb * ts, h_ref.shape[2]), wu_ref[...],
                   preferred_element_type=F32)
    for b in range(nb):
        for c in range(n_chunks):
            tm_ref[c, pl.ds(b, ts, stride=nb), :] = u_bt[b * ts:(b + 1) * ts, c * cw:(c + 1) * cw]
    u = jnp.concatenate([tm_ref[c] for c in range(n_chunks)], 1)
    ys = []
    for c in range(n_chunks):
        re0, im0 = c * sw2, c * sw2 + sw
        buf_ref[:, re0:re0 + sw2] = jnp.dot(u[:, c * cw:(c + 1) * cw].astype(BF16), bb_ref[c],
                                            preferred_element_type=F32)
        a_re = jnp.broadcast_to(are_ref[:, c * sw:(c + 1) * sw], (nb, sw))
        a_im = jnp.broadcast_to(aim_ref[:, c * sw:(c + 1) * sw], (nb, sw))

        def step(t, carry, re0=re0, im0=im0, a_re=a_re, a_im=a_im):
            x_re, x_im = carry
            r = pl.multiple_of(t * nb, nb)
            n_re = a_re * x_re - a_im * x_im + buf_ref[pl.ds(r, nb), re0:re0 + sw]
            n_im = a_re * x_im + a_im * x_re + buf_ref[pl.ds(r, nb), im0:im0 + sw]
            buf_ref[pl.ds(r, nb), re0:re0 + sw] = n_re
            buf_ref[pl.ds(r, nb), im0:im0 + sw] = n_im
            return n_re, n_im

        x_re, x_im = lax.fori_loop(0, ts, step, (st_ref[:, re0:re0 + sw], st_ref[:, im0:im0 + sw]),
                                   unroll=True)
        st_ref[:, re0:re0 + sw] = x_re
        st_ref[:, im0:im0 + sw] = x_im
        ys.append(jnp.dot(buf_ref[:, re0:re0 + sw2].astype(BF16), cc_ref[c], preferred_element_type=F32))

    y = jnp.concatenate(ys, 1) + d_ref[...] * u
    g = jax.nn.gelu(y)
    z = jnp.dot(g.astype(BF16), wg_ref[...], preferred_element_type=F32) + bg_ref[...]
    out = g * jax.nn.sigmoid(z)
    for c in range(n_chunks):
        tm_ref[c] = out[:, c * cw:(c + 1) * cw]
    for b in range(nb):
        for c in range(n_chunks):
            y_ref[b, :, c * cw:(c + 1) * cw] = tm_ref[c, pl.ds(b, ts, stride=nb), :]


def _s5(hb, w_u, nb, seq, lam_re, lam_im, log_dt, b_re, b_im, c_re, c_im, d_skip, w_glu, b_glu, ts=64):
    D, W = w_u.shape
    G, P = lam_re.shape
    C = SSM_CH
    lr = jnp.minimum(lam_re, -1e-4)
    li = lam_im
    dt = jnp.exp(log_dt)[:, None]
    mag = jnp.exp(lr * dt)
    a_re = mag * jnp.cos(li * dt)
    a_im = mag * jnp.sin(li * dt)
    den = lr * lr + li * li
    z_re = ((a_re - 1.0) * lr + a_im * li) / den
    z_im = (a_im * lr - (a_re - 1.0) * li) / den
    bb_re = z_re[..., None] * b_re - z_im[..., None] * b_im
    bb_im = z_re[..., None] * b_im + z_im[..., None] * b_re
    gc = LANES // C
    nc = G // gc
    eye = jnp.eye(gc, dtype=F32)

    def block_diag_in(m):
        return jnp.einsum('ngpc,gh->ngchp', m.reshape(nc, gc, P, C), eye).reshape(nc, gc * C, gc * P)

    def block_diag_out(m):
        return jnp.einsum('ngcp,gh->ngphc', m.reshape(nc, gc, C, P), eye).reshape(nc, gc * P, gc * C)

    bb = jnp.concatenate([block_diag_in(bb_re), block_diag_in(bb_im)], 2).astype(BF16)
    cc = jnp.concatenate([block_diag_out(c_re), -block_diag_out(c_im)], 1).astype(BF16)
    ns = G * P
    rows = ts * nb
    kern = functools.partial(_s5_kernel, nb=nb, ts=ts)
    const = lambda i: (0, 0)
    const3 = lambda i: (0, 0, 0)
    y = pl.pallas_call(
        kern,
        grid=(seq // ts,),
        in_specs=[pl.BlockSpec((nb, ts, D), lambda i: (0, i, 0)),
                  pl.BlockSpec((D, W), const),
                  pl.BlockSpec(bb.shape, const3),
                  pl.BlockSpec(cc.shape, const3),
                  pl.BlockSpec((1, ns), const),
                  pl.BlockSpec((1, ns), const),
                  pl.BlockSpec((1, W), const),
                  pl.BlockSpec((W, W), const),
                  pl.BlockSpec((1, W), const)],
        out_specs=pl.BlockSpec((nb, ts, W), lambda i: (0, i, 0)),
        out_shape=jax.ShapeDtypeStruct((nb, seq, W), F32),
        scratch_shapes=[pltpu.VMEM((rows, 2 * ns), F32), pltpu.VMEM((nb, 2 * ns), F32),
                        pltpu.VMEM((nc, rows, LANES), F32)],
        compiler_params=_params("arbitrary"),
        name="s5",
    )(hb.reshape(nb, seq, D), w_u, bb, cc, a_re.reshape(1, ns), a_im.reshape(1, ns),
      d_skip.reshape(1, W), w_glu.astype(BF16), b_glu.reshape(1, W))
    return y.reshape(nb * seq, W)


def _stack_pair(q):
    first = _lane_iota((1, LANES)) < HEAD_DIM
    return jnp.concatenate([jnp.where(first, q, 0.0), jnp.where(first, 0.0, q)], 0)


def _unstack_pair(x):
    t = x.shape[0] // 2
    first = _lane_iota((1, LANES)) < HEAD_DIM
    return jnp.where(first, x[:t], x[t:])


def _sb_kernel(q_ref, k_ref, v_ref, o_ref):
    tq = q_ref.shape[0]
    qi = pl.program_id(2)
    n_pairs = q_ref.shape[1] // LANES
    groups = [slice(p * LANES, (p + 1) * LANES) for p in range(n_pairs)]
    qs = [_stack_pair(q_ref[:, g]).astype(BF16) for g in groups]
    tri = (_row_iota((tq, tq)) > _lane_iota((tq, tq))).astype(BF16)
    diag = _lane_iota((2 * tq, tq)) < (_row_iota((2 * tq, tq)) % tq)

    def tile(kj, carry, masked):
        r = pl.multiple_of(kj * tq, tq)
        zs = [_dot_nt(q, k_ref[pl.ds(r, tq), g]) for q, g in zip(qs, groups)]
        stage = []
        for z in zs:
            pos = jnp.maximum(z, 0.0)
            neg = z - pos
            log_term = jnp.log2(1.0 + jnp.exp2(neg - pos))
            sp = pos + log_term
            if masked:
                sp = jnp.where(diag, sp, 0.0)
            stage.append((neg - log_term, jnp.sum(sp, -1, keepdims=True), _dot(sp, tri)))
        out = []
        for (log_beta, row_sum, after), g, (acc, tail) in zip(stage, groups, carry):
            w = jnp.exp2(log_beta - (after + tail))
            if masked:
                w = jnp.where(diag, w, 0.0)
            out.append((acc + _dot(w, v_ref[pl.ds(r, tq), g]), tail + row_sum))
        return tuple(out)

    init = tuple((jnp.zeros((2 * tq, LANES), F32), jnp.zeros((2 * tq, 1), F32)) for _ in groups)
    carry = tile(qi, init, True)
    carry = lax.fori_loop(0, qi // 2,
                          lambda i, c: tile(qi - 2 - 2 * i, tile(qi - 1 - 2 * i, c, False), False), carry)
    carry = lax.fori_loop(0, qi % 2, lambda i, c: tile(0, c, False), carry)
    for g, (acc, _) in zip(groups, carry):
        o_ref[:, g] = _unstack_pair(acc)


def _stick_breaking(qkv, nb, seq, q_col, k_col, v_col, n_pairs):
    tq = min(SB_TILE, seq)
    nq = seq // tq
    per = SB_PAIRS
    w = per * LANES
    assert n_pairs % per == 0 and q_col % per == 0 and k_col % per == 0 and v_col % per == 0
    return pl.pallas_call(
        _sb_kernel,
        grid=(nb, n_pairs // per, nq),
        in_specs=[pl.BlockSpec((tq, w), lambda b, p, i: (b * nq + i, q_col // per + p)),
                  pl.BlockSpec((seq, w), lambda b, p, i: (b, k_col // per + p)),
                  pl.BlockSpec((seq, w), lambda b, p, i: (b, v_col // per + p))],
        out_specs=pl.BlockSpec((tq, w), lambda b, p, i: (b * nq + i, p)),
        out_shape=jax.ShapeDtypeStruct((nb * seq, n_pairs * LANES), F32),
        compiler_params=_params("parallel", "parallel", "arbitrary"),
        name="stick_breaking",
    )(qkv, qkv, qkv)


def _dil_kernel(q_ref, k_ref, v_ref, o_ref, m_ref, l_ref, a_ref):
    seq = q_ref.shape[0]
    tq = Q_TILE
    first = _lane_iota((1, LANES)) < HEAD_DIM

    q_in = _row_iota((2 * tq, 1)) % tq

    def band_bias(wd, n_tiles):
        if n_tiles > 1:
            lag = q_in - (_lane_iota((1, 2 * tq)) - tq)
            return jnp.where((lag >= 0) & (lag <= wd), 0.0, NEG_BIG)
        return jnp.where(_lane_iota((1, tq)) <= q_in, 0.0, NEG_BIG)

    def tile(c, dil, band, n_tiles, r, i):
        base = r + dil * tq * i

        def rows(ref, start):
            if dil == 1:
                return ref[pl.ds(pl.multiple_of(start, tq), tq), :]
            return ref[pl.ds(start, tq, stride=dil), :]

        qs = _stack_pair(rows(q_ref, base)).astype(BF16)
        if n_tiles > 1:
            prev = r + dil * tq * jnp.maximum(i - 1, 0)
            kk = jnp.concatenate([rows(k_ref, prev), rows(k_ref, base)], 0)
            vv = jnp.concatenate([rows(v_ref, prev), rows(v_ref, base)], 0)
            no_prev = jnp.where((i == 0) & (_lane_iota((1, 2 * tq)) < tq), NEG_BIG, 0.0)
            bias = band + no_prev
        else:
            kk = rows(k_ref, base)
            vv = rows(v_ref, base)
            bias = band
        s = _dot_nt(qs, kk) + bias
        m = jnp.max(s, -1, keepdims=True)
        p = jnp.exp2(s - m)
        l = jnp.sum(p, -1, keepdims=True)
        acc = _dot(p, vv)
        m2 = jnp.where(first, m[:tq], m[tq:])
        l2 = jnp.where(first, l[:tq], l[tq:])
        a2 = jnp.where(first, acc[:tq], acc[tq:])
        if dil == 1:
            sl = pl.ds(pl.multiple_of(base, tq), tq)
        else:
            sl = pl.ds(base, tq, stride=dil)
        m_ref[c, sl, :] = m2
        l_ref[c, sl, :] = l2
        a_ref[c, sl, :] = a2

    for c, (window, dil) in enumerate(DIL_CONFIGS):
        wd = window // dil
        n_tiles = seq // dil // tq

        band = band_bias(wd, n_tiles)

        def per_tile(n, _, c=c, dil=dil, band=band, n_tiles=n_tiles):
            tile(c, dil, band, n_tiles, n // n_tiles, n % n_tiles)
            return 0

        lax.fori_loop(0, dil * n_tiles, per_tile, 0, unroll=16)

    def combine(i, _):
        sl = pl.ds(pl.multiple_of(i * tq, tq), tq)
        m0, m1, m2 = m_ref[0, sl, :], m_ref[1, sl, :], m_ref[2, sl, :]
        mx = jnp.maximum(jnp.maximum(m0, m1), m2)
        e0, e1, e2 = jnp.exp2(m0 - mx), jnp.exp2(m1 - mx), jnp.exp2(m2 - mx)
        num = e0 * a_ref[0, sl, :] + e1 * a_ref[1, sl, :] + e2 * a_ref[2, sl, :]
        den = e0 * l_ref[0, sl, :] + e1 * l_ref[1, sl, :] + e2 * l_ref[2, sl, :]
        o_ref[sl, :] = num / den
        return 0

    lax.fori_loop(0, seq // tq, combine, 0)


def _dilated(q_arr, q_col, k_arr, k_col, v_arr, v_col, nb, seq, n_pairs):
    return pl.pallas_call(
        _dil_kernel,
        grid=(nb, n_pairs),
        in_specs=[pl.BlockSpec((seq, LANES), lambda b, p: (b, q_col + p)),
                  pl.BlockSpec((seq, LANES), lambda b, p: (b, k_col + p)),
                  pl.BlockSpec((seq, LANES), lambda b, p: (b, v_col + p))],
        out_specs=pl.BlockSpec((seq, LANES), lambda b, p: (b, p)),
        out_shape=jax.ShapeDtypeStruct((nb * seq, n_pairs * LANES), F32),
        scratch_shapes=[pltpu.VMEM((3, seq, LANES), F32)] * 3,
        compiler_params=_params("parallel", "parallel"),
        name="dilated",
    )(q_arr, k_arr, v_arr)


def _cmp_kernel(t_ref, pe_ref, w1a_ref, w1b_ref, w2_ref, cos_ref, sin_ref, o_ref):
    nblk = t_ref.shape[0] // CMP_STRIDE
    j = pl.program_id(1)
    out = jnp.zeros((nblk, LANES), F32)
    for g in range(NSA_KV_HEADS):
        p1 = jnp.zeros((nblk, CMP_HIDDEN), F32)
        p2 = jnp.zeros((nblk, CMP_HIDDEN), F32)
        for l in range(CMP_STRIDE):
            x = t_ref[pl.ds(l, nblk, stride=CMP_STRIDE), :]
            p1 = p1 + _dot(x + pe_ref[0, l:l + 1, :], w1a_ref[0, g, l])
            p2 = p2 + _dot(x + pe_ref[0, CMP_STRIDE + l:CMP_STRIDE + l + 1, :], w1b_ref[0, g, l])
        hidden = p1 + pltpu.roll(p2, nblk - 1, axis=0)
        out = out + _dot(jax.nn.gelu(hidden), w2_ref[0, g])
    first = (_lane_iota((1, LANES)) % HEAD_DIM) < HALF
    partner = jnp.where(first, pltpu.roll(out, LANES - HALF, axis=1), pltpu.roll(out, HALF, axis=1))
    roped = out * cos_ref[...] + partner * sin_ref[...]
    o_ref[0] = jnp.where(j == 0, roped, out)


def _compress(p_arr, col0, nb, seq, pe, w1, w2, cos_c, sin_c):
    nblk = seq // CMP_STRIDE
    G = NSA_KV_HEADS
    pe2 = jnp.tile(pe, (1, 1, G))
    w1r = w1.reshape(2, CMP_LEN, HEAD_DIM, CMP_HIDDEN)
    w1e = jnp.zeros((2, G, CMP_LEN, LANES, CMP_HIDDEN), F32)
    w2e = jnp.zeros((2, G, CMP_HIDDEN, LANES), F32)
    for g in range(G):
        w1e = w1e.at[:, g, :, g * HEAD_DIM:(g + 1) * HEAD_DIM, :].set(w1r)
        w2e = w2e.at[:, g, :, g * HEAD_DIM:(g + 1) * HEAD_DIM].set(w2)
    w1e = w1e.astype(BF16)
    w2e = w2e.astype(BF16)
    return pl.pallas_call(
        _cmp_kernel,
        grid=(nb, 2),
        in_specs=[pl.BlockSpec((seq, LANES), lambda b, j: (b, col0 + j)),
                  pl.BlockSpec((1, CMP_LEN, LANES), lambda b, j: (j, 0, 0)),
                  pl.BlockSpec((1, G, CMP_STRIDE, LANES, CMP_HIDDEN), lambda b, j: (j, 0, 0, 0, 0)),
                  pl.BlockSpec((1, G, CMP_STRIDE, LANES, CMP_HIDDEN), lambda b, j: (j, 0, 1, 0, 0)),
                  pl.BlockSpec((1, G, CMP_HIDDEN, LANES), lambda b, j: (j, 0, 0, 0)),
                  pl.BlockSpec((nblk, LANES), lambda b, j: (0, 0)),
                  pl.BlockSpec((nblk, LANES), lambda b, j: (0, 0))],
        out_specs=pl.BlockSpec((1, nblk, LANES), lambda b, j: (j, b, 0)),
        out_shape=jax.ShapeDtypeStruct((2, nb * nblk, LANES), F32),
        compiler_params=_params("parallel", "parallel"),
        name="nsa_compress",
    )(p_arr, pe2, w1e, w1e, w2e, cos_c, sin_c)


def _nsa_kernel(q_ref, gate_ref, kc_ref, vc_ref, ks_ref, vs_ref, kw_ref, vw_ref, ovt_ref, o_ref):
    tq = q_ref.shape[0]
    R, G = NSA_REP, NSA_KV_HEADS
    seq = ks_ref.shape[0]
    n_sel = seq // SEL_BLOCK
    n_cmp = (seq - CMP_LEN) // CMP_STRIDE + 1
    nblk = kc_ref.shape[1]
    top_n = min(SEL_TOPN, n_sel)
    qi = pl.program_id(1)
    lane = _lane_iota((1, LANES))
    first = lane < HEAD_DIM
    mine = [first, jnp.logical_not(first)]
    t_row = qi * tq + _row_iota((tq, 1))

    qs = []
    for g in range(G):
        parts = []
        for r in range(R):
            h = g * R + r
            x = q_ref[:, (h // 2) * LANES:(h // 2 + 1) * LANES]
            parts.append(jnp.where(mine[g], x if h % 2 == g else _swap_halves(x), 0.0))
        qs.append(jnp.concatenate(parts, 0).astype(BF16))

    c_idx = _lane_iota((tq, nblk))
    visible = (c_idx < n_cmp) & (c_idx * CMP_STRIDE + (CMP_LEN - 1) <= qi * tq + _row_iota((tq, nblk)))
    c_bias = jnp.where(visible, 0.0, NEG_BIG)
    kc, vc = kc_ref[0], vc_ref[0]
    vc_swapped = _swap_halves(vc)
    s_cs = [_dot_nt(q, kc).reshape(R, tq, nblk) + c_bias for q in qs]
    p_cs, o_cs = [], []
    for g, s_c in enumerate(s_cs):
        m_c = jnp.max(s_c, -1, keepdims=True)
        m_c = jnp.where(m_c > 0.5 * NEG_BIG, m_c, 0.0)
        p_c = jnp.exp2(s_c - m_c)
        p_c = p_c * (1.0 / jnp.maximum(jnp.sum(p_c, -1, keepdims=True), 1e-30))
        p_cs.append(p_c)
        o_cs.append(_dot(p_c.reshape(R * tq, nblk), jnp.where(mine[g], vc, vc_swapped)).reshape(R, tq, LANES))

    n_idx = _row_iota((n_sel, tq))
    t_q = qi * tq + _lane_iota((n_sel, tq))
    cur = t_q // SEL_BLOCK
    future = n_idx * SEL_BLOCK > t_q
    forced = (n_idx == 0) | (n_idx == cur) | (n_idx == cur - 1)
    sels = []
    for p_c in p_cs:
        psum = jnp.sum(p_c, 0)
        p_hi = psum.astype(BF16)
        p_lo = (psum - p_hi.astype(F32)).astype(BF16)
        imp = (_dot_nt(ovt_ref[...], p_hi) + _dot_nt(ovt_ref[...], p_lo))[:n_sel]
        imp = jnp.where(forced, 1e9, jnp.where(future, -1.0, imp))
        rank = jnp.zeros((n_sel, tq), F32)
        for m in range(n_sel):
            row = imp[m:m + 1, :]
            earlier = (n_idx > m).astype(F32)
            rank = rank + jnp.where(row > imp, 1.0, jnp.where(row == imp, earlier, 0.0))
        sel_t = jnp.where((rank < top_n) & (imp > -0.5), 1.0, 0.0)
        sel_t = jnp.concatenate([sel_t, jnp.zeros((LANES - n_sel, tq), F32)], 0)
        sels.append(sel_t.T.astype(BF16))

    k_col = _lane_iota((tq, tq))
    blk_row = _row_iota((LANES, tq))
    blk_of_key = _lane_iota((LANES, tq)) // SEL_BLOCK

    def sel_biases(j, causal):
        expand = (blk_row == (j * (tq // SEL_BLOCK) + blk_of_key)).astype(BF16)
        out = []
        for sel in sels:
            chosen = jnp.dot(sel, expand, preferred_element_type=F32)
            bias = (1.0 - chosen) * NEG_BIG
            if causal:
                bias = jnp.where((j * tq + k_col) <= t_row, bias, NEG_BIG)
            out.append(bias)
        return out

    def win_bias(j):
        diff = t_row - (j * tq + k_col)
        return jnp.where((diff >= 0) & (diff < NSA_WINDOW), 0.0, NEG_BIG)

    def tile_of(ref, j):
        return ref[pl.ds(pl.multiple_of(j * tq, tq), tq), :]

    def scores(k_ref, j, biases):
        kt = tile_of(k_ref, j).astype(BF16)
        return [_dot_nt(q, kt).reshape(R, tq, tq) + b for q, b in zip(qs, biases)]

    def update(carry, s_groups, v_ref, j):
        v = tile_of(v_ref, j)
        out = []
        for g, ((m_run, acc), s) in enumerate(zip(carry, s_groups)):
            m_new = jnp.maximum(m_run, jnp.max(s, -1, keepdims=True))
            alpha = jnp.exp2(m_run - m_new)
            p = jnp.exp2(s - m_new)
            pv = _dot(p.reshape(R * tq, tq), jnp.where(mine[g], v, 1.0))
            out.append((m_new, alpha * acc + pv.reshape(R, tq, LANES)))
        return tuple(out)

    def normalise(carry):
        out = []
        for g, (_, acc) in enumerate(carry):
            swapped = _swap_halves(acc)
            out.append(jnp.where(mine[g], acc, swapped) / jnp.where(mine[g], swapped, acc))
        return out

    init = tuple((jnp.full((R, tq, 1), NEG_BIG, F32), jnp.zeros((R, tq, LANES), F32)) for _ in range(G))
    j_win = jnp.maximum(qi - NSA_WINDOW // tq, 0)

    def selected_only(j, carry):
        return update(carry, scores(ks_ref, j, sel_biases(j, False)), vs_ref, j)

    def selected_and_window(j, carry):
        b_w = win_bias(j)
        s_s, s_w = scores(ks_ref, j, sel_biases(j, True)), scores(kw_ref, j, [b_w] * G)
        return update(carry[0], s_s, vs_ref, j), update(carry[1], s_w, vw_ref, j)

    c_sel = lax.fori_loop(0, j_win, selected_only, init)
    c_sel, c_win = lax.fori_loop(j_win, qi + 1, selected_and_window, (c_sel, init))
    o_ss, o_ws = normalise(c_sel), normalise(c_win)

    gate = jax.nn.sigmoid(gate_ref[...])
    outs = []
    for g in range(G):
        for r in range(R):
            c0 = g * LANES + 3 * r
            outs.append(gate[:, c0:c0 + 1] * o_cs[g][r] + gate[:, c0 + 1:c0 + 2] * o_ss[g][r]
                        + gate[:, c0 + 2:c0 + 3] * o_ws[g][r])
    for pair in range(G * R // 2):
        o_ref[:, pair * LANES:(pair + 1) * LANES] = jnp.where(first, outs[2 * pair], outs[2 * pair + 1])


def _nsa(r_arr, p_arr, cmp_kv, overlap, nb, seq, cols):
    tq = min(NSA_TILE, seq)
    nq = seq // tq
    nblk = seq // CMP_STRIDE
    G = NSA_KV_HEADS
    qw = G * NSA_REP * HEAD_DIM
    gw = G * LANES
    assert (cols['q'] * LANES) % qw == 0 and (cols['gate'] * LANES) % gw == 0
    return pl.pallas_call(
        _nsa_kernel,
        grid=(nb, nq),
        in_specs=[pl.BlockSpec((tq, qw), lambda b, i: (b * nq + i, cols['q'] * LANES // qw)),
                  pl.BlockSpec((tq, gw), lambda b, i: (b * nq + i, cols['gate'] * LANES // gw)),
                  pl.BlockSpec((1, nblk, LANES), lambda b, i: (0, b, 0)),
                  pl.BlockSpec((1, nblk, LANES), lambda b, i: (1, b, 0)),
                  pl.BlockSpec((seq, LANES), lambda b, i: (b, cols['k_slc'])),
                  pl.BlockSpec((seq, LANES), lambda b, i: (b, cols['v_slc'])),
                  pl.BlockSpec((seq, LANES), lambda b, i: (b, cols['k_win'])),
                  pl.BlockSpec((seq, LANES), lambda b, i: (b, cols['v_win'])),
                  pl.BlockSpec((LANES, nblk), lambda b, i: (0, 0))],
        out_specs=pl.BlockSpec((tq, qw), lambda b, i: (b * nq + i, 0)),
        out_shape=jax.ShapeDtypeStruct((nb * seq, qw), F32),
        compiler_params=_params("parallel", "arbitrary"),
        name="nsa",
    )(r_arr, p_arr, cmp_kv, cmp_kv, r_arr, p_arr, r_arr, p_arr, overlap)


def _mix_out_kernel(ya_ref, yb_ref, yc_ref, yd_ref, ng_ref, w_ref, h_ref, g_ref, b_ref,
                    o_ref, ob_ref, *, alpha):
    fs = []
    for rows in _row_chunks(o_ref.shape[0]):
        ys = []
        for n, y_ref in enumerate((ya_ref, yb_ref, yc_ref, yd_ref)):
            y = y_ref[rows, :]
            y = y * lax.rsqrt(jnp.mean(y * y, -1, keepdims=True) + RMS_EPS) * ng_ref[n:n + 1, :]
            ys.append(y.astype(BF16))
        fs.append(jnp.dot(jnp.concatenate(ys, 1), w_ref[...], preferred_element_type=F32))
    for rows, f in zip(_row_chunks(o_ref.shape[0]), fs):
        out = _layer_norm(alpha * h_ref[rows, :] + f, g_ref[...], b_ref[...])
        o_ref[rows, :] = out
        ob_ref[rows, :] = out.astype(BF16)


def _mix_out(ya, yb, yc, yd, norm_g, w_out, h, ln_g, ln_b, alpha, tm):
    T, D = h.shape
    gw = yb.shape[1]
    tok = lambda i: (i, 0)
    const = lambda i: (0, 0)
    return pl.pallas_call(
        functools.partial(_mix_out_kernel, alpha=alpha),
        grid=(T // tm,),
        in_specs=[pl.BlockSpec((tm, gw), tok),
                  pl.BlockSpec((tm, gw), tok), pl.BlockSpec((tm, gw), tok), pl.BlockSpec((tm, gw), tok),
                  pl.BlockSpec((4, gw), const),
                  pl.BlockSpec((4 * gw, D), const),
                  pl.BlockSpec((tm, D), tok),
                  pl.BlockSpec((1, D), const), pl.BlockSpec((1, D), const)],
        out_specs=[pl.BlockSpec((tm, D), tok), pl.BlockSpec((tm, D), tok)],
        out_shape=[jax.ShapeDtypeStruct((T, D), F32), jax.ShapeDtypeStruct((T, D), BF16)],
        compiler_params=_params("parallel"),
        name="mix_out",
    )(ya, yb, yc, yd, norm_g, w_out, h, ln_g.reshape(1, D), ln_b.reshape(1, D))


def _xattn_kernel(hb_ref, h_ref, wq_ref, kv_ref, wo_ref, g_ref, b_ref, o_ref, ob_ref, *, alpha):
    width = wq_ref.shape[1]
    chunks = _row_chunks(o_ref.shape[0])
    qs = [jnp.dot(hb_ref[rows, :], wq_ref[...], preferred_element_type=F32) for rows in chunks]
    fs = []
    for q in qs:
        heads = []
        for hd in range(XA_HEADS):
            lo = hd * XA_HEAD_DIM
            k = kv_ref[:, lo:lo + XA_HEAD_DIM]
            v = kv_ref[:, width + lo:width + lo + XA_HEAD_DIM]
            s = _dot_nt(q[:, lo:lo + XA_HEAD_DIM], k) * (XA_HEAD_DIM ** -0.5)
            p = jnp.exp(s - jnp.max(s, -1, keepdims=True))
            p = p * (1.0 / jnp.sum(p, -1, keepdims=True))
            heads.append(_dot(p, v).astype(BF16))
        fs.append(jnp.dot(jnp.concatenate(heads, 1), wo_ref[...], preferred_element_type=F32))
    for rows, f in zip(chunks, fs):
        out = _layer_norm(alpha * h_ref[rows, :] + f, g_ref[...], b_ref[...])
        o_ref[rows, :] = out
        ob_ref[rows, :] = out.astype(BF16)


def _xattn(hb, h, wq, kv, wo, ln_g, ln_b, seq, mem_len, alpha, tm):
    T, D = h.shape
    width = wq.shape[1]
    ns = seq // tm
    tok = lambda i: (i, 0)
    const = lambda i: (0, 0)
    return pl.pallas_call(
        functools.partial(_xattn_kernel, alpha=alpha),
        grid=(T // tm,),
        in_specs=[pl.BlockSpec((tm, D), tok), pl.BlockSpec((tm, D), tok),
                  pl.BlockSpec((D, width), const),
                  pl.BlockSpec((mem_len, 2 * width), lambda i: (i // ns, 0)),
                  pl.BlockSpec((width, D), const),
                  pl.BlockSpec((1, D), const), pl.BlockSpec((1, D), const)],
        out_specs=[pl.BlockSpec((tm, D), tok), pl.BlockSpec((tm, D), tok)],
        out_shape=[jax.ShapeDtypeStruct((T, D), F32), jax.ShapeDtypeStruct((T, D), BF16)],
        compiler_params=_params("parallel"),
        name="cross_attn",
    )(hb, h, wq, kv, wo, ln_g.reshape(1, D), ln_b.reshape(1, D))


def _ffn_kernel(hb_ref, h_ref, wg_ref, wu_ref, wd_ref, g_ref, b_ref, o_ref, *rest, alpha):
    acc_ref = rest[-1]
    j = pl.program_id(1)

    @pl.when(j == 0)
    def _():
        acc_ref[...] = alpha * h_ref[...]

    x = hb_ref[...]
    gate = jnp.dot(x, wg_ref[...], preferred_element_type=F32)
    up = jnp.dot(x, wu_ref[...], preferred_element_type=F32)
    act = (jax.nn.silu(gate) * up).astype(BF16)
    acc_ref[...] += jnp.dot(act, wd_ref[...], preferred_element_type=F32)

    @pl.when(j == pl.num_programs(1) - 1)
    def _():
        out = _layer_norm(acc_ref[...], g_ref[...], b_ref[...])
        o_ref[...] = out
        if len(rest) == 2:
            rest[0][...] = out.astype(BF16)


def _ffn(hb, h, wg, wu, wd, ln_g, ln_b, alpha, tm, th, with_bf16_copy):
    T, D = h.shape
    H = wg.shape[1]
    tok = lambda i, j: (i, 0)
    const = lambda i, j: (0, 0)
    n_out = 2 if with_bf16_copy else 1
    return pl.pallas_call(
        functools.partial(_ffn_kernel, alpha=alpha),
        grid=(T // tm, H // th),
        in_specs=[pl.BlockSpec((tm, D), tok), pl.BlockSpec((tm, D), tok),
                  pl.BlockSpec((D, th), lambda i, j: (0, j)),
                  pl.BlockSpec((D, th), lambda i, j: (0, j)),
                  pl.BlockSpec((th, D), lambda i, j: (j, 0)),
                  pl.BlockSpec((1, D), const), pl.BlockSpec((1, D), const)],
        out_specs=[pl.BlockSpec((tm, D), tok), pl.BlockSpec((tm, D), tok)][:n_out],
        out_shape=[jax.ShapeDtypeStruct((T, D), F32), jax.ShapeDtypeStruct((T, D), BF16)][:n_out],
        scratch_shapes=[pltpu.VMEM((tm, D), F32)],
        compiler_params=_params("parallel", "arbitrary"),
        name="ffn",
    )(hb, h, wg, wu, wd, ln_g.reshape(1, D), ln_b.reshape(1, D))


def _rope_tables(pos):
    inv_freq = ROPE_THETA ** (-jnp.arange(HALF, dtype=F32) / HALF)
    ang = pos.astype(F32)[:, None] * inv_freq[None, :]
    cos = jnp.tile(jnp.cos(ang), (1, LANES // HALF))
    sin = jnp.sin(ang)
    sin = jnp.tile(jnp.concatenate([-sin, sin], 1), (1, LANES // HEAD_DIM))
    return cos, sin


def _overlap_table(seq):
    n_cmp = (seq - CMP_LEN) // CMP_STRIDE + 1
    n_sel = seq // SEL_BLOCK
    ci = np.arange(n_cmp)[:, None] * CMP_STRIDE
    sj = np.arange(n_sel)[None, :] * SEL_BLOCK
    ov = np.clip(np.minimum(ci + CMP_LEN, sj + SEL_BLOCK) - np.maximum(ci, sj), 0, None) / CMP_LEN
    full = np.zeros((LANES, seq // CMP_STRIDE), np.float32)
    full[:n_sel, :n_cmp] = ov.T
    return jnp.asarray(full, dtype=BF16)


def _split_w_in(w_in_stack, layer, width):
    hw = width // 4
    kvw = NSA_KV_HEADS * HEAD_DIM
    n_gate = 3 * (hw // HEAD_DIM)
    o = np.cumsum([0, hw, hw, 6 * kvw, n_gate, 3 * hw, 3 * hw])
    col_scale = np.ones((o[-1],), np.float32)
    for q_lo in (o[1], o[4], o[5]):
        col_scale[q_lo:q_lo + hw] = HEAD_DIM ** -0.5 * math.log2(math.e)
    w_in = _layer_bf16(w_in_stack, layer, jnp.asarray(col_scale))
    u = w_in[:, o[0]:o[1]]
    nq = w_in[:, o[1]:o[2]]
    kv = [w_in[:, o[2] + j * kvw:o[2] + (j + 1) * kvw] for j in range(6)]
    gate = w_in[:, o[3]:o[4]]
    sb = w_in[:, o[4]:o[5]]
    dil = w_in[:, o[5]:o[6]]
    per_group = 3 * NSA_REP
    gates = [jnp.pad(gate[:, g * per_group:(g + 1) * per_group], ((0, 0), (0, LANES - per_group)))
             for g in range(NSA_KV_HEADS)]
    w_rope = jnp.concatenate([nq, kv[2], kv[4], dil[:, :2 * hw]], 1)
    w_plain = jnp.concatenate([sb, dil[:, 2 * hw:], kv[0], kv[1], kv[3], kv[5]] + gates, 1)
    return u, w_rope, w_plain


R_NQ, R_KSLC, R_KWIN, R_DILQ, R_DILK = 0, 4, 5, 6, 10
P_SBQ, P_SBK, P_SBV, P_DILV, P_KCMP, P_VSLC, P_VWIN, P_GATE = 0, 4, 8, 12, 16, 18, 19, 20


def _hybrid_mixer(hb, h, nb, seq, w_in_stack, layer, s5_params, cmp_pe, cmp_w1, cmp_w2, norm_g, w_out,
                  ln_g, ln_b, tables, alpha, tm):
    cos, sin, cos_c, sin_c, overlap = tables
    width = w_out.shape[0]
    n_pairs = width // 4 // LANES
    w_u, w_rope, w_plain = _split_w_in(w_in_stack, layer, width)
    r_arr = _proj(hb, w_rope, min(2 * tm, seq), w_rope.shape[1], seq=seq, rope=(cos, sin))
    p_arr = _proj(hb, w_plain, tm, w_plain.shape[1])
    y_a = _s5(hb, w_u, nb, seq, *s5_params)
    cmp_kv = _compress(p_arr, P_KCMP, nb, seq, cmp_pe, cmp_w1, cmp_w2, cos_c, sin_c)
    y_b = _nsa(r_arr, p_arr, cmp_kv, overlap, nb, seq,
               dict(q=R_NQ, gate=P_GATE, k_slc=R_KSLC, v_slc=P_VSLC, k_win=R_KWIN, v_win=P_VWIN))
    y_c = _stick_breaking(p_arr, nb, seq, P_SBQ, P_SBK, P_SBV, n_pairs)
    y_d = _dilated(r_arr, R_DILQ, r_arr, R_DILK, p_arr, P_DILV, nb, seq, n_pairs)
    return _mix_out(y_a, y_b, y_c, y_d, norm_g, w_out, h, ln_g, ln_b, alpha, tm)


def kernel(x, mem, ln_in_g, ln_in_b, w_in, s5_lambda_re, s5_lambda_im, s5_log_dt, s5_b_re, s5_b_im, s5_c_re, s5_c_im, s5_d, s5_w_glu, s5_b_glu, nsa_cmp_pe, nsa_cmp_w1, nsa_cmp_w2, mix_norm_g, w_out, ln1_g, ln1_b, xa_wq, xa_wkv, xa_wo, ln2_g, ln2_b, ffn_w_gate, ffn_w_up, ffn_w_down, ln3_g, ln3_b):
    nb, seq, d_model = x.shape
    mem_len = mem.shape[1]
    depth = w_in.shape[0]
    alpha = (2 * depth) ** 0.25
    tm = min(512, seq)
    pos = jnp.arange(seq)
    cos, sin = _rope_tables(pos)
    nblk = seq // CMP_STRIDE
    cos_c, sin_c = _rope_tables(jnp.arange(nblk) * CMP_STRIDE + CMP_LEN - 1)
    tables = (cos, sin, cos_c, sin_c, _overlap_table(seq))
    mem_b = mem.reshape(nb * mem_len, d_model).astype(BF16)

    h, hb = _ln_in(x.reshape(nb * seq, d_model), ln_in_g, ln_in_b, tm)
    for l in range(depth):
        s5_params = (s5_lambda_re[l], s5_lambda_im[l], s5_log_dt[l], s5_b_re[l], s5_b_im[l],
                     s5_c_re[l], s5_c_im[l], s5_d[l], s5_w_glu[l], s5_b_glu[l])
        h, hb = _hybrid_mixer(hb, h, nb, seq, w_in, l, s5_params, nsa_cmp_pe[l], nsa_cmp_w1[l],
                              nsa_cmp_w2[l], mix_norm_g[l], _layer_bf16(w_out, l), ln1_g[l], ln1_b[l],
                              tables, alpha, tm)
        kv = _proj(mem_b, _layer_bf16(xa_wkv, l), min(512, nb * mem_len), xa_wkv.shape[2] // 2)
        h, hb = _xattn(hb, h, _layer_bf16(xa_wq, l), kv, _layer_bf16(xa_wo, l), ln2_g[l], ln2_b[l],
                       seq, mem_len, alpha, tm)
        last = l == depth - 1
        outs = _ffn(hb, h, _layer_bf16(ffn_w_gate, l), _layer_bf16(ffn_w_up, l),
                    _layer_bf16(ffn_w_down, l), ln3_g[l], ln3_b[l], alpha, tm, 512, not last)
        h, hb = (outs[0], None) if last else outs
    return h.reshape(nb, seq, d_model)
```

```python
import functools
import math

import numpy as np
import jax
import jax.numpy as jnp
from jax import lax
from jax.experimental import pallas as pl
from jax.experimental.pallas import tpu as pltpu

F32 = jnp.float32
BF16 = jnp.bfloat16

LANES = 128
MXU_WIDTH = 256
CAST_ROWS = 512
VMEM_LIMIT = 56 * 1024 * 1024

HEAD_DIM = 64
HALF = HEAD_DIM // 2
ROPE_THETA = 10000.0
LN_EPS = 1e-5
RMS_EPS = 1e-6
SSM_CH = 16
SSM_STATE = 64
NSA_KV_HEADS = 2
NSA_REP = 4
CMP_LEN = 32
CMP_STRIDE = 16
CMP_HIDDEN = 128
SEL_BLOCK = 64
SEL_TOPN = 8
NSA_WINDOW = 512
DIL_CONFIGS = ((128, 1), (512, 4), (2048, 16))
XA_HEADS = 4
XA_HEAD_DIM = 128
Q_TILE = 128
SB_TILE = 256
SB_PAIRS = 4
NSA_TILE = 256
NEG_BIG = -1e30


def _params(*sem):
    return pltpu.CompilerParams(dimension_semantics=sem, vmem_limit_bytes=VMEM_LIMIT)


def _dot(a, b):
    return jnp.dot(a.astype(BF16), b.astype(BF16), preferred_element_type=F32)


def _dot_nt(a, b):
    return lax.dot_general(a.astype(BF16), b.astype(BF16), (((1,), (1,)), ((), ())),
                           preferred_element_type=F32)


def _layer_norm(x, g, b):
    mu = jnp.mean(x, -1, keepdims=True)
    xc = x - mu
    var = jnp.mean(xc * xc, -1, keepdims=True)
    return xc * lax.rsqrt(var + LN_EPS) * g + b


def _lane_iota(shape):
    return lax.broadcasted_iota(jnp.int32, shape, len(shape) - 1)


def _row_iota(shape):
    return lax.broadcasted_iota(jnp.int32, shape, len(shape) - 2)


def _swap_halves(x):
    return pltpu.roll(x, HEAD_DIM, axis=x.ndim - 1)


def _ln_in_kernel(x_ref, g_ref, b_ref, h_ref, hb_ref):
    y = _layer_norm(x_ref[...], g_ref[...], b_ref[...])
    h_ref[...] = y
    hb_ref[...] = y.astype(BF16)


def _ln_in(x2, g, b, tm):
    T, D = x2.shape
    return pl.pallas_call(
        _ln_in_kernel,
        grid=(T // tm,),
        in_specs=[pl.BlockSpec((tm, D), lambda i: (i, 0)),
                  pl.BlockSpec((1, D), lambda i: (0, 0)),
                  pl.BlockSpec((1, D), lambda i: (0, 0))],
        out_specs=[pl.BlockSpec((tm, D), lambda i: (i, 0)),
                   pl.BlockSpec((tm, D), lambda i: (i, 0))],
        out_shape=[jax.ShapeDtypeStruct((T, D), F32), jax.ShapeDtypeStruct((T, D), BF16)],
        compiler_params=_params("parallel"),
        name="ln_in",
    )(x2, g.reshape(1, D), b.reshape(1, D))


def _cast_kernel(w_ref, o_ref):
    o_ref[...] = w_ref[...].astype(BF16)


def _cast_scaled_kernel(w_ref, s_ref, o_ref):
    o_ref[...] = (w_ref[...] * s_ref[...]).astype(BF16)


def _layer_bf16(w_stack, layer, col_scale=None):
    _, K, N = w_stack.shape
    tk = min(CAST_ROWS, K)
    in_specs = [pl.BlockSpec((None, tk, N), lambda i: (layer, i, 0))]
    args = [w_stack]
    kern = _cast_kernel
    if col_scale is not None:
        in_specs.append(pl.BlockSpec((1, N), lambda i: (0, 0)))
        args.append(col_scale.reshape(1, N))
        kern = _cast_scaled_kernel
    return pl.pallas_call(
        kern, grid=(K // tk,), in_specs=in_specs,
        out_specs=pl.BlockSpec((tk, N), lambda i: (i, 0)),
        out_shape=jax.ShapeDtypeStruct((K, N), BF16),
        compiler_params=_params("parallel"), name="cast_bf16",
    )(*args)


def _row_chunks(n, parts=2):
    step = n // parts
    return [slice(i * step, (i + 1) * step) for i in range(parts)]


def _column_chunks(n):
    return [(lo, min(lo + MXU_WIDTH, n)) for lo in range(0, n, MXU_WIDTH)]


def _proj_kernel(a_ref, w_ref, o_ref):
    a = a_ref[...]
    for lo, hi in _column_chunks(o_ref.shape[1]):
        o_ref[:, lo:hi] = jnp.dot(a, w_ref[:, lo:hi], preferred_element_type=F32)


def _proj_rope_kernel(a_ref, w_ref, cos_ref, sin_ref, o_ref):
    a = a_ref[...]
    cos = cos_ref[...]
    sin = sin_ref[...]
    first = (_lane_iota((1, LANES)) % HEAD_DIM) < HALF
    for lo, hi in _column_chunks(o_ref.shape[1]):
        acc = jnp.dot(a, w_ref[:, lo:hi], preferred_element_type=F32)
        for c in range((hi - lo) // LANES):
            x = acc[:, c * LANES:(c + 1) * LANES]
            partner = jnp.where(first, pltpu.roll(x, LANES - HALF, axis=1), pltpu.roll(x, HALF, axis=1))
            o_ref[:, lo + c * LANES:lo + (c + 1) * LANES] = x * cos + partner * sin


def _proj(a, w, tm, tn, seq=None, rope=None):
    M, K = a.shape
    N = w.shape[1]
    nm, nn = M // tm, N // tn
    in_specs = [pl.BlockSpec((tm, K), lambda j, i: (i, 0)),
                pl.BlockSpec((K, tn), lambda j, i: (0, j))]
    args = [a, w]
    kern = _proj_kernel
    if rope is not None:
        ns = seq // tm
        in_specs += [pl.BlockSpec((tm, LANES), lambda j, i: (i % ns, 0))] * 2
        args += list(rope)
        kern = _proj_rope_kernel
    return pl.pallas_call(
        kern, grid=(nn, nm), in_specs=in_specs,
        out_specs=pl.BlockSpec((tm, tn), lambda j, i: (i, j)),
        out_shape=jax.ShapeDtypeStruct((M, N), F32),
        compiler_params=_params("parallel", "parallel"), name="proj",
    )(*args)


def _s5_kernel(h_ref, wu_ref, bb_ref, cc_ref, are_ref, aim_ref, d_ref, wg_ref, bg_ref, y_ref,
               buf_ref, st_ref, tm_ref, *, nb, ts):
    n_chunks, cw, sw2 = bb_ref.shape
    sw = sw2 // 2
    assert cw == tm_ref.shape[2]

    @pl.when(pl.program_id(0) == 0)
    def _():
        st_ref[...] = jnp.zeros_like(st_ref)

    u_bt = jnp.dot(h_ref[...].reshape(nb * ts, h_ref.shape[2]), wu_ref[...],
                   preferred_element_type=F32)
    for b in range(nb):
        for c in range(n_chunks):
            tm_ref[c, pl.ds(b, ts, stride=nb), :] = u_bt[b * ts:(b + 1) * ts, c * cw:(c + 1) * cw]
    u = jnp.concatenate([tm_ref[c] for c in range(n_chunks)], 1)
    ys = []
    for c in range(n_chunks):
        re0, im0 = c * sw2, c * sw2 + sw
        buf_ref[:, re0:re0 + sw2] = jnp.dot(u[:, c * cw:(c + 1) * cw].astype(BF16), bb_ref[c],
                                            preferred_element_type=F32)
        a_re = jnp.broadcast_to(are_ref[:, c * sw:(c + 1) * sw], (nb, sw))
        a_im = jnp.broadcast_to(aim_ref[:, c * sw:(c + 1) * sw], (nb, sw))

        def step(t, carry, re0=re0, im0=im0, a_re=a_re, a_im=a_im):
            x_re, x_im = carry
            r = pl.multiple_of(t * nb, nb)
            n_re = a_re * x_re - a_im * x_im + buf_ref[pl.ds(r, nb), re0:re0 + sw]
            n_im = a_re * x_im + a_im * x_re + buf_ref[pl.ds(r, nb), im0:im0 + sw]
            buf_ref[pl.ds(r, nb), re0:re0 + sw] = n_re
            buf_ref[pl.ds(r, nb), im0:im0 + sw] = n_im
            return n_re, n_im

        x_re, x_im = lax.fori_loop(0, ts, step, (st_ref[:, re0:re0 + sw], st_ref[:, im0:im0 + sw]),
                                   unroll=True)
        st_ref[:, re0:re0 + sw] = x_re
        st_ref[:, im0:im0 + sw] = x_im
        ys.append(jnp.dot(buf_ref[:, re0:re0 + sw2].astype(BF16), cc_ref[c], preferred_element_type=F32))

    y = jnp.concatenate(ys, 1) + d_ref[...] * u
    g = jax.nn.gelu(y)
    z = jnp.dot(g.astype(BF16), wg_ref[...], preferred_element_type=F32) + bg_ref[...]
    out = g * jax.nn.sigmoid(z)
    for c in range(n_chunks):
        tm_ref[c] = out[:, c * cw:(c + 1) * cw]
    for b in range(nb):
        for c in range(n_chunks):
            y_ref[b, :, c * cw:(c + 1) * cw] = tm_ref[c, pl.ds(b, ts, stride=nb), :]


def _s5(hb, w_u, nb, seq, lam_re, lam_im, log_dt, b_re, b_im, c_re, c_im, d_skip, w_glu, b_glu, ts=64):
    D, W = w_u.shape
    G, P = lam_re.shape
    C = SSM_CH
    lr = jnp.minimum(lam_re, -1e-4)
    li = lam_im
    dt = jnp.exp(log_dt)[:, None]
    mag = jnp.exp(lr * dt)
    a_re = mag * jnp.cos(li * dt)
    a_im = mag * jnp.sin(li * dt)
    den = lr * lr + li * li
    z_re = ((a_re - 1.0) * lr + a_im * li) / den
    z_im = (a_im * lr - (a_re - 1.0) * li) / den
    bb_re = z_re[..., None] * b_re - z_im[..., None] * b_im
    bb_im = z_re[..., None] * b_im + z_im[..., None] * b_re
    gc = LANES // C
    nc = G // gc
    eye = jnp.eye(gc, dtype=F32)

    def block_diag_in(m):
        return jnp.einsum('ngpc,gh->ngchp', m.reshape(nc, gc, P, C), eye).reshape(nc, gc * C, gc * P)

    def block_diag_out(m):
        return jnp.einsum('ngcp,gh->ngphc', m.reshape(nc, gc, C, P), eye).reshape(nc, gc * P, gc * C)

    bb = jnp.concatenate([block_diag_in(bb_re), block_diag_in(bb_im)], 2).astype(BF16)
    cc = jnp.concatenate([block_diag_out(c_re), -block_diag_out(c_im)], 1).astype(BF16)
    ns = G * P
    rows = ts * nb
    kern = functools.partial(_s5_kernel, nb=nb, ts=ts)
    const = lambda i: (0, 0)
    const3 = lambda i: (0, 0, 0)
    y = pl.pallas_call(
        kern,
        grid=(seq // ts,),
        in_specs=[pl.BlockSpec((nb, ts, D), lambda i: (0, i, 0)),
                  pl.BlockSpec((D, W), const),
                  pl.BlockSpec(bb.shape, const3),
                  pl.BlockSpec(cc.shape, const3),
                  pl.BlockSpec((1, ns), const),
                  pl.BlockSpec((1, ns), const),
                  pl.BlockSpec((1, W), const),
                  pl.BlockSpec((W, W), const),
                  pl.BlockSpec((1, W), const)],
        out_specs=pl.BlockSpec((nb, ts, W), lambda i: (0, i, 0)),
        out_shape=jax.ShapeDtypeStruct((nb, seq, W), F32),
        scratch_shapes=[pltpu.VMEM((rows, 2 * ns), F32), pltpu.VMEM((nb, 2 * ns), F32),
                        pltpu.VMEM((nc, rows, LANES), F32)],
        compiler_params=_params("arbitrary"),
        name="s5",
    )(hb.reshape(nb, seq, D), w_u, bb, cc, a_re.reshape(1, ns), a_im.reshape(1, ns),
      d_skip.reshape(1, W), w_glu.astype(BF16), b_glu.reshape(1, W))
    return y.reshape(nb * seq, W)


def _stack_pair(q):
    first = _lane_iota((1, LANES)) < HEAD_DIM
    return jnp.concatenate([jnp.where(first, q, 0.0), jnp.where(first, 0.0, q)], 0)


def _unstack_pair(x):
    t = x.shape[0] // 2
    first = _lane_iota((1, LANES)) < HEAD_DIM
    return jnp.where(first, x[:t], x[t:])


def _sb_kernel(q_ref, k_ref, v_ref, o_ref, acc_ref, tail_ref):
    tq = q_ref.shape[0]
    qi = pl.program_id(2)
    n_pairs = q_ref.shape[1] // LANES
    groups = [slice(p * LANES, (p + 1) * LANES) for p in range(n_pairs)]
    qs = [_stack_pair(q_ref[:, g]).astype(BF16) for g in groups]
    tri = (_row_iota((tq, tq)) > _lane_iota((tq, tq))).astype(BF16)
    diag = _lane_iota((2 * tq, tq)) < (_row_iota((2 * tq, tq)) % tq)

    def tile(kj, diagonal):
        r = pl.multiple_of(kj * tq, tq)
        zs = [_dot_nt(q, k_ref[pl.ds(r, tq), g]) for q, g in zip(qs, groups)]
        stage = []
        for z in zs:
            pos = jnp.maximum(z, 0.0)
            neg = z - pos
            log_term = jnp.log2(1.0 + jnp.exp2(neg - pos))
            sp = pos + log_term
            if diagonal:
                sp = jnp.where(diag, sp, 0.0)
            stage.append((neg - log_term, jnp.sum(sp, -1, keepdims=True), _dot(sp, tri)))
        for p, ((log_beta, row_sum, after), g) in enumerate(zip(stage, groups)):
            if diagonal:
                w = jnp.where(diag, jnp.exp2(log_beta - after), 0.0)
                acc_ref[p] = _dot(w, v_ref[pl.ds(r, tq), g])
                tail_ref[p] = row_sum
            else:
                w = jnp.exp2(log_beta - (after + tail_ref[p]))
                acc_ref[p] += _dot(w, v_ref[pl.ds(r, tq), g])
                tail_ref[p] += row_sum

    tile(qi, True)

    def two_tiles(i, _):
        tile(qi - 1 - 2 * i, False)
        tile(qi - 2 - 2 * i, False)
        return 0

    def last_tile(i, _):
        tile(0, False)
        return 0

    lax.fori_loop(0, qi // 2, two_tiles, 0)
    lax.fori_loop(0, qi % 2, last_tile, 0)
    for p, g in enumerate(groups):
        o_ref[:, g] = _unstack_pair(acc_ref[p])


def _stick_breaking(qkv, nb, seq, q_col, k_col, v_col, n_pairs):
    tq = min(SB_TILE, seq)
    nq = seq // tq
    per = SB_PAIRS
    w = per * LANES
    assert n_pairs % per == 0 and q_col % per == 0 and k_col % per == 0 and v_col % per == 0
    return pl.pallas_call(
        _sb_kernel,
        grid=(nb, n_pairs // per, nq),
        in_specs=[pl.BlockSpec((tq, w), lambda b, p, i: (b * nq + i, q_col // per + p)),
                  pl.BlockSpec((seq, w), lambda b, p, i: (b, k_col // per + p)),
                  pl.BlockSpec((seq, w), lambda b, p, i: (b, v_col // per + p))],
        out_specs=pl.BlockSpec((tq, w), lambda b, p, i: (b * nq + i, p)),
        out_shape=jax.ShapeDtypeStruct((nb * seq, n_pairs * LANES), F32),
        scratch_shapes=[pltpu.VMEM((per, 2 * tq, LANES), F32), pltpu.VMEM((per, 2 * tq, 1), F32)],
        compiler_params=_params("parallel", "parallel", "arbitrary"),
        name="stick_breaking",
    )(qkv, qkv, qkv)


def _dil_kernel(q_ref, k_ref, v_ref, o_ref, m_ref, l_ref, a_ref):
    seq = q_ref.shape[0]
    tq = Q_TILE
    first = _lane_iota((1, LANES)) < HEAD_DIM

    q_in = _row_iota((2 * tq, 1)) % tq

    def band_bias(wd, n_tiles):
        if n_tiles > 1:
            lag = q_in - (_lane_iota((1, 2 * tq)) - tq)
            return jnp.where((lag >= 0) & (lag <= wd), 0.0, NEG_BIG)
        return jnp.where(_lane_iota((1, tq)) <= q_in, 0.0, NEG_BIG)

    def tile(c, dil, band, n_tiles, r, i):
        base = r + dil * tq * i

        def rows(ref, start):
            if dil == 1:
                return ref[pl.ds(pl.multiple_of(start, tq), tq), :]
            return ref[pl.ds(start, tq, stride=dil), :]

        qs = _stack_pair(rows(q_ref, base)).astype(BF16)
        if n_tiles > 1:
            prev = r + dil * tq * jnp.maximum(i - 1, 0)
            kk = jnp.concatenate([rows(k_ref, prev), rows(k_ref, base)], 0)
            vv = jnp.concatenate([rows(v_ref, prev), rows(v_ref, base)], 0)
            no_prev = jnp.where((i == 0) & (_lane_iota((1, 2 * tq)) < tq), NEG_BIG, 0.0)
            bias = band + no_prev
        else:
            kk = rows(k_ref, base)
            vv = rows(v_ref, base)
            bias = band
        s = _dot_nt(qs, kk) + bias
        m = jnp.max(s, -1, keepdims=True)
        p = jnp.exp2(s - m)
        l = jnp.sum(p, -1, keepdims=True)
        acc = _dot(p, vv)
        m2 = jnp.where(first, m[:tq], m[tq:])
        l2 = jnp.where(first, l[:tq], l[tq:])
        a2 = jnp.where(first, acc[:tq], acc[tq:])
        if dil == 1:
            sl = pl.ds(pl.multiple_of(base, tq), tq)
        else:
            sl = pl.ds(base, tq, stride=dil)
        m_ref[c, sl, :] = m2
        l_ref[c, sl, :] = l2
        a_ref[c, sl, :] = a2

    for c, (window, dil) in enumerate(DIL_CONFIGS):
        wd = window // dil
        n_tiles = seq // dil // tq

        band = band_bias(wd, n_tiles)

        def per_tile(n, _, c=c, dil=dil, band=band, n_tiles=n_tiles):
            tile(c, dil, band, n_tiles, n // n_tiles, n % n_tiles)
            return 0

        lax.fori_loop(0, dil * n_tiles, per_tile, 0, unroll=16)

    def combine(i, _):
        sl = pl.ds(pl.multiple_of(i * tq, tq), tq)
        m0, m1, m2 = m_ref[0, sl, :], m_ref[1, sl, :], m_ref[2, sl, :]
        mx = jnp.maximum(jnp.maximum(m0, m1), m2)
        e0, e1, e2 = jnp.exp2(m0 - mx), jnp.exp2(m1 - mx), jnp.exp2(m2 - mx)
        num = e0 * a_ref[0, sl, :] + e1 * a_ref[1, sl, :] + e2 * a_ref[2, sl, :]
        den = e0 * l_ref[0, sl, :] + e1 * l_ref[1, sl, :] + e2 * l_ref[2, sl, :]
        o_ref[sl, :] = num / den
        return 0

    lax.fori_loop(0, seq // tq, combine, 0)


def _dilated(q_arr, q_col, k_arr, k_col, v_arr, v_col, nb, seq, n_pairs):
    return pl.pallas_call(
        _dil_kernel,
        grid=(nb, n_pairs),
        in_specs=[pl.BlockSpec((seq, LANES), lambda b, p: (b, q_col + p)),
                  pl.BlockSpec((seq, LANES), lambda b, p: (b, k_col + p)),
                  pl.BlockSpec((seq, LANES), lambda b, p: (b, v_col + p))],
        out_specs=pl.BlockSpec((seq, LANES), lambda b, p: (b, p)),
        out_shape=jax.ShapeDtypeStruct((nb * seq, n_pairs * LANES), F32),
        scratch_shapes=[pltpu.VMEM((3, seq, LANES), F32)] * 3,
        compiler_params=_params("parallel", "parallel"),
        name="dilated",
    )(q_arr, k_arr, v_arr)


def _cmp_kernel(t_ref, pe_ref, w1a_ref, w1b_ref, w2_ref, cos_ref, sin_ref, o_ref):
    nblk = t_ref.shape[0] // CMP_STRIDE
    j = pl.program_id(1)
    out = jnp.zeros((nblk, LANES), F32)
    for g in range(NSA_KV_HEADS):
        p1 = jnp.zeros((nblk, CMP_HIDDEN), F32)
        p2 = jnp.zeros((nblk, CMP_HIDDEN), F32)
        for l in range(CMP_STRIDE):
            x = t_ref[pl.ds(l, nblk, stride=CMP_STRIDE), :]
            p1 = p1 + _dot(x + pe_ref[0, l:l + 1, :], w1a_ref[0, g, l])
            p2 = p2 + _dot(x + pe_ref[0, CMP_STRIDE + l:CMP_STRIDE + l + 1, :], w1b_ref[0, g, l])
        hidden = p1 + pltpu.roll(p2, nblk - 1, axis=0)
        out = out + _dot(jax.nn.gelu(hidden), w2_ref[0, g])
    first = (_lane_iota((1, LANES)) % HEAD_DIM) < HALF
    partner = jnp.where(first, pltpu.roll(out, LANES - HALF, axis=1), pltpu.roll(out, HALF, axis=1))
    roped = out * cos_ref[...] + partner * sin_ref[...]
    o_ref[0] = jnp.where(j == 0, roped, out)


def _compress(p_arr, col0, nb, seq, pe, w1, w2, cos_c, sin_c):
    nblk = seq // CMP_STRIDE
    G = NSA_KV_HEADS
    pe2 = jnp.tile(pe, (1, 1, G))
    w1r = w1.reshape(2, CMP_LEN, HEAD_DIM, CMP_HIDDEN)
    w1e = jnp.zeros((2, G, CMP_LEN, LANES, CMP_HIDDEN), F32)
    w2e = jnp.zeros((2, G, CMP_HIDDEN, LANES), F32)
    for g in range(G):
        w1e = w1e.at[:, g, :, g * HEAD_DIM:(g + 1) * HEAD_DIM, :].set(w1r)
        w2e = w2e.at[:, g, :, g * HEAD_DIM:(g + 1) * HEAD_DIM].set(w2)
    w1e = w1e.astype(BF16)
    w2e = w2e.astype(BF16)
    return pl.pallas_call(
        _cmp_kernel,
        grid=(nb, 2),
        in_specs=[pl.BlockSpec((seq, LANES), lambda b, j: (b, col0 + j)),
                  pl.BlockSpec((1, CMP_LEN, LANES), lambda b, j: (j, 0, 0)),
                  pl.BlockSpec((1, G, CMP_STRIDE, LANES, CMP_HIDDEN), lambda b, j: (j, 0, 0, 0, 0)),
                  pl.BlockSpec((1, G, CMP_STRIDE, LANES, CMP_HIDDEN), lambda b, j: (j, 0, 1, 0, 0)),
                  pl.BlockSpec((1, G, CMP_HIDDEN, LANES), lambda b, j: (j, 0, 0, 0)),
                  pl.BlockSpec((nblk, LANES), lambda b, j: (0, 0)),
                  pl.BlockSpec((nblk, LANES), lambda b, j: (0, 0))],
        out_specs=pl.BlockSpec((1, nblk, LANES), lambda b, j: (j, b, 0)),
        out_shape=jax.ShapeDtypeStruct((2, nb * nblk, LANES), F32),
        compiler_params=_params("parallel", "parallel"),
        name="nsa_compress",
    )(p_arr, pe2, w1e, w1e, w2e, cos_c, sin_c)


def _nsa_kernel(q_ref, gate_ref, kc_ref, vc_ref, ks_ref, vs_ref, kw_ref, vw_ref, ovt_ref, o_ref,
                as_ref, aw_ref):
    tq = q_ref.shape[0]
    R, G = NSA_REP, NSA_KV_HEADS
    seq = ks_ref.shape[0]
    n_sel = seq // SEL_BLOCK
    n_cmp = (seq - CMP_LEN) // CMP_STRIDE + 1
    nblk = kc_ref.shape[1]
    top_n = min(SEL_TOPN, n_sel)
    qi = pl.program_id(1)
    lane = _lane_iota((1, LANES))
    first = lane < HEAD_DIM
    mine = [first, jnp.logical_not(first)]
    t_row = qi * tq + _row_iota((tq, 1))

    qs = []
    for g in range(G):
        parts = []
        for r in range(R):
            h = g * R + r
            x = q_ref[:, (h // 2) * LANES:(h // 2 + 1) * LANES]
            parts.append(jnp.where(mine[g], x if h % 2 == g else _swap_halves(x), 0.0))
        qs.append(jnp.concatenate(parts, 0).astype(BF16))

    c_idx = _lane_iota((tq, nblk))
    visible = (c_idx < n_cmp) & (c_idx * CMP_STRIDE + (CMP_LEN - 1) <= qi * tq + _row_iota((tq, nblk)))
    c_bias = jnp.where(visible, 0.0, NEG_BIG)
    kc, vc = kc_ref[0], vc_ref[0]
    vc_swapped = _swap_halves(vc)
    s_cs = [_dot_nt(q, kc).reshape(R, tq, nblk) + c_bias for q in qs]
    p_cs, o_cs = [], []
    for g, s_c in enumerate(s_cs):
        m_c = jnp.max(s_c, -1, keepdims=True)
        m_c = jnp.where(m_c > 0.5 * NEG_BIG, m_c, 0.0)
        p_c = jnp.exp2(s_c - m_c)
        p_c = p_c * (1.0 / jnp.maximum(jnp.sum(p_c, -1, keepdims=True), 1e-30))
        p_cs.append(p_c)
        o_cs.append(_dot(p_c.reshape(R * tq, nblk), jnp.where(mine[g], vc, vc_swapped)).reshape(R, tq, LANES))

    n_idx = _row_iota((n_sel, tq))
    t_q = qi * tq + _lane_iota((n_sel, tq))
    cur = t_q // SEL_BLOCK
    future = n_idx * SEL_BLOCK > t_q
    forced = (n_idx == 0) | (n_idx == cur) | (n_idx == cur - 1)
    sels = []
    for p_c in p_cs:
        psum = jnp.sum(p_c, 0)
        p_hi = psum.astype(BF16)
        p_lo = (psum - p_hi.astype(F32)).astype(BF16)
        imp = (_dot_nt(ovt_ref[...], p_hi) + _dot_nt(ovt_ref[...], p_lo))[:n_sel]
        imp = jnp.where(forced, 1e9, jnp.where(future, -1.0, imp))
        rank = jnp.zeros((n_sel, tq), F32)
        for m in range(n_sel):
            row = imp[m:m + 1, :]
            earlier = (n_idx > m).astype(F32)
            rank = rank + jnp.where(row > imp, 1.0, jnp.where(row == imp, earlier, 0.0))
        sel_t = jnp.where((rank < top_n) & (imp > -0.5), 1.0, 0.0)
        sel_t = jnp.concatenate([sel_t, jnp.zeros((LANES - n_sel, tq), F32)], 0)
        sels.append(sel_t.T.astype(BF16))

    k_col = _lane_iota((tq, tq))
    blk_row = _row_iota((LANES, tq))
    blk_of_key = _lane_iota((LANES, tq)) // SEL_BLOCK

    def sel_biases(j, causal):
        expand = (blk_row == (j * (tq // SEL_BLOCK) + blk_of_key)).astype(BF16)
        out = []
        for sel in sels:
            chosen = jnp.dot(sel, expand, preferred_element_type=F32)
            bias = (1.0 - chosen) * NEG_BIG
            if causal:
                bias = jnp.where((j * tq + k_col) <= t_row, bias, NEG_BIG)
            out.append(bias)
        return out

    def win_bias(j):
        diff = t_row - (j * tq + k_col)
        return jnp.where((diff >= 0) & (diff < NSA_WINDOW), 0.0, NEG_BIG)

    def tile_of(ref, j):
        return ref[pl.ds(pl.multiple_of(j * tq, tq), tq), :]

    def scores(k_ref, j, biases):
        kt = tile_of(k_ref, j).astype(BF16)
        return [_dot_nt(q, kt).reshape(R, tq, tq) + b for q, b in zip(qs, biases)]

    def update(m_runs, acc_ref, s_groups, v_ref, j):
        v = tile_of(v_ref, j)
        m_news = []
        for g, (m_run, s) in enumerate(zip(m_runs, s_groups)):
            m_new = jnp.maximum(m_run, jnp.max(s, -1, keepdims=True))
            alpha = jnp.exp2(m_run - m_new)
            p = jnp.exp2(s - m_new)
            pv = _dot(p.reshape(R * tq, tq), jnp.where(mine[g], v, 1.0))
            acc_ref[g] = alpha * acc_ref[g] + pv.reshape(R, tq, LANES)
            m_news.append(m_new)
        return tuple(m_news)

    def normalise(acc_ref):
        out = []
        for g in range(G):
            acc = acc_ref[g]
            swapped = _swap_halves(acc)
            out.append(jnp.where(mine[g], acc, swapped) / jnp.where(mine[g], swapped, acc))
        return out

    as_ref[...] = jnp.zeros(as_ref.shape, F32)
    aw_ref[...] = jnp.zeros(aw_ref.shape, F32)
    m_init = tuple(jnp.full((R, tq, 1), NEG_BIG, F32) for _ in range(G))
    j_win = jnp.maximum(qi - NSA_WINDOW // tq, 0)

    def selected_only(j, m_sel):
        return update(m_sel, as_ref, scores(ks_ref, j, sel_biases(j, False)), vs_ref, j)

    def selected_and_window(j, ms):
        b_w = win_bias(j)
        s_s, s_w = scores(ks_ref, j, sel_biases(j, True)), scores(kw_ref, j, [b_w] * G)
        return update(ms[0], as_ref, s_s, vs_ref, j), update(ms[1], aw_ref, s_w, vw_ref, j)

    m_sel = lax.fori_loop(0, j_win, selected_only, m_init)
    lax.fori_loop(j_win, qi + 1, selected_and_window, (m_sel, m_init))
    o_ss, o_ws = normalise(as_ref), normalise(aw_ref)

    gate = jax.nn.sigmoid(gate_ref[...])
    outs = []
    for g in range(G):
        for r in range(R):
            c0 = g * LANES + 3 * r
            outs.append(gate[:, c0:c0 + 1] * o_cs[g][r] + gate[:, c0 + 1:c0 + 2] * o_ss[g][r]
                        + gate[:, c0 + 2:c0 + 3] * o_ws[g][r])
    for pair in range(G * R // 2):
        o_ref[:, pair * LANES:(pair + 1) * LANES] = jnp.where(first, outs[2 * pair], outs[2 * pair + 1])


def _nsa(r_arr, p_arr, cmp_kv, overlap, nb, seq, cols):
    tq = min(NSA_TILE, seq)
    nq = seq // tq
    nblk = seq // CMP_STRIDE
    G = NSA_KV_HEADS
    qw = G * NSA_REP * HEAD_DIM
    gw = G * LANES
    assert (cols['q'] * LANES) % qw == 0 and (cols['gate'] * LANES) % gw == 0
    return pl.pallas_call(
        _nsa_kernel,
        grid=(nb, nq),
        in_specs=[pl.BlockSpec((tq, qw), lambda b, i: (b * nq + i, cols['q'] * LANES // qw)),
                  pl.BlockSpec((tq, gw), lambda b, i: (b * nq + i, cols['gate'] * LANES // gw)),
                  pl.BlockSpec((1, nblk, LANES), lambda b, i: (0, b, 0)),
                  pl.BlockSpec((1, nblk, LANES), lambda b, i: (1, b, 0)),
                  pl.BlockSpec((seq, LANES), lambda b, i: (b, cols['k_slc'])),
                  pl.BlockSpec((seq, LANES), lambda b, i: (b, cols['v_slc'])),
                  pl.BlockSpec((seq, LANES), lambda b, i: (b, cols['k_win'])),
                  pl.BlockSpec((seq, LANES), lambda b, i: (b, cols['v_win'])),
                  pl.BlockSpec((LANES, nblk), lambda b, i: (0, 0))],
        out_specs=pl.BlockSpec((tq, qw), lambda b, i: (b * nq + i, 0)),
        out_shape=jax.ShapeDtypeStruct((nb * seq, qw), F32),
        scratch_shapes=[pltpu.VMEM((G, NSA_REP, tq, LANES), F32)] * 2,
        compiler_params=_params("parallel", "arbitrary"),
        name="nsa",
    )(r_arr, p_arr, cmp_kv, cmp_kv, r_arr, p_arr, r_arr, p_arr, overlap)


def _mix_out_kernel(ya_ref, yb_ref, yc_ref, yd_ref, ng_ref, w_ref, h_ref, g_ref, b_ref,
                    o_ref, ob_ref, *, alpha):
    fs = []
    for rows in _row_chunks(o_ref.shape[0]):
        ys = []
        for n, y_ref in enumerate((ya_ref, yb_ref, yc_ref, yd_ref)):
            y = y_ref[rows, :]
            y = y * lax.rsqrt(jnp.mean(y * y, -1, keepdims=True) + RMS_EPS) * ng_ref[n:n + 1, :]
            ys.append(y.astype(BF16))
        fs.append(jnp.dot(jnp.concatenate(ys, 1), w_ref[...], preferred_element_type=F32))
    for rows, f in zip(_row_chunks(o_ref.shape[0]), fs):
        out = _layer_norm(alpha * h_ref[rows, :] + f, g_ref[...], b_ref[...])
        o_ref[rows, :] = out
        ob_ref[rows, :] = out.astype(BF16)


def _mix_out(ya, yb, yc, yd, norm_g, w_out, h, ln_g, ln_b, alpha, tm):
    T, D = h.shape
    gw = yb.shape[1]
    tok = lambda i: (i, 0)
    const = lambda i: (0, 0)
    return pl.pallas_call(
        functools.partial(_mix_out_kernel, alpha=alpha),
        grid=(T // tm,),
        in_specs=[pl.BlockSpec((tm, gw), tok),
                  pl.BlockSpec((tm, gw), tok), pl.BlockSpec((tm, gw), tok), pl.BlockSpec((tm, gw), tok),
                  pl.BlockSpec((4, gw), const),
                  pl.BlockSpec((4 * gw, D), const),
                  pl.BlockSpec((tm, D), tok),
                  pl.BlockSpec((1, D), const), pl.BlockSpec((1, D), const)],
        out_specs=[pl.BlockSpec((tm, D), tok), pl.BlockSpec((tm, D), tok)],
        out_shape=[jax.ShapeDtypeStruct((T, D), F32), jax.ShapeDtypeStruct((T, D), BF16)],
        compiler_params=_params("parallel"),
        name="mix_out",
    )(ya, yb, yc, yd, norm_g, w_out, h, ln_g.reshape(1, D), ln_b.reshape(1, D))


def _xattn_kernel(hb_ref, h_ref, wq_ref, kv_ref, wo_ref, g_ref, b_ref, o_ref, ob_ref, *, alpha):
    width = wq_ref.shape[1]
    chunks = _row_chunks(o_ref.shape[0])
    qs = [jnp.dot(hb_ref[rows, :], wq_ref[...], preferred_element_type=F32) for rows in chunks]
    fs = []
    for q in qs:
        heads = []
        for hd in range(XA_HEADS):
            lo = hd * XA_HEAD_DIM
            k = kv_ref[:, lo:lo + XA_HEAD_DIM]
            v = kv_ref[:, width + lo:width + lo + XA_HEAD_DIM]
            s = _dot_nt(q[:, lo:lo + XA_HEAD_DIM], k) * (XA_HEAD_DIM ** -0.5)
            p = jnp.exp(s - jnp.max(s, -1, keepdims=True))
            p = p * (1.0 / jnp.sum(p, -1, keepdims=True))
            heads.append(_dot(p, v).astype(BF16))
        fs.append(jnp.dot(jnp.concatenate(heads, 1), wo_ref[...], preferred_element_type=F32))
    for rows, f in zip(chunks, fs):
        out = _layer_norm(alpha * h_ref[rows, :] + f, g_ref[...], b_ref[...])
        o_ref[rows, :] = out
        ob_ref[rows, :] = out.astype(BF16)


def _xattn(hb, h, wq, kv, wo, ln_g, ln_b, seq, mem_len, alpha, tm):
    T, D = h.shape
    width = wq.shape[1]
    ns = seq // tm
    tok = lambda i: (i, 0)
    const = lambda i: (0, 0)
    return pl.pallas_call(
        functools.partial(_xattn_kernel, alpha=alpha),
        grid=(T // tm,),
        in_specs=[pl.BlockSpec((tm, D), tok), pl.BlockSpec((tm, D), tok),
                  pl.BlockSpec((D, width), const),
                  pl.BlockSpec((mem_len, 2 * width), lambda i: (i // ns, 0)),
                  pl.BlockSpec((width, D), const),
                  pl.BlockSpec((1, D), const), pl.BlockSpec((1, D), const)],
        out_specs=[pl.BlockSpec((tm, D), tok), pl.BlockSpec((tm, D), tok)],
        out_shape=[jax.ShapeDtypeStruct((T, D), F32), jax.ShapeDtypeStruct((T, D), BF16)],
        compiler_params=_params("parallel"),
        name="cross_attn",
    )(hb, h, wq, kv, wo, ln_g.reshape(1, D), ln_b.reshape(1, D))


def _ffn_kernel(hb_ref, h_ref, wg_ref, wu_ref, wd_ref, g_ref, b_ref, o_ref, *rest, alpha):
    acc_ref = rest[-1]
    j = pl.program_id(1)

    @pl.when(j == 0)
    def _():
        acc_ref[...] = alpha * h_ref[...]

    x = hb_ref[...]
    gate = jnp.dot(x, wg_ref[...], preferred_element_type=F32)
    up = jnp.dot(x, wu_ref[...], preferred_element_type=F32)
    act = (jax.nn.silu(gate) * up).astype(BF16)
    acc_ref[...] += jnp.dot(act, wd_ref[...], preferred_element_type=F32)

    @pl.when(j == pl.num_programs(1) - 1)
    def _():
        out = _layer_norm(acc_ref[...], g_ref[...], b_ref[...])
        o_ref[...] = out
        if len(rest) == 2:
            rest[0][...] = out.astype(BF16)


def _ffn(hb, h, wg, wu, wd, ln_g, ln_b, alpha, tm, th, with_bf16_copy):
    T, D = h.shape
    H = wg.shape[1]
    tok = lambda i, j: (i, 0)
    const = lambda i, j: (0, 0)
    n_out = 2 if with_bf16_copy else 1
    return pl.pallas_call(
        functools.partial(_ffn_kernel, alpha=alpha),
        grid=(T // tm, H // th),
        in_specs=[pl.BlockSpec((tm, D), tok), pl.BlockSpec((tm, D), tok),
                  pl.BlockSpec((D, th), lambda i, j: (0, j)),
                  pl.BlockSpec((D, th), lambda i, j: (0, j)),
                  pl.BlockSpec((th, D), lambda i, j: (j, 0)),
                  pl.BlockSpec((1, D), const), pl.BlockSpec((1, D), const)],
        out_specs=[pl.BlockSpec((tm, D), tok), pl.BlockSpec((tm, D), tok)][:n_out],
        out_shape=[jax.ShapeDtypeStruct((T, D), F32), jax.ShapeDtypeStruct((T, D), BF16)][:n_out],
        scratch_shapes=[pltpu.VMEM((tm, D), F32)],
        compiler_params=_params("parallel", "arbitrary"),
        name="ffn",
    )(hb, h, wg, wu, wd, ln_g.reshape(1, D), ln_b.reshape(1, D))


def _rope_tables(pos):
    inv_freq = ROPE_THETA ** (-jnp.arange(HALF, dtype=F32) / HALF)
    ang = pos.astype(F32)[:, None] * inv_freq[None, :]
    cos = jnp.tile(jnp.cos(ang), (1, LANES // HALF))
    sin = jnp.sin(ang)
    sin = jnp.tile(jnp.concatenate([-sin, sin], 1), (1, LANES // HEAD_DIM))
    return cos, sin


def _overlap_table(seq):
    n_cmp = (seq - CMP_LEN) // CMP_STRIDE + 1
    n_sel = seq // SEL_BLOCK
    ci = np.arange(n_cmp)[:, None] * CMP_STRIDE
    sj = np.arange(n_sel)[None, :] * SEL_BLOCK
    ov = np.clip(np.minimum(ci + CMP_LEN, sj + SEL_BLOCK) - np.maximum(ci, sj), 0, None) / CMP_LEN
    full = np.zeros((LANES, seq // CMP_STRIDE), np.float32)
    full[:n_sel, :n_cmp] = ov.T
    return jnp.asarray(full, dtype=BF16)


def _split_w_in(w_in_stack, layer, width):
    hw = width // 4
    kvw = NSA_KV_HEADS * HEAD_DIM
    n_gate = 3 * (hw // HEAD_DIM)
    o = np.cumsum([0, hw, hw, 6 * kvw, n_gate, 3 * hw, 3 * hw])
    col_scale = np.ones((o[-1],), np.float32)
    for q_lo in (o[1], o[4], o[5]):
        col_scale[q_lo:q_lo + hw] = HEAD_DIM ** -0.5 * math.log2(math.e)
    w_in = _layer_bf16(w_in_stack, layer, jnp.asarray(col_scale))
    u = w_in[:, o[0]:o[1]]
    nq = w_in[:, o[1]:o[2]]
    kv = [w_in[:, o[2] + j * kvw:o[2] + (j + 1) * kvw] for j in range(6)]
    gate = w_in[:, o[3]:o[4]]
    sb = w_in[:, o[4]:o[5]]
    dil = w_in[:, o[5]:o[6]]
    per_group = 3 * NSA_REP
    gates = [jnp.pad(gate[:, g * per_group:(g + 1) * per_group], ((0, 0), (0, LANES - per_group)))
             for g in range(NSA_KV_HEADS)]
    w_rope = jnp.concatenate([nq, kv[2], kv[4], dil[:, :2 * hw]], 1)
    w_plain = jnp.concatenate([sb, dil[:, 2 * hw:], kv[0], kv[1], kv[3], kv[5]] + gates, 1)
    return u, w_rope, w_plain


R_NQ, R_KSLC, R_KWIN, R_DILQ, R_DILK = 0, 4, 5, 6, 10
P_SBQ, P_SBK, P_SBV, P_DILV, P_KCMP, P_VSLC, P_VWIN, P_GATE = 0, 4, 8, 12, 16, 18, 19, 20


def _hybrid_mixer(hb, h, nb, seq, w_in_stack, layer, s5_params, cmp_pe, cmp_w1, cmp_w2, norm_g, w_out,
                  ln_g, ln_b, tables, alpha, tm):
    cos, sin, cos_c, sin_c, overlap = tables
    width = w_out.shape[0]
    n_pairs = width // 4 // LANES
    w_u, w_rope, w_plain = _split_w_in(w_in_stack, layer, width)
    r_arr = _proj(hb, w_rope, min(2 * tm, seq), w_rope.shape[1], seq=seq, rope=(cos, sin))
    p_arr = _proj(hb, w_plain, tm, w_plain.shape[1])
    y_a = _s5(hb, w_u, nb, seq, *s5_params)
    cmp_kv = _compress(p_arr, P_KCMP, nb, seq, cmp_pe, cmp_w1, cmp_w2, cos_c, sin_c)
    y_b = _nsa(r_arr, p_arr, cmp_kv, overlap, nb, seq,
               dict(q=R_NQ, gate=P_GATE, k_slc=R_KSLC, v_slc=P_VSLC, k_win=R_KWIN, v_win=P_VWIN))
    y_c = _stick_breaking(p_arr, nb, seq, P_SBQ, P_SBK, P_SBV, n_pairs)
    y_d = _dilated(r_arr, R_DILQ, r_arr, R_DILK, p_arr, P_DILV, nb, seq, n_pairs)
    return _mix_out(y_a, y_b, y_c, y_d, norm_g, w_out, h, ln_g, ln_b, alpha, tm)


def kernel(x, mem, ln_in_g, ln_in_b, w_in, s5_lambda_re, s5_lambda_im, s5_log_dt, s5_b_re, s5_b_im, s5_c_re, s5_c_im, s5_d, s5_w_glu, s5_b_glu, nsa_cmp_pe, nsa_cmp_w1, nsa_cmp_w2, mix_norm_g, w_out, ln1_g, ln1_b, xa_wq, xa_wkv, xa_wo, ln2_g, ln2_b, ffn_w_gate, ffn_w_up, ffn_w_down, ln3_g, ln3_b):
    nb, seq, d_model = x.shape
    mem_len = mem.shape[1]
    depth = w_in.shape[0]
    alpha = (2 * depth) ** 0.25
    tm = min(512, seq)
    pos = jnp.arange(seq)
    cos, sin = _rope_tables(pos)
    nblk = seq // CMP_STRIDE
    cos_c, sin_c = _rope_tables(jnp.arange(nblk) * CMP_STRIDE + CMP_LEN - 1)
    tables = (cos, sin, cos_c, sin_c, _overlap_table(seq))
    mem_b = mem.reshape(nb * mem_len, d_model).astype(BF16)

    h, hb = _ln_in(x.reshape(nb * seq, d_model), ln_in_g, ln_in_b, tm)
    for l in range(depth):
        s5_params = (s5_lambda_re[l], s5_lambda_im[l], s5_log_dt[l], s5_b_re[l], s5_b_im[l],
                     s5_c_re[l], s5_c_im[l], s5_d[l], s5_w_glu[l], s5_b_glu[l])
        h, hb = _hybrid_mixer(hb, h, nb, seq, w_in, l, s5_params, nsa_cmp_pe[l], nsa_cmp_w1[l],
                              nsa_cmp_w2[l], mix_norm_g[l], _layer_bf16(w_out, l), ln1_g[l], ln1_b[l],
                              tables, alpha, tm)
        kv = _proj(mem_b, _layer_bf16(xa_wkv, l), min(512, nb * mem_len), xa_wkv.shape[2] // 2)
        h, hb = _xattn(hb, h, _layer_bf16(xa_wq, l), kv, _layer_bf16(xa_wo, l), ln2_g[l], ln2_b[l],
                       seq, mem_len, alpha, tm)
        last = l == depth - 1
        outs = _ffn(hb, h, _layer_bf16(ffn_w_gate, l), _layer_bf16(ffn_w_up, l),
                    _layer_bf16(ffn_w_down, l), ln3_g[l], ln3_b[l], alpha, tm, 512, not last)
        h, hb = (outs[0], None) if last else outs
    return h.reshape(nb, seq, d_model)
```

```python
import functools
import math

import numpy as np
import jax
import jax.numpy as jnp
from jax import lax
from jax.experimental import pallas as pl
from jax.experimental.pallas import tpu as pltpu

F32 = jnp.float32
BF16 = jnp.bfloat16

LANES = 128
MXU_WIDTH = 256
CAST_ROWS = 512
VMEM_LIMIT = 56 * 1024 * 1024

HEAD_DIM = 64
HALF = HEAD_DIM // 2
ROPE_THETA = 10000.0
LN_EPS = 1e-5
RMS_EPS = 1e-6
SSM_CH = 16
SSM_STATE = 64
NSA_KV_HEADS = 2
NSA_REP = 4
CMP_LEN = 32
CMP_STRIDE = 16
CMP_HIDDEN = 128
SEL_BLOCK = 64
SEL_TOPN = 8
NSA_WINDOW = 512
DIL_CONFIGS = ((128, 1), (512, 4), (2048, 16))
XA_HEADS = 4
XA_HEAD_DIM = 128
Q_TILE = 128
SB_TILE = 256
SB_PAIRS = 4
NSA_TILE = 256
NEG_BIG = -1e30


def _params(*sem):
    return pltpu.CompilerParams(dimension_semantics=sem, vmem_limit_bytes=VMEM_LIMIT)


def _dot(a, b):
    return jnp.dot(a.astype(BF16), b.astype(BF16), preferred_element_type=F32)


def _dot_nt(a, b):
    return lax.dot_general(a.astype(BF16), b.astype(BF16), (((1,), (1,)), ((), ())),
                           preferred_element_type=F32)


def _layer_norm(x, g, b):
    mu = jnp.mean(x, -1, keepdims=True)
    xc = x - mu
    var = jnp.mean(xc * xc, -1, keepdims=True)
    return xc * lax.rsqrt(var + LN_EPS) * g + b


def _lane_iota(shape):
    return lax.broadcasted_iota(jnp.int32, shape, len(shape) - 1)


def _row_iota(shape):
    return lax.broadcasted_iota(jnp.int32, shape, len(shape) - 2)


def _swap_halves(x):
    return pltpu.roll(x, HEAD_DIM, axis=x.ndim - 1)


def _ln_in_kernel(x_ref, g_ref, b_ref, h_ref, hb_ref):
    y = _layer_norm(x_ref[...], g_ref[...], b_ref[...])
    h_ref[...] = y
    hb_ref[...] = y.astype(BF16)


def _ln_in(x2, g, b, tm):
    T, D = x2.shape
    return pl.pallas_call(
        _ln_in_kernel,
        grid=(T // tm,),
        in_specs=[pl.BlockSpec((tm, D), lambda i: (i, 0)),
                  pl.BlockSpec((1, D), lambda i: (0, 0)),
                  pl.BlockSpec((1, D), lambda i: (0, 0))],
        out_specs=[pl.BlockSpec((tm, D), lambda i: (i, 0)),
                   pl.BlockSpec((tm, D), lambda i: (i, 0))],
        out_shape=[jax.ShapeDtypeStruct((T, D), F32), jax.ShapeDtypeStruct((T, D), BF16)],
        compiler_params=_params("parallel"),
        name="ln_in",
    )(x2, g.reshape(1, D), b.reshape(1, D))


def _cast_kernel(w_ref, o_ref):
    o_ref[...] = w_ref[...].astype(BF16)


def _cast_scaled_kernel(w_ref, s_ref, o_ref):
    o_ref[...] = (w_ref[...] * s_ref[...]).astype(BF16)


def _layer_bf16(w_stack, layer, col_scale=None):
    _, K, N = w_stack.shape
    tk = min(CAST_ROWS, K)
    in_specs = [pl.BlockSpec((None, tk, N), lambda i: (layer, i, 0))]
    args = [w_stack]
    kern = _cast_kernel
    if col_scale is not None:
        in_specs.append(pl.BlockSpec((1, N), lambda i: (0, 0)))
        args.append(col_scale.reshape(1, N))
        kern = _cast_scaled_kernel
    return pl.pallas_call(
        kern, grid=(K // tk,), in_specs=in_specs,
        out_specs=pl.BlockSpec((tk, N), lambda i: (i, 0)),
        out_shape=jax.ShapeDtypeStruct((K, N), BF16),
        compiler_params=_params("parallel"), name="cast_bf16",
    )(*args)


def _row_chunks(n, parts=2):
    step = n // parts
    return [slice(i * step, (i + 1) * step) for i in range(parts)]


def _column_chunks(n):
    return [(lo, min(lo + MXU_WIDTH, n)) for lo in range(0, n, MXU_WIDTH)]


def _proj_kernel(a_ref, w_ref, o_ref):
    a = a_ref[...]
    for lo, hi in _column_chunks(o_ref.shape[1]):
        o_ref[:, lo:hi] = jnp.dot(a, w_ref[:, lo:hi], preferred_element_type=F32)


def _proj_rope_kernel(a_ref, w_ref, cos_ref, sin_ref, o_ref):
    a = a_ref[...]
    cos = cos_ref[...]
    sin = sin_ref[...]
    first = (_lane_iota((1, LANES)) % HEAD_DIM) < HALF
    for lo, hi in _column_chunks(o_ref.shape[1]):
        acc = jnp.dot(a, w_ref[:, lo:hi], preferred_element_type=F32)
        for c in range((hi - lo) // LANES):
            x = acc[:, c * LANES:(c + 1) * LANES]
            partner = jnp.where(first, pltpu.roll(x, LANES - HALF, axis=1), pltpu.roll(x, HALF, axis=1))
            o_ref[:, lo + c * LANES:lo + (c + 1) * LANES] = x * cos + partner * sin


def _proj(a, w, tm, tn, seq=None, rope=None):
    M, K = a.shape
    N = w.shape[1]
    nm, nn = M // tm, N // tn
    in_specs = [pl.BlockSpec((tm, K), lambda j, i: (i, 0)),
                pl.BlockSpec((K, tn), lambda j, i: (0, j))]
    args = [a, w]
    kern = _proj_kernel
    if rope is not None:
        ns = seq // tm
        in_specs += [pl.BlockSpec((tm, LANES), lambda j, i: (i % ns, 0))] * 2
        args += list(rope)
        kern = _proj_rope_kernel
    return pl.pallas_call(
        kern, grid=(nn, nm), in_specs=in_specs,
        out_specs=pl.BlockSpec((tm, tn), lambda j, i: (i, j)),
        out_shape=jax.ShapeDtypeStruct((M, N), F32),
        compiler_params=_params("parallel", "parallel"), name="proj",
    )(*args)


def _s5_kernel(h_ref, wu_ref, bb_ref, cc_ref, are_ref, aim_ref, d_ref, wg_ref, bg_ref, y_ref,
               buf_ref, st_ref, tm_ref, *, nb, ts):
    n_chunks, cw, sw2 = bb_ref.shape
    sw = sw2 // 2
    assert cw == tm_ref.shape[2]

    @pl.when(pl.program_id(0) == 0)
    def _():
        st_ref[...] = jnp.zeros_like(st_ref)

    u_bt = jnp.dot(h_ref[...].reshape(nb * ts, h_ref.shape[2]), wu_ref[...],
                   preferred_element_type=F32)
    for b in range(nb):
        for c in range(n_chunks):
            tm_ref[c, pl.ds(b, ts, stride=nb), :] = u_bt[b * ts:(b + 1) * ts, c * cw:(c + 1) * cw]
    u = jnp.concatenate([tm_ref[c] for c in range(n_chunks)], 1)
    ys = []
    for c in range(n_chunks):
        re0, im0 = c * sw2, c * sw2 + sw
        buf_ref[:, re0:re0 + sw2] = jnp.dot(u[:, c * cw:(c + 1) * cw].astype(BF16), bb_ref[c],
                                            preferred_element_type=F32)
        a_re = jnp.broadcast_to(are_ref[:, c * sw:(c + 1) * sw], (nb, sw))
        a_im = jnp.broadcast_to(aim_ref[:, c * sw:(c + 1) * sw], (nb, sw))

        def step(t, carry, re0=re0, im0=im0, a_re=a_re, a_im=a_im):
            x_re, x_im = carry
            r = pl.multiple_of(t * nb, nb)
            n_re = a_re * x_re - a_im * x_im + buf_ref[pl.ds(r, nb), re0:re0 + sw]
            n_im = a_re * x_im + a_im * x_re + buf_ref[pl.ds(r, nb), im0:im0 + sw]
            buf_ref[pl.ds(r, nb), re0:re0 + sw] = n_re
            buf_ref[pl.ds(r, nb), im0:im0 + sw] = n_im
            return n_re, n_im

        x_re, x_im = lax.fori_loop(0, ts, step, (st_ref[:, re0:re0 + sw], st_ref[:, im0:im0 + sw]),
                                   unroll=True)
        st_ref[:, re0:re0 + sw] = x_re
        st_ref[:, im0:im0 + sw] = x_im
        ys.append(jnp.dot(buf_ref[:, re0:re0 + sw2].astype(BF16), cc_ref[c], preferred_element_type=F32))

    y = jnp.concatenate(ys, 1) + d_ref[...] * u
    g = jax.nn.gelu(y)
    z = jnp.dot(g.astype(BF16), wg_ref[...], preferred_element_type=F32) + bg_ref[...]
    out = g * jax.nn.sigmoid(z)
    for c in range(n_chunks):
        tm_ref[c] = out[:, c * cw:(c + 1) * cw]
    for b in range(nb):
        for c in range(n_chunks):
            y_ref[b, :, c * cw:(c + 1) * cw] = tm_ref[c, pl.ds(b, ts, stride=nb), :]


def _s5(hb, w_u, nb, seq, lam_re, lam_im, log_dt, b_re, b_im, c_re, c_im, d_skip, w_glu, b_glu, ts=64):
    D, W = w_u.shape
    G, P = lam_re.shape
    C = SSM_CH
    lr = jnp.minimum(lam_re, -1e-4)
    li = lam_im
    dt = jnp.exp(log_dt)[:, None]
    mag = jnp.exp(lr * dt)
    a_re = mag * jnp.cos(li * dt)
    a_im = mag * jnp.sin(li * dt)
    den = lr * lr + li * li
    z_re = ((a_re - 1.0) * lr + a_im * li) / den
    z_im = (a_im * lr - (a_re - 1.0) * li) / den
    bb_re = z_re[..., None] * b_re - z_im[..., None] * b_im
    bb_im = z_re[..., None] * b_im + z_im[..., None] * b_re
    gc = LANES // C
    nc = G // gc
    eye = jnp.eye(gc, dtype=F32)

    def block_diag_in(m):
        return jnp.einsum('ngpc,gh->ngchp', m.reshape(nc, gc, P, C), eye).reshape(nc, gc * C, gc * P)

    def block_diag_out(m):
        return jnp.einsum('ngcp,gh->ngphc', m.reshape(nc, gc, C, P), eye).reshape(nc, gc * P, gc * C)

    bb = jnp.concatenate([block_diag_in(bb_re), block_diag_in(bb_im)], 2).astype(BF16)
    cc = jnp.concatenate([block_diag_out(c_re), -block_diag_out(c_im)], 1).astype(BF16)
    ns = G * P
    rows = ts * nb
    kern = functools.partial(_s5_kernel, nb=nb, ts=ts)
    const = lambda i: (0, 0)
    const3 = lambda i: (0, 0, 0)
    y = pl.pallas_call(
        kern,
        grid=(seq // ts,),
        in_specs=[pl.BlockSpec((nb, ts, D), lambda i: (0, i, 0)),
                  pl.BlockSpec((D, W), const),
                  pl.BlockSpec(bb.shape, const3),
                  pl.BlockSpec(cc.shape, const3),
                  pl.BlockSpec((1, ns), const),
                  pl.BlockSpec((1, ns), const),
                  pl.BlockSpec((1, W), const),
                  pl.BlockSpec((W, W), const),
                  pl.BlockSpec((1, W), const)],
        out_specs=pl.BlockSpec((nb, ts, W), lambda i: (0, i, 0)),
        out_shape=jax.ShapeDtypeStruct((nb, seq, W), F32),
        scratch_shapes=[pltpu.VMEM((rows, 2 * ns), F32), pltpu.VMEM((nb, 2 * ns), F32),
                        pltpu.VMEM((nc, rows, LANES), F32)],
        compiler_params=_params("arbitrary"),
        name="s5",
    )(hb.reshape(nb, seq, D), w_u, bb, cc, a_re.reshape(1, ns), a_im.reshape(1, ns),
      d_skip.reshape(1, W), w_glu.astype(BF16), b_glu.reshape(1, W))
    return y.reshape(nb * seq, W)


def _stack_pair(q):
    first = _lane_iota((1, LANES)) < HEAD_DIM
    return jnp.concatenate([jnp.where(first, q, 0.0), jnp.where(first, 0.0, q)], 0)


def _unstack_pair(x):
    t = x.shape[0] // 2
    first = _lane_iota((1, LANES)) < HEAD_DIM
    return jnp.where(first, x[:t], x[t:])


def _sb_kernel(q_ref, k_ref, v_ref, o_ref, acc_ref, tail_ref):
    tq = q_ref.shape[0]
    qi = pl.program_id(2)
    n_pairs = q_ref.shape[1] // LANES
    groups = [slice(p * LANES, (p + 1) * LANES) for p in range(n_pairs)]
    qs = [_stack_pair(q_ref[:, g]).astype(BF16) for g in groups]
    tri = (_row_iota((tq, tq)) > _lane_iota((tq, tq))).astype(BF16)
    diag = _lane_iota((2 * tq, tq)) < (_row_iota((2 * tq, tq)) % tq)

    def tile(kj, diagonal):
        r = pl.multiple_of(kj * tq, tq)
        zs = [_dot_nt(q, k_ref[pl.ds(r, tq), g]) for q, g in zip(qs, groups)]
        stage = []
        for z in zs:
            pos = jnp.maximum(z, 0.0)
            neg = z - pos
            log_term = jnp.log2(1.0 + jnp.exp2(neg - pos))
            sp = pos + log_term
            if diagonal:
                sp = jnp.where(diag, sp, 0.0)
            stage.append((neg - log_term, jnp.sum(sp, -1, keepdims=True), _dot(sp, tri)))
        for p, ((log_beta, row_sum, after), g) in enumerate(zip(stage, groups)):
            if diagonal:
                w = jnp.where(diag, jnp.exp2(log_beta - after), 0.0)
                acc_ref[p] = _dot(w, v_ref[pl.ds(r, tq), g])
                tail_ref[p] = row_sum
            else:
                w = jnp.exp2(log_beta - (after + tail_ref[p]))
                acc_ref[p] += _dot(w, v_ref[pl.ds(r, tq), g])
                tail_ref[p] += row_sum

    tile(qi, True)

    def two_tiles(i, _):
        tile(qi - 1 - 2 * i, False)
        tile(qi - 2 - 2 * i, False)
        return 0

    def last_tile(i, _):
        tile(0, False)
        return 0

    lax.fori_loop(0, qi // 2, two_tiles, 0)
    lax.fori_loop(0, qi % 2, last_tile, 0)
    for p, g in enumerate(groups):
        o_ref[:, g] = _unstack_pair(acc_ref[p])


def _stick_breaking(qkv, nb, seq, q_col, k_col, v_col, n_pairs):
    tq = min(SB_TILE, seq)
    nq = seq // tq
    per = SB_PAIRS
    w = per * LANES
    assert n_pairs % per == 0 and q_col % per == 0 and k_col % per == 0 and v_col % per == 0
    return pl.pallas_call(
        _sb_kernel,
        grid=(nb, n_pairs // per, nq),
        in_specs=[pl.BlockSpec((tq, w), lambda b, p, i: (b * nq + i, q_col // per + p)),
                  pl.BlockSpec((seq, w), lambda b, p, i: (b, k_col // per + p)),
                  pl.BlockSpec((seq, w), lambda b, p, i: (b, v_col // per + p))],
        out_specs=pl.BlockSpec((tq, w), lambda b, p, i: (b * nq + i, p)),
        out_shape=jax.ShapeDtypeStruct((nb * seq, n_pairs * LANES), F32),
        scratch_shapes=[pltpu.VMEM((per, 2 * tq, LANES), F32), pltpu.VMEM((per, 2 * tq, 1), F32)],
        compiler_params=_params("parallel", "parallel", "arbitrary"),
        name="stick_breaking",
    )(qkv, qkv, qkv)


def _dil_kernel(q_ref, k_ref, v_ref, o_ref, m_ref, l_ref, a_ref):
    seq = q_ref.shape[0]
    tq = Q_TILE
    first = _lane_iota((1, LANES)) < HEAD_DIM

    q_in = _row_iota((2 * tq, 1)) % tq

    def band_bias(wd, n_tiles):
        if n_tiles > 1:
            lag = q_in - (_lane_iota((1, 2 * tq)) - tq)
            return jnp.where((lag >= 0) & (lag <= wd), 0.0, NEG_BIG)
        return jnp.where(_lane_iota((1, tq)) <= q_in, 0.0, NEG_BIG)

    def tile(c, dil, band, n_tiles, r, i):
        base = r + dil * tq * i

        def rows(ref, start):
            if dil == 1:
                return ref[pl.ds(pl.multiple_of(start, tq), tq), :]
            return ref[pl.ds(start, tq, stride=dil), :]

        qs = _stack_pair(rows(q_ref, base)).astype(BF16)
        if n_tiles > 1:
            prev = r + dil * tq * jnp.maximum(i - 1, 0)
            kk = jnp.concatenate([rows(k_ref, prev), rows(k_ref, base)], 0)
            vv = jnp.concatenate([rows(v_ref, prev), rows(v_ref, base)], 0)
            no_prev = jnp.where((i == 0) & (_lane_iota((1, 2 * tq)) < tq), NEG_BIG, 0.0)
            bias = band + no_prev
        else:
            kk = rows(k_ref, base)
            vv = rows(v_ref, base)
            bias = band
        s = _dot_nt(qs, kk) + bias
        m = jnp.max(s, -1, keepdims=True)
        p = jnp.exp2(s - m)
        l = jnp.sum(p, -1, keepdims=True)
        acc = _dot(p, vv)
        m2 = jnp.where(first, m[:tq], m[tq:])
        l2 = jnp.where(first, l[:tq], l[tq:])
        a2 = jnp.where(first, acc[:tq], acc[tq:])
        if dil == 1:
            sl = pl.ds(pl.multiple_of(base, tq), tq)
        else:
            sl = pl.ds(base, tq, stride=dil)
        m_ref[c, sl, :] = m2
        l_ref[c, sl, :] = l2
        a_ref[c, sl, :] = a2

    for c, (window, dil) in enumerate(DIL_CONFIGS):
        wd = window // dil
        n_tiles = seq // dil // tq

        band = band_bias(wd, n_tiles)

        def per_tile(n, _, c=c, dil=dil, band=band, n_tiles=n_tiles):
            tile(c, dil, band, n_tiles, n // n_tiles, n % n_tiles)
            return 0

        lax.fori_loop(0, dil * n_tiles, per_tile, 0, unroll=16)

    def combine(i, _):
        sl = pl.ds(pl.multiple_of(i * tq, tq), tq)
        m0, m1, m2 = m_ref[0, sl, :], m_ref[1, sl, :], m_ref[2, sl, :]
        mx = jnp.maximum(jnp.maximum(m0, m1), m2)
        e0, e1, e2 = jnp.exp2(m0 - mx), jnp.exp2(m1 - mx), jnp.exp2(m2 - mx)
        num = e0 * a_ref[0, sl, :] + e1 * a_ref[1, sl, :] + e2 * a_ref[2, sl, :]
        den = e0 * l_ref[0, sl, :] + e1 * l_ref[1, sl, :] + e2 * l_ref[2, sl, :]
        o_ref[sl, :] = num / den
        return 0

    lax.fori_loop(0, seq // tq, combine, 0)


def _dilated(q_arr, q_col, k_arr, k_col, v_arr, v_col, nb, seq, n_pairs):
    return pl.pallas_call(
        _dil_kernel,
        grid=(nb, n_pairs),
        in_specs=[pl.BlockSpec((seq, LANES), lambda b, p: (b, q_col + p)),
                  pl.BlockSpec((seq, LANES), lambda b, p: (b, k_col + p)),
                  pl.BlockSpec((seq, LANES), lambda b, p: (b, v_col + p))],
        out_specs=pl.BlockSpec((seq, LANES), lambda b, p: (b, p)),
        out_shape=jax.ShapeDtypeStruct((nb * seq, n_pairs * LANES), F32),
        scratch_shapes=[pltpu.VMEM((3, seq, LANES), F32)] * 3,
        compiler_params=_params("parallel", "parallel"),
        name="dilated",
    )(q_arr, k_arr, v_arr)


def _cmp_kernel(t_ref, pe_ref, w1a_ref, w1b_ref, w2_ref, cos_ref, sin_ref, o_ref):
    nblk = t_ref.shape[0] // CMP_STRIDE
    j = pl.program_id(1)
    out = jnp.zeros((nblk, LANES), F32)
    for g in range(NSA_KV_HEADS):
        p1 = jnp.zeros((nblk, CMP_HIDDEN), F32)
        p2 = jnp.zeros((nblk, CMP_HIDDEN), F32)
        for l in range(CMP_STRIDE):
            x = t_ref[pl.ds(l, nblk, stride=CMP_STRIDE), :]
            p1 = p1 + _dot(x + pe_ref[0, l:l + 1, :], w1a_ref[0, g, l])
            p2 = p2 + _dot(x + pe_ref[0, CMP_STRIDE + l:CMP_STRIDE + l + 1, :], w1b_ref[0, g, l])
        hidden = p1 + pltpu.roll(p2, nblk - 1, axis=0)
        out = out + _dot(jax.nn.gelu(hidden), w2_ref[0, g])
    first = (_lane_iota((1, LANES)) % HEAD_DIM) < HALF
    partner = jnp.where(first, pltpu.roll(out, LANES - HALF, axis=1), pltpu.roll(out, HALF, axis=1))
    roped = out * cos_ref[...] + partner * sin_ref[...]
    o_ref[0] = jnp.where(j == 0, roped, out)


def _compress(p_arr, col0, nb, seq, pe, w1, w2, cos_c, sin_c):
    nblk = seq // CMP_STRIDE
    G = NSA_KV_HEADS
    pe2 = jnp.tile(pe, (1, 1, G))
    w1r = w1.reshape(2, CMP_LEN, HEAD_DIM, CMP_HIDDEN)
    w1e = jnp.zeros((2, G, CMP_LEN, LANES, CMP_HIDDEN), F32)
    w2e = jnp.zeros((2, G, CMP_HIDDEN, LANES), F32)
    for g in range(G):
        w1e = w1e.at[:, g, :, g * HEAD_DIM:(g + 1) * HEAD_DIM, :].set(w1r)
        w2e = w2e.at[:, g, :, g * HEAD_DIM:(g + 1) * HEAD_DIM].set(w2)
    w1e = w1e.astype(BF16)
    w2e = w2e.astype(BF16)
    return pl.pallas_call(
        _cmp_kernel,
        grid=(nb, 2),
        in_specs=[pl.BlockSpec((seq, LANES), lambda b, j: (b, col0 + j)),
                  pl.BlockSpec((1, CMP_LEN, LANES), lambda b, j: (j, 0, 0)),
                  pl.BlockSpec((1, G, CMP_STRIDE, LANES, CMP_HIDDEN), lambda b, j: (j, 0, 0, 0, 0)),
                  pl.BlockSpec((1, G, CMP_STRIDE, LANES, CMP_HIDDEN), lambda b, j: (j, 0, 1, 0, 0)),
                  pl.BlockSpec((1, G, CMP_HIDDEN, LANES), lambda b, j: (j, 0, 0, 0)),
                  pl.BlockSpec((nblk, LANES), lambda b, j: (0, 0)),
                  pl.BlockSpec((nblk, LANES), lambda b, j: (0, 0))],
        out_specs=pl.BlockSpec((1, nblk, LANES), lambda b, j: (j, b, 0)),
        out_shape=jax.ShapeDtypeStruct((2, nb * nblk, LANES), F32),
        compiler_params=_params("parallel", "parallel"),
        name="nsa_compress",
    )(p_arr, pe2, w1e, w1e, w2e, cos_c, sin_c)


def _nsa_kernel(q_ref, gate_ref, kc_ref, vc_ref, ks_ref, vs_ref, kw_ref, vw_ref, ovt_ref, o_ref,
                ms_ref, as_ref, mw_ref, aw_ref):
    tq = q_ref.shape[0]
    R, G = NSA_REP, NSA_KV_HEADS
    seq = ks_ref.shape[0]
    n_sel = seq // SEL_BLOCK
    n_cmp = (seq - CMP_LEN) // CMP_STRIDE + 1
    nblk = kc_ref.shape[1]
    top_n = min(SEL_TOPN, n_sel)
    qi = pl.program_id(1)
    lane = _lane_iota((1, LANES))
    first = lane < HEAD_DIM
    mine = [first, jnp.logical_not(first)]
    t_row = qi * tq + _row_iota((tq, 1))

    qs = []
    for g in range(G):
        parts = []
        for r in range(R):
            h = g * R + r
            x = q_ref[:, (h // 2) * LANES:(h // 2 + 1) * LANES]
            parts.append(jnp.where(mine[g], x if h % 2 == g else _swap_halves(x), 0.0))
        qs.append(jnp.concatenate(parts, 0).astype(BF16))

    c_idx = _lane_iota((tq, nblk))
    visible = (c_idx < n_cmp) & (c_idx * CMP_STRIDE + (CMP_LEN - 1) <= qi * tq + _row_iota((tq, nblk)))
    c_bias = jnp.where(visible, 0.0, NEG_BIG)
    kc, vc = kc_ref[0], vc_ref[0]
    vc_swapped = _swap_halves(vc)
    s_cs = [_dot_nt(q, kc).reshape(R, tq, nblk) + c_bias for q in qs]
    p_cs, o_cs = [], []
    for g, s_c in enumerate(s_cs):
        m_c = jnp.max(s_c, -1, keepdims=True)
        m_c = jnp.where(m_c > 0.5 * NEG_BIG, m_c, 0.0)
        p_c = jnp.exp2(s_c - m_c)
        p_c = p_c * (1.0 / jnp.maximum(jnp.sum(p_c, -1, keepdims=True), 1e-30))
        p_cs.append(p_c)
        o_cs.append(_dot(p_c.reshape(R * tq, nblk), jnp.where(mine[g], vc, vc_swapped)).reshape(R, tq, LANES))

    n_idx = _row_iota((n_sel, tq))
    t_q = qi * tq + _lane_iota((n_sel, tq))
    cur = t_q // SEL_BLOCK
    future = n_idx * SEL_BLOCK > t_q
    forced = (n_idx == 0) | (n_idx == cur) | (n_idx == cur - 1)
    sels = []
    for p_c in p_cs:
        psum = jnp.sum(p_c, 0)
        p_hi = psum.astype(BF16)
        p_lo = (psum - p_hi.astype(F32)).astype(BF16)
        imp = (_dot_nt(ovt_ref[...], p_hi) + _dot_nt(ovt_ref[...], p_lo))[:n_sel]
        imp = jnp.where(forced, 1e9, jnp.where(future, -1.0, imp))
        rank = jnp.zeros((n_sel, tq), F32)
        for m in range(n_sel):
            row = imp[m:m + 1, :]
            earlier = (n_idx > m).astype(F32)
            rank = rank + jnp.where(row > imp, 1.0, jnp.where(row == imp, earlier, 0.0))
        sel_t = jnp.where((rank < top_n) & (imp > -0.5), 1.0, 0.0)
        sel_t = jnp.concatenate([sel_t, jnp.zeros((LANES - n_sel, tq), F32)], 0)
        sels.append(sel_t.T.astype(BF16))

    k_col = _lane_iota((tq, tq))
    blk_row = _row_iota((LANES, tq))
    blk_of_key = _lane_iota((LANES, tq)) // SEL_BLOCK

    def sel_biases(j, causal):
        expand = (blk_row == (j * (tq // SEL_BLOCK) + blk_of_key)).astype(BF16)
        out = []
        for sel in sels:
            chosen = jnp.dot(sel, expand, preferred_element_type=F32)
            bias = (1.0 - chosen) * NEG_BIG
            if causal:
                bias = jnp.where((j * tq + k_col) <= t_row, bias, NEG_BIG)
            out.append(bias)
        return out

    def win_bias(j):
        diff = t_row - (j * tq + k_col)
        return jnp.where((diff >= 0) & (diff < NSA_WINDOW), 0.0, NEG_BIG)

    def tile_of(ref, j):
        return ref[pl.ds(pl.multiple_of(j * tq, tq), tq), :]

    def scores(k_ref, j, biases):
        kt = tile_of(k_ref, j).astype(BF16)
        return [_dot_nt(q, kt).reshape(R, tq, tq) + b for q, b in zip(qs, biases)]

    def update(m_ref, acc_ref, s_groups, v_ref, j):
        v = tile_of(v_ref, j)
        for g, s in enumerate(s_groups):
            m_run = m_ref[g]
            m_new = jnp.maximum(m_run, jnp.max(s, -1, keepdims=True))
            alpha = jnp.exp2(m_run - m_new)
            p = jnp.exp2(s - jnp.concatenate([m_new] * (tq // LANES), -1))
            pv = _dot(p.reshape(R * tq, tq), jnp.where(mine[g], v, 1.0))
            acc_ref[g] = alpha * acc_ref[g] + pv.reshape(R, tq, LANES)
            m_ref[g] = m_new

    def normalise(acc_ref):
        out = []
        for g in range(G):
            acc = acc_ref[g]
            swapped = _swap_halves(acc)
            out.append(jnp.where(mine[g], acc, swapped) / jnp.where(mine[g], swapped, acc))
        return out

    for m_ref, acc_ref in ((ms_ref, as_ref), (mw_ref, aw_ref)):
        m_ref[...] = jnp.full(m_ref.shape, NEG_BIG, F32)
        acc_ref[...] = jnp.zeros(acc_ref.shape, F32)
    j_win = jnp.maximum(qi - NSA_WINDOW // tq, 0)

    def selected_only(j, _):
        update(ms_ref, as_ref, scores(ks_ref, j, sel_biases(j, False)), vs_ref, j)
        return 0

    def selected_and_window(j, _):
        b_w = win_bias(j)
        s_s, s_w = scores(ks_ref, j, sel_biases(j, True)), scores(kw_ref, j, [b_w] * G)
        update(ms_ref, as_ref, s_s, vs_ref, j)
        update(mw_ref, aw_ref, s_w, vw_ref, j)
        return 0

    def paired(step, first_tile, n_tiles):
        lax.fori_loop(0, n_tiles // 2,
                      lambda i, c: step(first_tile + 2 * i + 1, step(first_tile + 2 * i, c)), 0)
        lax.fori_loop(0, n_tiles % 2, lambda i, c: step(first_tile + n_tiles - 1, c), 0)

    paired(selected_only, 0, j_win)
    paired(selected_and_window, j_win, qi + 1 - j_win)
    o_ss, o_ws = normalise(as_ref), normalise(aw_ref)

    gate = jax.nn.sigmoid(gate_ref[...])
    outs = []
    for g in range(G):
        for r in range(R):
            c0 = g * LANES + 3 * r
            outs.append(gate[:, c0:c0 + 1] * o_cs[g][r] + gate[:, c0 + 1:c0 + 2] * o_ss[g][r]
                        + gate[:, c0 + 2:c0 + 3] * o_ws[g][r])
    for pair in range(G * R // 2):
        o_ref[:, pair * LANES:(pair + 1) * LANES] = jnp.where(first, outs[2 * pair], outs[2 * pair + 1])


def _nsa(r_arr, p_arr, cmp_kv, overlap, nb, seq, cols):
    tq = min(NSA_TILE, seq)
    nq = seq // tq
    nblk = seq // CMP_STRIDE
    G = NSA_KV_HEADS
    qw = G * NSA_REP * HEAD_DIM
    gw = G * LANES
    assert (cols['q'] * LANES) % qw == 0 and (cols['gate'] * LANES) % gw == 0
    return pl.pallas_call(
        _nsa_kernel,
        grid=(nb, nq),
        in_specs=[pl.BlockSpec((tq, qw), lambda b, i: (b * nq + i, cols['q'] * LANES // qw)),
                  pl.BlockSpec((tq, gw), lambda b, i: (b * nq + i, cols['gate'] * LANES // gw)),
                  pl.BlockSpec((1, nblk, LANES), lambda b, i: (0, b, 0)),
                  pl.BlockSpec((1, nblk, LANES), lambda b, i: (1, b, 0)),
                  pl.BlockSpec((seq, LANES), lambda b, i: (b, cols['k_slc'])),
                  pl.BlockSpec((seq, LANES), lambda b, i: (b, cols['v_slc'])),
                  pl.BlockSpec((seq, LANES), lambda b, i: (b, cols['k_win'])),
                  pl.BlockSpec((seq, LANES), lambda b, i: (b, cols['v_win'])),
                  pl.BlockSpec((LANES, nblk), lambda b, i: (0, 0))],
        out_specs=pl.BlockSpec((tq, qw), lambda b, i: (b * nq + i, 0)),
        out_shape=jax.ShapeDtypeStruct((nb * seq, qw), F32),
        scratch_shapes=[pltpu.VMEM((G, NSA_REP, tq, LANES), F32)] * 4,
        compiler_params=_params("parallel", "arbitrary"),
        name="nsa",
    )(r_arr, p_arr, cmp_kv, cmp_kv, r_arr, p_arr, r_arr, p_arr, overlap)


def _mix_out_kernel(ya_ref, yb_ref, yc_ref, yd_ref, ng_ref, w_ref, h_ref, g_ref, b_ref,
                    o_ref, ob_ref, *, alpha):
    fs = []
    for rows in _row_chunks(o_ref.shape[0]):
        ys = []
        for n, y_ref in enumerate((ya_ref, yb_ref, yc_ref, yd_ref)):
            y = y_ref[rows, :]
            y = y * lax.rsqrt(jnp.mean(y * y, -1, keepdims=True) + RMS_EPS) * ng_ref[n:n + 1, :]
            ys.append(y.astype(BF16))
        fs.append(jnp.dot(jnp.concatenate(ys, 1), w_ref[...], preferred_element_type=F32))
    for rows, f in zip(_row_chunks(o_ref.shape[0]), fs):
        out = _layer_norm(alpha * h_ref[rows, :] + f, g_ref[...], b_ref[...])
        o_ref[rows, :] = out
        ob_ref[rows, :] = out.astype(BF16)


def _mix_out(ya, yb, yc, yd, norm_g, w_out, h, ln_g, ln_b, alpha, tm):
    T, D = h.shape
    gw = yb.shape[1]
    tok = lambda i: (i, 0)
    const = lambda i: (0, 0)
    return pl.pallas_call(
        functools.partial(_mix_out_kernel, alpha=alpha),
        grid=(T // tm,),
        in_specs=[pl.BlockSpec((tm, gw), tok),
                  pl.BlockSpec((tm, gw), tok), pl.BlockSpec((tm, gw), tok), pl.BlockSpec((tm, gw), tok),
                  pl.BlockSpec((4, gw), const),
                  pl.BlockSpec((4 * gw, D), const),
                  pl.BlockSpec((tm, D), tok),
                  pl.BlockSpec((1, D), const), pl.BlockSpec((1, D), const)],
        out_specs=[pl.BlockSpec((tm, D), tok), pl.BlockSpec((tm, D), tok)],
        out_shape=[jax.ShapeDtypeStruct((T, D), F32), jax.ShapeDtypeStruct((T, D), BF16)],
        compiler_params=_params("parallel"),
        name="mix_out",
    )(ya, yb, yc, yd, norm_g, w_out, h, ln_g.reshape(1, D), ln_b.reshape(1, D))


def _xattn_kernel(hb_ref, h_ref, wq_ref, kv_ref, wo_ref, g_ref, b_ref, o_ref, ob_ref, *, alpha):
    width = wq_ref.shape[1]
    chunks = _row_chunks(o_ref.shape[0])
    qs = [jnp.dot(hb_ref[rows, :], wq_ref[...], preferred_element_type=F32) for rows in chunks]
    fs = []
    for q in qs:
        heads = []
        for hd in range(XA_HEADS):
            lo = hd * XA_HEAD_DIM
            k = kv_ref[:, lo:lo + XA_HEAD_DIM]
            v = kv_ref[:, width + lo:width + lo + XA_HEAD_DIM]
            s = _dot_nt(q[:, lo:lo + XA_HEAD_DIM], k) * (XA_HEAD_DIM ** -0.5)
            p = jnp.exp(s - jnp.max(s, -1, keepdims=True))
            p = p * (1.0 / jnp.sum(p, -1, keepdims=True))
            heads.append(_dot(p, v).astype(BF16))
        fs.append(jnp.dot(jnp.concatenate(heads, 1), wo_ref[...], preferred_element_type=F32))
    for rows, f in zip(chunks, fs):
        out = _layer_norm(alpha * h_ref[rows, :] + f, g_ref[...], b_ref[...])
        o_ref[rows, :] = out
        ob_ref[rows, :] = out.astype(BF16)


def _xattn(hb, h, wq, kv, wo, ln_g, ln_b, seq, mem_len, alpha, tm):
    T, D = h.shape
    width = wq.shape[1]
    ns = seq // tm
    tok = lambda i: (i, 0)
    const = lambda i: (0, 0)
    return pl.pallas_call(
        functools.partial(_xattn_kernel, alpha=alpha),
        grid=(T // tm,),
        in_specs=[pl.BlockSpec((tm, D), tok), pl.BlockSpec((tm, D), tok),
                  pl.BlockSpec((D, width), const),
                  pl.BlockSpec((mem_len, 2 * width), lambda i: (i // ns, 0)),
                  pl.BlockSpec((width, D), const),
                  pl.BlockSpec((1, D), const), pl.BlockSpec((1, D), const)],
        out_specs=[pl.BlockSpec((tm, D), tok), pl.BlockSpec((tm, D), tok)],
        out_shape=[jax.ShapeDtypeStruct((T, D), F32), jax.ShapeDtypeStruct((T, D), BF16)],
        compiler_params=_params("parallel"),
        name="cross_attn",
    )(hb, h, wq, kv, wo, ln_g.reshape(1, D), ln_b.reshape(1, D))


def _ffn_kernel(hb_ref, h_ref, wg_ref, wu_ref, wd_ref, g_ref, b_ref, o_ref, *rest, alpha):
    acc_ref = rest[-1]
    j = pl.program_id(1)

    @pl.when(j == 0)
    def _():
        acc_ref[...] = alpha * h_ref[...]

    x = hb_ref[...]
    gate = jnp.dot(x, wg_ref[...], preferred_element_type=F32)
    up = jnp.dot(x, wu_ref[...], preferred_element_type=F32)
    act = (jax.nn.silu(gate) * up).astype(BF16)
    acc_ref[...] += jnp.dot(act, wd_ref[...], preferred_element_type=F32)

    @pl.when(j == pl.num_programs(1) - 1)
    def _():
        out = _layer_norm(acc_ref[...], g_ref[...], b_ref[...])
        o_ref[...] = out
        if len(rest) == 2:
            rest[0][...] = out.astype(BF16)


def _ffn(hb, h, wg, wu, wd, ln_g, ln_b, alpha, tm, th, with_bf16_copy):
    T, D = h.shape
    H = wg.shape[1]
    tok = lambda i, j: (i, 0)
    const = lambda i, j: (0, 0)
    n_out = 2 if with_bf16_copy else 1
    return pl.pallas_call(
        functools.partial(_ffn_kernel, alpha=alpha),
        grid=(T // tm, H // th),
        in_specs=[pl.BlockSpec((tm, D), tok), pl.BlockSpec((tm, D), tok),
                  pl.BlockSpec((D, th), lambda i, j: (0, j)),
                  pl.BlockSpec((D, th), lambda i, j: (0, j)),
                  pl.BlockSpec((th, D), lambda i, j: (j, 0)),
                  pl.BlockSpec((1, D), const), pl.BlockSpec((1, D), const)],
        out_specs=[pl.BlockSpec((tm, D), tok), pl.BlockSpec((tm, D), tok)][:n_out],
        out_shape=[jax.ShapeDtypeStruct((T, D), F32), jax.ShapeDtypeStruct((T, D), BF16)][:n_out],
        scratch_shapes=[pltpu.VMEM((tm, D), F32)],
        compiler_params=_params("parallel", "arbitrary"),
        name="ffn",
    )(hb, h, wg, wu, wd, ln_g.reshape(1, D), ln_b.reshape(1, D))


def _rope_tables(pos):
    inv_freq = ROPE_THETA ** (-jnp.arange(HALF, dtype=F32) / HALF)
    ang = pos.astype(F32)[:, None] * inv_freq[None, :]
    cos = jnp.tile(jnp.cos(ang), (1, LANES // HALF))
    sin = jnp.sin(ang)
    sin = jnp.tile(jnp.concatenate([-sin, sin], 1), (1, LANES // HEAD_DIM))
    return cos, sin


def _overlap_table(seq):
    n_cmp = (seq - CMP_LEN) // CMP_STRIDE + 1
    n_sel = seq // SEL_BLOCK
    ci = np.arange(n_cmp)[:, None] * CMP_STRIDE
    sj = np.arange(n_sel)[None, :] * SEL_BLOCK
    ov = np.clip(np.minimum(ci + CMP_LEN, sj + SEL_BLOCK) - np.maximum(ci, sj), 0, None) / CMP_LEN
    full = np.zeros((LANES, seq // CMP_STRIDE), np.float32)
    full[:n_sel, :n_cmp] = ov.T
    return jnp.asarray(full, dtype=BF16)


def _split_w_in(w_in_stack, layer, width):
    hw = width // 4
    kvw = NSA_KV_HEADS * HEAD_DIM
    n_gate = 3 * (hw // HEAD_DIM)
    o = np.cumsum([0, hw, hw, 6 * kvw, n_gate, 3 * hw, 3 * hw])
    col_scale = np.ones((o[-1],), np.float32)
    for q_lo in (o[1], o[4], o[5]):
        col_scale[q_lo:q_lo + hw] = HEAD_DIM ** -0.5 * math.log2(math.e)
    w_in = _layer_bf16(w_in_stack, layer, jnp.asarray(col_scale))
    u = w_in[:, o[0]:o[1]]
    nq = w_in[:, o[1]:o[2]]
    kv = [w_in[:, o[2] + j * kvw:o[2] + (j + 1) * kvw] for j in range(6)]
    gate = w_in[:, o[3]:o[4]]
    sb = w_in[:, o[4]:o[5]]
    dil = w_in[:, o[5]:o[6]]
    per_group = 3 * NSA_REP
    gates = [jnp.pad(gate[:, g * per_group:(g + 1) * per_group], ((0, 0), (0, LANES - per_group)))
             for g in range(NSA_KV_HEADS)]
    w_rope = jnp.concatenate([nq, kv[2], kv[4], dil[:, :2 * hw]], 1)
    w_plain = jnp.concatenate([sb, dil[:, 2 * hw:], kv[0], kv[1], kv[3], kv[5]] + gates, 1)
    return u, w_rope, w_plain


R_NQ, R_KSLC, R_KWIN, R_DILQ, R_DILK = 0, 4, 5, 6, 10
P_SBQ, P_SBK, P_SBV, P_DILV, P_KCMP, P_VSLC, P_VWIN, P_GATE = 0, 4, 8, 12, 16, 18, 19, 20


def _hybrid_mixer(hb, h, nb, seq, w_in_stack, layer, s5_params, cmp_pe, cmp_w1, cmp_w2, norm_g, w_out,
                  ln_g, ln_b, tables, alpha, tm):
    cos, sin, cos_c, sin_c, overlap = tables
    width = w_out.shape[0]
    n_pairs = width // 4 // LANES
    w_u, w_rope, w_plain = _split_w_in(w_in_stack, layer, width)
    r_arr = _proj(hb, w_rope, min(2 * tm, seq), w_rope.shape[1], seq=seq, rope=(cos, sin))
    p_arr = _proj(hb, w_plain, tm, w_plain.shape[1])
    y_a = _s5(hb, w_u, nb, seq, *s5_params)
    cmp_kv = _compress(p_arr, P_KCMP, nb, seq, cmp_pe, cmp_w1, cmp_w2, cos_c, sin_c)
    y_b = _nsa(r_arr, p_arr, cmp_kv, overlap, nb, seq,
               dict(q=R_NQ, gate=P_GATE, k_slc=R_KSLC, v_slc=P_VSLC, k_win=R_KWIN, v_win=P_VWIN))
    y_c = _stick_breaking(p_arr, nb, seq, P_SBQ, P_SBK, P_SBV, n_pairs)
    y_d = _dilated(r_arr, R_DILQ, r_arr, R_DILK, p_arr, P_DILV, nb, seq, n_pairs)
    return _mix_out(y_a, y_b, y_c, y_d, norm_g, w_out, h, ln_g, ln_b, alpha, tm)


def kernel(x, mem, ln_in_g, ln_in_b, w_in, s5_lambda_re, s5_lambda_im, s5_log_dt, s5_b_re, s5_b_im, s5_c_re, s5_c_im, s5_d, s5_w_glu, s5_b_glu, nsa_cmp_pe, nsa_cmp_w1, nsa_cmp_w2, mix_norm_g, w_out, ln1_g, ln1_b, xa_wq, xa_wkv, xa_wo, ln2_g, ln2_b, ffn_w_gate, ffn_w_up, ffn_w_down, ln3_g, ln3_b):
    nb, seq, d_model = x.shape
    mem_len = mem.shape[1]
    depth = w_in.shape[0]
    alpha = (2 * depth) ** 0.25
    tm = min(512, seq)
    pos = jnp.arange(seq)
    cos, sin = _rope_tables(pos)
    nblk = seq // CMP_STRIDE
    cos_c, sin_c = _rope_tables(jnp.arange(nblk) * CMP_STRIDE + CMP_LEN - 1)
    tables = (cos, sin, cos_c, sin_c, _overlap_table(seq))
    mem_b = mem.reshape(nb * mem_len, d_model).astype(BF16)

    h, hb = _ln_in(x.reshape(nb * seq, d_model), ln_in_g, ln_in_b, tm)
    for l in range(depth):
        s5_params = (s5_lambda_re[l], s5_lambda_im[l], s5_log_dt[l], s5_b_re[l], s5_b_im[l],
                     s5_c_re[l], s5_c_im[l], s5_d[l], s5_w_glu[l], s5_b_glu[l])
        h, hb = _hybrid_mixer(hb, h, nb, seq, w_in, l, s5_params, nsa_cmp_pe[l], nsa_cmp_w1[l],
                              nsa_cmp_w2[l], mix_norm_g[l], _layer_bf16(w_out, l), ln1_g[l], ln1_b[l],
                              tables, alpha, tm)
        kv = _proj(mem_b, _layer_bf16(xa_wkv, l), min(512, nb * mem_len), xa_wkv.shape[2] // 2)
        h, hb = _xattn(hb, h, _layer_bf16(xa_wq, l), kv, _layer_bf16(xa_wo, l), ln2_g[l], ln2_b[l],
                       seq, mem_len, alpha, tm)
        last = l == depth - 1
        outs = _ffn(hb, h, _layer_bf16(ffn_w_gate, l), _layer_bf16(ffn_w_up, l),
                    _layer_bf16(ffn_w_down, l), ln3_g[l], ln3_b[l], alpha, tm, 512, not last)
        h, hb = (outs[0], None) if last else outs
    return h.reshape(nb, seq, d_model)
```

```python
import functools
import math

import numpy as np
import jax
import jax.numpy as jnp
from jax import lax
from jax.experimental import pallas as pl
from jax.experimental.pallas import tpu as pltpu

F32 = jnp.float32
BF16 = jnp.bfloat16

LANES = 128
MXU_WIDTH = 256
CAST_ROWS = 512
VMEM_LIMIT = 56 * 1024 * 1024

HEAD_DIM = 64
HALF = HEAD_DIM // 2
ROPE_THETA = 10000.0
LN_EPS = 1e-5
RMS_EPS = 1e-6
SSM_CH = 16
SSM_STATE = 64
NSA_KV_HEADS = 2
NSA_REP = 4
CMP_LEN = 32
CMP_STRIDE = 16
CMP_HIDDEN = 128
SEL_BLOCK = 64
SEL_TOPN = 8
NSA_WINDOW = 512
DIL_CONFIGS = ((128, 1), (512, 4), (2048, 16))
XA_HEADS = 4
XA_HEAD_DIM = 128
Q_TILE = 128
SB_TILE = 256
SB_PAIRS = 4
NSA_TILE = 256
NEG_BIG = -1e30


def _params(*sem):
    return pltpu.CompilerParams(dimension_semantics=sem, vmem_limit_bytes=VMEM_LIMIT)


def _dot(a, b):
    return jnp.dot(a.astype(BF16), b.astype(BF16), preferred_element_type=F32)


def _dot_nt(a, b):
    return lax.dot_general(a.astype(BF16), b.astype(BF16), (((1,), (1,)), ((), ())),
                           preferred_element_type=F32)


def _layer_norm(x, g, b):
    mu = jnp.mean(x, -1, keepdims=True)
    xc = x - mu
    var = jnp.mean(xc * xc, -1, keepdims=True)
    return xc * lax.rsqrt(var + LN_EPS) * g + b


def _lane_iota(shape):
    return lax.broadcasted_iota(jnp.int32, shape, len(shape) - 1)


def _row_iota(shape):
    return lax.broadcasted_iota(jnp.int32, shape, len(shape) - 2)


def _swap_halves(x):
    return pltpu.roll(x, HEAD_DIM, axis=x.ndim - 1)


def _ln_in_kernel(x_ref, g_ref, b_ref, h_ref, hb_ref):
    y = _layer_norm(x_ref[...], g_ref[...], b_ref[...])
    h_ref[...] = y
    hb_ref[...] = y.astype(BF16)


def _ln_in(x2, g, b, tm):
    T, D = x2.shape
    return pl.pallas_call(
        _ln_in_kernel,
        grid=(T // tm,),
        in_specs=[pl.BlockSpec((tm, D), lambda i: (i, 0)),
                  pl.BlockSpec((1, D), lambda i: (0, 0)),
                  pl.BlockSpec((1, D), lambda i: (0, 0))],
        out_specs=[pl.BlockSpec((tm, D), lambda i: (i, 0)),
                   pl.BlockSpec((tm, D), lambda i: (i, 0))],
        out_shape=[jax.ShapeDtypeStruct((T, D), F32), jax.ShapeDtypeStruct((T, D), BF16)],
        compiler_params=_params("parallel"),
        name="ln_in",
    )(x2, g.reshape(1, D), b.reshape(1, D))


def _cast_kernel(w_ref, o_ref):
    o_ref[...] = w_ref[...].astype(BF16)


def _cast_scaled_kernel(w_ref, s_ref, o_ref):
    o_ref[...] = (w_ref[...] * s_ref[...]).astype(BF16)


def _layer_bf16(w_stack, layer, col_scale=None):
    _, K, N = w_stack.shape
    tk = min(CAST_ROWS, K)
    in_specs = [pl.BlockSpec((None, tk, N), lambda i: (layer, i, 0))]
    args = [w_stack]
    kern = _cast_kernel
    if col_scale is not None:
        in_specs.append(pl.BlockSpec((1, N), lambda i: (0, 0)))
        args.append(col_scale.reshape(1, N))
        kern = _cast_scaled_kernel
    return pl.pallas_call(
        kern, grid=(K // tk,), in_specs=in_specs,
        out_specs=pl.BlockSpec((tk, N), lambda i: (i, 0)),
        out_shape=jax.ShapeDtypeStruct((K, N), BF16),
        compiler_params=_params("parallel"), name="cast_bf16",
    )(*args)


def _row_chunks(n, parts=2):
    step = n // parts
    return [slice(i * step, (i + 1) * step) for i in range(parts)]


def _column_chunks(n):
    return [(lo, min(lo + MXU_WIDTH, n)) for lo in range(0, n, MXU_WIDTH)]


def _proj_kernel(a_ref, w_ref, o_ref):
    a = a_ref[...]
    for lo, hi in _column_chunks(o_ref.shape[1]):
        o_ref[:, lo:hi] = jnp.dot(a, w_ref[:, lo:hi], preferred_element_type=F32)


def _proj_rope_kernel(a_ref, w_ref, cos_ref, sin_ref, o_ref):
    a = a_ref[...]
    cos = cos_ref[...]
    sin = sin_ref[...]
    first = (_lane_iota((1, LANES)) % HEAD_DIM) < HALF
    for lo, hi in _column_chunks(o_ref.shape[1]):
        acc = jnp.dot(a, w_ref[:, lo:hi], preferred_element_type=F32)
        for c in range((hi - lo) // LANES):
            x = acc[:, c * LANES:(c + 1) * LANES]
            partner = jnp.where(first, pltpu.roll(x, LANES - HALF, axis=1), pltpu.roll(x, HALF, axis=1))
            o_ref[:, lo + c * LANES:lo + (c + 1) * LANES] = x * cos + partner * sin


def _proj(a, w, tm, tn, seq=None, rope=None):
    M, K = a.shape
    N = w.shape[1]
    nm, nn = M // tm, N // tn
    in_specs = [pl.BlockSpec((tm, K), lambda j, i: (i, 0)),
                pl.BlockSpec((K, tn), lambda j, i: (0, j))]
    args = [a, w]
    kern = _proj_kernel
    if rope is not None:
        ns = seq // tm
        in_specs += [pl.BlockSpec((tm, LANES), lambda j, i: (i % ns, 0))] * 2
        args += list(rope)
        kern = _proj_rope_kernel
    return pl.pallas_call(
        kern, grid=(nn, nm), in_specs=in_specs,
        out_specs=pl.BlockSpec((tm, tn), lambda j, i: (i, j)),
        out_shape=jax.ShapeDtypeStruct((M, N), F32),
        compiler_params=_params("parallel", "parallel"), name="proj",
    )(*args)


def _s5_kernel(h_ref, wu_ref, bb_ref, cc_ref, are_ref, aim_ref, d_ref, wg_ref, bg_ref, y_ref,
               buf_ref, st_ref, tm_ref, *, nb, ts):
    n_chunks, cw, sw2 = bb_ref.shape
    sw = sw2 // 2
    assert cw == tm_ref.shape[2]

    @pl.when(pl.program_id(0) == 0)
    def _():
        st_ref[...] = jnp.zeros_like(st_ref)

    u_bt = jnp.dot(h_ref[...].reshape(nb * ts, h_ref.shape[2]), wu_ref[...],
                   preferred_element_type=F32)
    for b in range(nb):
        for c in range(n_chunks):
            tm_ref[c, pl.ds(b, ts, stride=nb), :] = u_bt[b * ts:(b + 1) * ts, c * cw:(c + 1) * cw]
    u = jnp.concatenate([tm_ref[c] for c in range(n_chunks)], 1)
    ys = []
    for c in range(n_chunks):
        re0, im0 = c * sw2, c * sw2 + sw
        buf_ref[:, re0:re0 + sw2] = jnp.dot(u[:, c * cw:(c + 1) * cw].astype(BF16), bb_ref[c],
                                            preferred_element_type=F32)
        a_re = jnp.broadcast_to(are_ref[:, c * sw:(c + 1) * sw], (nb, sw))
        a_im = jnp.broadcast_to(aim_ref[:, c * sw:(c + 1) * sw], (nb, sw))

        def step(t, carry, re0=re0, im0=im0, a_re=a_re, a_im=a_im):
            x_re, x_im = carry
            r = pl.multiple_of(t * nb, nb)
            n_re = a_re * x_re - a_im * x_im + buf_ref[pl.ds(r, nb), re0:re0 + sw]
            n_im = a_re * x_im + a_im * x_re + buf_ref[pl.ds(r, nb), im0:im0 + sw]
            buf_ref[pl.ds(r, nb), re0:re0 + sw] = n_re
            buf_ref[pl.ds(r, nb), im0:im0 + sw] = n_im
            return n_re, n_im

        x_re, x_im = lax.fori_loop(0, ts, step, (st_ref[:, re0:re0 + sw], st_ref[:, im0:im0 + sw]),
                                   unroll=True)
        st_ref[:, re0:re0 + sw] = x_re
        st_ref[:, im0:im0 + sw] = x_im
        ys.append(jnp.dot(buf_ref[:, re0:re0 + sw2].astype(BF16), cc_ref[c], preferred_element_type=F32))

    y = jnp.concatenate(ys, 1) + d_ref[...] * u
    g = jax.nn.gelu(y)
    z = jnp.dot(g.astype(BF16), wg_ref[...], preferred_element_type=F32) + bg_ref[...]
    out = g * jax.nn.sigmoid(z)
    for c in range(n_chunks):
        tm_ref[c] = out[:, c * cw:(c + 1) * cw]
    for b in range(nb):
        for c in range(n_chunks):
            y_ref[b, :, c * cw:(c + 1) * cw] = tm_ref[c, pl.ds(b, ts, stride=nb), :]


def _s5(hb, w_u, nb, seq, lam_re, lam_im, log_dt, b_re, b_im, c_re, c_im, d_skip, w_glu, b_glu, ts=64):
    D, W = w_u.shape
    G, P = lam_re.shape
    C = SSM_CH
    lr = jnp.minimum(lam_re, -1e-4)
    li = lam_im
    dt = jnp.exp(log_dt)[:, None]
    mag = jnp.exp(lr * dt)
    a_re = mag * jnp.cos(li * dt)
    a_im = mag * jnp.sin(li * dt)
    den = lr * lr + li * li
    z_re = ((a_re - 1.0) * lr + a_im * li) / den
    z_im = (a_im * lr - (a_re - 1.0) * li) / den
    bb_re = z_re[..., None] * b_re - z_im[..., None] * b_im
    bb_im = z_re[..., None] * b_im + z_im[..., None] * b_re
    gc = LANES // C
    nc = G // gc
    eye = jnp.eye(gc, dtype=F32)

    def block_diag_in(m):
        return jnp.einsum('ngpc,gh->ngchp', m.reshape(nc, gc, P, C), eye).reshape(nc, gc * C, gc * P)

    def block_diag_out(m):
        return jnp.einsum('ngcp,gh->ngphc', m.reshape(nc, gc, C, P), eye).reshape(nc, gc * P, gc * C)

    bb = jnp.concatenate([block_diag_in(bb_re), block_diag_in(bb_im)], 2).astype(BF16)
    cc = jnp.concatenate([block_diag_out(c_re), -block_diag_out(c_im)], 1).astype(BF16)
    ns = G * P
    rows = ts * nb
    kern = functools.partial(_s5_kernel, nb=nb, ts=ts)
    const = lambda i: (0, 0)
    const3 = lambda i: (0, 0, 0)
    y = pl.pallas_call(
        kern,
        grid=(seq // ts,),
        in_specs=[pl.BlockSpec((nb, ts, D), lambda i: (0, i, 0)),
                  pl.BlockSpec((D, W), const),
                  pl.BlockSpec(bb.shape, const3),
                  pl.BlockSpec(cc.shape, const3),
                  pl.BlockSpec((1, ns), const),
                  pl.BlockSpec((1, ns), const),
                  pl.BlockSpec((1, W), const),
                  pl.BlockSpec((W, W), const),
                  pl.BlockSpec((1, W), const)],
        out_specs=pl.BlockSpec((nb, ts, W), lambda i: (0, i, 0)),
        out_shape=jax.ShapeDtypeStruct((nb, seq, W), F32),
        scratch_shapes=[pltpu.VMEM((rows, 2 * ns), F32), pltpu.VMEM((nb, 2 * ns), F32),
                        pltpu.VMEM((nc, rows, LANES), F32)],
        compiler_params=_params("arbitrary"),
        name="s5",
    )(hb.reshape(nb, seq, D), w_u, bb, cc, a_re.reshape(1, ns), a_im.reshape(1, ns),
      d_skip.reshape(1, W), w_glu.astype(BF16), b_glu.reshape(1, W))
    return y.reshape(nb * seq, W)


def _stack_pair(q):
    first = _lane_iota((1, LANES)) < HEAD_DIM
    return jnp.concatenate([jnp.where(first, q, 0.0), jnp.where(first, 0.0, q)], 0)


def _unstack_pair(x):
    t = x.shape[0] // 2
    first = _lane_iota((1, LANES)) < HEAD_DIM
    return jnp.where(first, x[:t], x[t:])


def _sb_kernel(q_ref, kf_ref, vf_ref, o_ref, acc_ref, tail_ref, k_ref, v_ref):
    tq = q_ref.shape[0]
    qi = pl.program_id(2)

    @pl.when(qi == 0)
    def _():
        k_ref[...] = kf_ref[...].astype(BF16)
        v_ref[...] = vf_ref[...].astype(BF16)

    n_pairs = q_ref.shape[1] // LANES
    groups = [slice(p * LANES, (p + 1) * LANES) for p in range(n_pairs)]
    qs = [_stack_pair(q_ref[:, g]).astype(BF16) for g in groups]
    tri = (_row_iota((tq, tq)) > _lane_iota((tq, tq))).astype(BF16)
    diag = _lane_iota((2 * tq, tq)) < (_row_iota((2 * tq, tq)) % tq)

    def tile(kj, diagonal):
        r = pl.multiple_of(kj * tq, tq)
        zs = [_dot_nt(q, k_ref[pl.ds(r, tq), g]) for q, g in zip(qs, groups)]
        stage = []
        for z in zs:
            pos = jnp.maximum(z, 0.0)
            neg = z - pos
            log_term = jnp.log2(1.0 + jnp.exp2(neg - pos))
            sp = pos + log_term
            if diagonal:
                sp = jnp.where(diag, sp, 0.0)
            stage.append((neg - log_term, jnp.sum(sp, -1, keepdims=True), _dot(sp, tri)))
        for p, ((log_beta, row_sum, after), g) in enumerate(zip(stage, groups)):
            if diagonal:
                w = jnp.where(diag, jnp.exp2(log_beta - after), 0.0)
                acc_ref[p] = _dot(w, v_ref[pl.ds(r, tq), g])
                tail_ref[p] = row_sum
            else:
                w = jnp.exp2(log_beta - (after + tail_ref[p]))
                acc_ref[p] += _dot(w, v_ref[pl.ds(r, tq), g])
                tail_ref[p] += row_sum

    tile(qi, True)

    def two_tiles(i, _):
        tile(qi - 1 - 2 * i, False)
        tile(qi - 2 - 2 * i, False)
        return 0

    def last_tile(i, _):
        tile(0, False)
        return 0

    lax.fori_loop(0, qi // 2, two_tiles, 0)
    lax.fori_loop(0, qi % 2, last_tile, 0)
    for p, g in enumerate(groups):
        o_ref[:, g] = _unstack_pair(acc_ref[p])


def _stick_breaking(qkv, nb, seq, q_col, k_col, v_col, n_pairs):
    tq = min(SB_TILE, seq)
    nq = seq // tq
    per = SB_PAIRS
    w = per * LANES
    assert n_pairs % per == 0 and q_col % per == 0 and k_col % per == 0 and v_col % per == 0
    return pl.pallas_call(
        _sb_kernel,
        grid=(nb, n_pairs // per, nq),
        in_specs=[pl.BlockSpec((tq, w), lambda b, p, i: (b * nq + i, q_col // per + p)),
                  pl.BlockSpec((seq, w), lambda b, p, i: (b, k_col // per + p)),
                  pl.BlockSpec((seq, w), lambda b, p, i: (b, v_col // per + p))],
        out_specs=pl.BlockSpec((tq, w), lambda b, p, i: (b * nq + i, p)),
        out_shape=jax.ShapeDtypeStruct((nb * seq, n_pairs * LANES), F32),
        scratch_shapes=[pltpu.VMEM((per, 2 * tq, LANES), F32), pltpu.VMEM((per, 2 * tq, 1), F32),
                        pltpu.VMEM((seq, w), BF16), pltpu.VMEM((seq, w), BF16)],
        compiler_params=_params("parallel", "parallel", "arbitrary"),
        name="stick_breaking",
    )(qkv, qkv, qkv)


def _dil_kernel(q_ref, k_ref, v_ref, o_ref, m_ref, l_ref, a_ref):
    seq = q_ref.shape[0]
    tq = Q_TILE
    first = _lane_iota((1, LANES)) < HEAD_DIM

    q_in = _row_iota((2 * tq, 1)) % tq

    def band_bias(wd, n_tiles):
        if n_tiles > 1:
            lag = q_in - (_lane_iota((1, 2 * tq)) - tq)
            return jnp.where((lag >= 0) & (lag <= wd), 0.0, NEG_BIG)
        return jnp.where(_lane_iota((1, tq)) <= q_in, 0.0, NEG_BIG)

    def tile(c, dil, band, n_tiles, r, i):
        base = r + dil * tq * i

        def rows(ref, start):
            if dil == 1:
                return ref[pl.ds(pl.multiple_of(start, tq), tq), :]
            return ref[pl.ds(start, tq, stride=dil), :]

        qs = _stack_pair(rows(q_ref, base)).astype(BF16)
        if n_tiles > 1:
            prev = r + dil * tq * jnp.maximum(i - 1, 0)
            kk = jnp.concatenate([rows(k_ref, prev), rows(k_ref, base)], 0)
            vv = jnp.concatenate([rows(v_ref, prev), rows(v_ref, base)], 0)
            no_prev = jnp.where((i == 0) & (_lane_iota((1, 2 * tq)) < tq), NEG_BIG, 0.0)
            bias = band + no_prev
        else:
            kk = rows(k_ref, base)
            vv = rows(v_ref, base)
            bias = band
        s = _dot_nt(qs, kk) + bias
        m = jnp.max(s, -1, keepdims=True)
        p = jnp.exp2(s - m)
        l = jnp.sum(p, -1, keepdims=True)
        acc = _dot(p, vv)
        m2 = jnp.where(first, m[:tq], m[tq:])
        l2 = jnp.where(first, l[:tq], l[tq:])
        a2 = jnp.where(first, acc[:tq], acc[tq:])
        if dil == 1:
            sl = pl.ds(pl.multiple_of(base, tq), tq)
        else:
            sl = pl.ds(base, tq, stride=dil)
        m_ref[c, sl, :] = m2
        l_ref[c, sl, :] = l2
        a_ref[c, sl, :] = a2

    for c, (window, dil) in enumerate(DIL_CONFIGS):
        wd = window // dil
        n_tiles = seq // dil // tq

        band = band_bias(wd, n_tiles)

        def per_tile(n, _, c=c, dil=dil, band=band, n_tiles=n_tiles):
            tile(c, dil, band, n_tiles, n // n_tiles, n % n_tiles)
            return 0

        lax.fori_loop(0, dil * n_tiles, per_tile, 0, unroll=16)

    def combine(i, _):
        sl = pl.ds(pl.multiple_of(i * tq, tq), tq)
        m0, m1, m2 = m_ref[0, sl, :], m_ref[1, sl, :], m_ref[2, sl, :]
        mx = jnp.maximum(jnp.maximum(m0, m1), m2)
        e0, e1, e2 = jnp.exp2(m0 - mx), jnp.exp2(m1 - mx), jnp.exp2(m2 - mx)
        num = e0 * a_ref[0, sl, :] + e1 * a_ref[1, sl, :] + e2 * a_ref[2, sl, :]
        den = e0 * l_ref[0, sl, :] + e1 * l_ref[1, sl, :] + e2 * l_ref[2, sl, :]
        o_ref[sl, :] = num / den
        return 0

    lax.fori_loop(0, seq // tq, combine, 0)


def _dilated(q_arr, q_col, k_arr, k_col, v_arr, v_col, nb, seq, n_pairs):
    return pl.pallas_call(
        _dil_kernel,
        grid=(nb, n_pairs),
        in_specs=[pl.BlockSpec((seq, LANES), lambda b, p: (b, q_col + p)),
                  pl.BlockSpec((seq, LANES), lambda b, p: (b, k_col + p)),
                  pl.BlockSpec((seq, LANES), lambda b, p: (b, v_col + p))],
        out_specs=pl.BlockSpec((seq, LANES), lambda b, p: (b, p)),
        out_shape=jax.ShapeDtypeStruct((nb * seq, n_pairs * LANES), F32),
        scratch_shapes=[pltpu.VMEM((3, seq, LANES), F32)] * 3,
        compiler_params=_params("parallel", "parallel"),
        name="dilated",
    )(q_arr, k_arr, v_arr)


def _cmp_kernel(t_ref, pe_ref, w1a_ref, w1b_ref, w2_ref, cos_ref, sin_ref, o_ref):
    nblk = t_ref.shape[0] // CMP_STRIDE
    j = pl.program_id(1)
    out = jnp.zeros((nblk, LANES), F32)
    for g in range(NSA_KV_HEADS):
        p1 = jnp.zeros((nblk, CMP_HIDDEN), F32)
        p2 = jnp.zeros((nblk, CMP_HIDDEN), F32)
        for l in range(CMP_STRIDE):
            x = t_ref[pl.ds(l, nblk, stride=CMP_STRIDE), :]
            p1 = p1 + _dot(x + pe_ref[0, l:l + 1, :], w1a_ref[0, g, l])
            p2 = p2 + _dot(x + pe_ref[0, CMP_STRIDE + l:CMP_STRIDE + l + 1, :], w1b_ref[0, g, l])
        hidden = p1 + pltpu.roll(p2, nblk - 1, axis=0)
        out = out + _dot(jax.nn.gelu(hidden), w2_ref[0, g])
    first = (_lane_iota((1, LANES)) % HEAD_DIM) < HALF
    partner = jnp.where(first, pltpu.roll(out, LANES - HALF, axis=1), pltpu.roll(out, HALF, axis=1))
    roped = out * cos_ref[...] + partner * sin_ref[...]
    o_ref[0] = jnp.where(j == 0, roped, out)


def _compress(p_arr, col0, nb, seq, pe, w1, w2, cos_c, sin_c):
    nblk = seq // CMP_STRIDE
    G = NSA_KV_HEADS
    pe2 = jnp.tile(pe, (1, 1, G))
    w1r = w1.reshape(2, CMP_LEN, HEAD_DIM, CMP_HIDDEN)
    w1e = jnp.zeros((2, G, CMP_LEN, LANES, CMP_HIDDEN), F32)
    w2e = jnp.zeros((2, G, CMP_HIDDEN, LANES), F32)
    for g in range(G):
        w1e = w1e.at[:, g, :, g * HEAD_DIM:(g + 1) * HEAD_DIM, :].set(w1r)
        w2e = w2e.at[:, g, :, g * HEAD_DIM:(g + 1) * HEAD_DIM].set(w2)
    w1e = w1e.astype(BF16)
    w2e = w2e.astype(BF16)
    return pl.pallas_call(
        _cmp_kernel,
        grid=(nb, 2),
        in_specs=[pl.BlockSpec((seq, LANES), lambda b, j: (b, col0 + j)),
                  pl.BlockSpec((1, CMP_LEN, LANES), lambda b, j: (j, 0, 0)),
                  pl.BlockSpec((1, G, CMP_STRIDE, LANES, CMP_HIDDEN), lambda b, j: (j, 0, 0, 0, 0)),
                  pl.BlockSpec((1, G, CMP_STRIDE, LANES, CMP_HIDDEN), lambda b, j: (j, 0, 1, 0, 0)),
                  pl.BlockSpec((1, G, CMP_HIDDEN, LANES), lambda b, j: (j, 0, 0, 0)),
                  pl.BlockSpec((nblk, LANES), lambda b, j: (0, 0)),
                  pl.BlockSpec((nblk, LANES), lambda b, j: (0, 0))],
        out_specs=pl.BlockSpec((1, nblk, LANES), lambda b, j: (j, b, 0)),
        out_shape=jax.ShapeDtypeStruct((2, nb * nblk, LANES), F32),
        compiler_params=_params("parallel", "parallel"),
        name="nsa_compress",
    )(p_arr, pe2, w1e, w1e, w2e, cos_c, sin_c)


def _nsa_kernel(q_ref, gate_ref, kc_ref, vc_ref, ks_ref, vs_ref, kw_ref, vw_ref, ovt_ref, o_ref,
                ms_ref, as_ref, mw_ref, aw_ref):
    tq = q_ref.shape[0]
    R, G = NSA_REP, NSA_KV_HEADS
    seq = ks_ref.shape[0]
    n_sel = seq // SEL_BLOCK
    n_cmp = (seq - CMP_LEN) // CMP_STRIDE + 1
    nblk = kc_ref.shape[1]
    top_n = min(SEL_TOPN, n_sel)
    qi = pl.program_id(1)
    lane = _lane_iota((1, LANES))
    first = lane < HEAD_DIM
    mine = [first, jnp.logical_not(first)]
    t_row = qi * tq + _row_iota((tq, 1))

    qs = []
    for g in range(G):
        parts = []
        for r in range(R):
            h = g * R + r
            x = q_ref[:, (h // 2) * LANES:(h // 2 + 1) * LANES]
            parts.append(jnp.where(mine[g], x if h % 2 == g else _swap_halves(x), 0.0))
        qs.append(jnp.concatenate(parts, 0).astype(BF16))

    c_idx = _lane_iota((tq, nblk))
    visible = (c_idx < n_cmp) & (c_idx * CMP_STRIDE + (CMP_LEN - 1) <= qi * tq + _row_iota((tq, nblk)))
    c_bias = jnp.where(visible, 0.0, NEG_BIG)
    kc, vc = kc_ref[0], vc_ref[0]
    vc_swapped = _swap_halves(vc)
    s_cs = [_dot_nt(q, kc).reshape(R, tq, nblk) + c_bias for q in qs]
    p_cs, o_cs = [], []
    for g, s_c in enumerate(s_cs):
        m_c = jnp.max(s_c, -1, keepdims=True)
        m_c = jnp.where(m_c > 0.5 * NEG_BIG, m_c, 0.0)
        p_c = jnp.exp2(s_c - m_c)
        p_c = p_c * (1.0 / jnp.maximum(jnp.sum(p_c, -1, keepdims=True), 1e-30))
        p_cs.append(p_c)
        o_cs.append(_dot(p_c.reshape(R * tq, nblk), jnp.where(mine[g], vc, vc_swapped)).reshape(R, tq, LANES))

    n_idx = _row_iota((n_sel, tq))
    t_q = qi * tq + _lane_iota((n_sel, tq))
    cur = t_q // SEL_BLOCK
    future = n_idx * SEL_BLOCK > t_q
    forced = (n_idx == 0) | (n_idx == cur) | (n_idx == cur - 1)
    sels = []
    for p_c in p_cs:
        psum = jnp.sum(p_c, 0)
        p_hi = psum.astype(BF16)
        p_lo = (psum - p_hi.astype(F32)).astype(BF16)
        imp = (_dot_nt(ovt_ref[...], p_hi) + _dot_nt(ovt_ref[...], p_lo))[:n_sel]
        imp = jnp.where(forced, 1e9, jnp.where(future, -1.0, imp))
        rank = jnp.zeros((n_sel, tq), F32)
        for m in range(n_sel):
            row = imp[m:m + 1, :]
            earlier = (n_idx > m).astype(F32)
            rank = rank + jnp.where(row > imp, 1.0, jnp.where(row == imp, earlier, 0.0))
        sel_t = jnp.where((rank < top_n) & (imp > -0.5), 1.0, 0.0)
        sel_t = jnp.concatenate([sel_t, jnp.zeros((LANES - n_sel, tq), F32)], 0)
        sels.append(sel_t.T.astype(BF16))

    k_col = _lane_iota((tq, tq))
    blk_row = _row_iota((LANES, tq))
    blk_of_key = _lane_iota((LANES, tq)) // SEL_BLOCK

    def sel_biases(j, causal):
        expand = (blk_row == (j * (tq // SEL_BLOCK) + blk_of_key)).astype(BF16)
        out = []
        for sel in sels:
            chosen = jnp.dot(sel, expand, preferred_element_type=F32)
            bias = (1.0 - chosen) * NEG_BIG
            if causal:
                bias = jnp.where((j * tq + k_col) <= t_row, bias, NEG_BIG)
            out.append(bias)
        return out

    def win_bias(j):
        diff = t_row - (j * tq + k_col)
        return jnp.where((diff >= 0) & (diff < NSA_WINDOW), 0.0, NEG_BIG)

    def tile_of(ref, j):
        return ref[pl.ds(pl.multiple_of(j * tq, tq), tq), :]

    def scores(k_ref, j, biases):
        kt = tile_of(k_ref, j).astype(BF16)
        return [_dot_nt(q, kt).reshape(R, tq, tq) + b for q, b in zip(qs, biases)]

    def update(m_ref, acc_ref, s_groups, v_ref, j):
        v = tile_of(v_ref, j)
        for g, s in enumerate(s_groups):
            m_run = m_ref[g]
            m_new = jnp.maximum(m_run, jnp.max(s, -1, keepdims=True))
            alpha = jnp.exp2(m_run - m_new)
            p = jnp.exp2(s - jnp.concatenate([m_new] * (tq // LANES), -1))
            pv = _dot(p.reshape(R * tq, tq), jnp.where(mine[g], v, 1.0))
            acc_ref[g] = alpha * acc_ref[g] + pv.reshape(R, tq, LANES)
            m_ref[g] = m_new

    def normalise(acc_ref):
        out = []
        for g in range(G):
            acc = acc_ref[g]
            swapped = _swap_halves(acc)
            out.append(jnp.where(mine[g], acc, swapped) / jnp.where(mine[g], swapped, acc))
        return out

    for m_ref, acc_ref in ((ms_ref, as_ref), (mw_ref, aw_ref)):
        m_ref[...] = jnp.full(m_ref.shape, NEG_BIG, F32)
        acc_ref[...] = jnp.zeros(acc_ref.shape, F32)
    j_win = jnp.maximum(qi - NSA_WINDOW // tq, 0)

    def selected_only(j, _):
        update(ms_ref, as_ref, scores(ks_ref, j, sel_biases(j, False)), vs_ref, j)
        return 0

    def selected_and_window(j, _):
        b_w = win_bias(j)
        s_s, s_w = scores(ks_ref, j, sel_biases(j, True)), scores(kw_ref, j, [b_w] * G)
        update(ms_ref, as_ref, s_s, vs_ref, j)
        update(mw_ref, aw_ref, s_w, vw_ref, j)
        return 0

    def paired(step, first_tile, n_tiles):
        lax.fori_loop(0, n_tiles // 2,
                      lambda i, c: step(first_tile + 2 * i + 1, step(first_tile + 2 * i, c)), 0)
        lax.fori_loop(0, n_tiles % 2, lambda i, c: step(first_tile + n_tiles - 1, c), 0)

    paired(selected_only, 0, j_win)
    paired(selected_and_window, j_win, qi + 1 - j_win)
    o_ss, o_ws = normalise(as_ref), normalise(aw_ref)

    gate = jax.nn.sigmoid(gate_ref[...])
    outs = []
    for g in range(G):
        for r in range(R):
            c0 = g * LANES + 3 * r
            outs.append(gate[:, c0:c0 + 1] * o_cs[g][r] + gate[:, c0 + 1:c0 + 2] * o_ss[g][r]
                        + gate[:, c0 + 2:c0 + 3] * o_ws[g][r])
    for pair in range(G * R // 2):
        o_ref[:, pair * LANES:(pair + 1) * LANES] = jnp.where(first, outs[2 * pair], outs[2 * pair + 1])


def _nsa(r_arr, p_arr, cmp_kv, overlap, nb, seq, cols):
    tq = min(NSA_TILE, seq)
    nq = seq // tq
    nblk = seq // CMP_STRIDE
    G = NSA_KV_HEADS
    qw = G * NSA_REP * HEAD_DIM
    gw = G * LANES
    assert (cols['q'] * LANES) % qw == 0 and (cols['gate'] * LANES) % gw == 0
    return pl.pallas_call(
        _nsa_kernel,
        grid=(nb, nq),
        in_specs=[pl.BlockSpec((tq, qw), lambda b, i: (b * nq + i, cols['q'] * LANES // qw)),
                  pl.BlockSpec((tq, gw), lambda b, i: (b * nq + i, cols['gate'] * LANES // gw)),
                  pl.BlockSpec((1, nblk, LANES), lambda b, i: (0, b, 0)),
                  pl.BlockSpec((1, nblk, LANES), lambda b, i: (1, b, 0)),
                  pl.BlockSpec((seq, LANES), lambda b, i: (b, cols['k_slc'])),
                  pl.BlockSpec((seq, LANES), lambda b, i: (b, cols['v_slc'])),
                  pl.BlockSpec((seq, LANES), lambda b, i: (b, cols['k_win'])),
                  pl.BlockSpec((seq, LANES), lambda b, i: (b, cols['v_win'])),
                  pl.BlockSpec((LANES, nblk), lambda b, i: (0, 0))],
        out_specs=pl.BlockSpec((tq, qw), lambda b, i: (b * nq + i, 0)),
        out_shape=jax.ShapeDtypeStruct((nb * seq, qw), F32),
        scratch_shapes=[pltpu.VMEM((G, NSA_REP, tq, LANES), F32)] * 4,
        compiler_params=_params("parallel", "arbitrary"),
        name="nsa",
    )(r_arr, p_arr, cmp_kv, cmp_kv, r_arr, p_arr, r_arr, p_arr, overlap)


def _mix_out_kernel(ya_ref, yb_ref, yc_ref, yd_ref, ng_ref, w_ref, h_ref, g_ref, b_ref,
                    o_ref, ob_ref, *, alpha):
    fs = []
    for rows in _row_chunks(o_ref.shape[0]):
        ys = []
        for n, y_ref in enumerate((ya_ref, yb_ref, yc_ref, yd_ref)):
            y = y_ref[rows, :]
            y = y * lax.rsqrt(jnp.mean(y * y, -1, keepdims=True) + RMS_EPS) * ng_ref[n:n + 1, :]
            ys.append(y.astype(BF16))
        fs.append(jnp.dot(jnp.concatenate(ys, 1), w_ref[...], preferred_element_type=F32))
    for rows, f in zip(_row_chunks(o_ref.shape[0]), fs):
        out = _layer_norm(alpha * h_ref[rows, :] + f, g_ref[...], b_ref[...])
        o_ref[rows, :] = out
        ob_ref[rows, :] = out.astype(BF16)


def _mix_out(ya, yb, yc, yd, norm_g, w_out, h, ln_g, ln_b, alpha, tm):
    T, D = h.shape
    gw = yb.shape[1]
    tok = lambda i: (i, 0)
    const = lambda i: (0, 0)
    return pl.pallas_call(
        functools.partial(_mix_out_kernel, alpha=alpha),
        grid=(T // tm,),
        in_specs=[pl.BlockSpec((tm, gw), tok),
                  pl.BlockSpec((tm, gw), tok), pl.BlockSpec((tm, gw), tok), pl.BlockSpec((tm, gw), tok),
                  pl.BlockSpec((4, gw), const),
                  pl.BlockSpec((4 * gw, D), const),
                  pl.BlockSpec((tm, D), tok),
                  pl.BlockSpec((1, D), const), pl.BlockSpec((1, D), const)],
        out_specs=[pl.BlockSpec((tm, D), tok), pl.BlockSpec((tm, D), tok)],
        out_shape=[jax.ShapeDtypeStruct((T, D), F32), jax.ShapeDtypeStruct((T, D), BF16)],
        compiler_params=_params("parallel"),
        name="mix_out",
    )(ya, yb, yc, yd, norm_g, w_out, h, ln_g.reshape(1, D), ln_b.reshape(1, D))


def _xattn_kernel(hb_ref, h_ref, wq_ref, kv_ref, wo_ref, g_ref, b_ref, o_ref, ob_ref, *, alpha):
    width = wq_ref.shape[1]
    chunks = _row_chunks(o_ref.shape[0])
    qs = [jnp.dot(hb_ref[rows, :], wq_ref[...], preferred_element_type=F32) for rows in chunks]
    fs = []
    for q in qs:
        heads = []
        for hd in range(XA_HEADS):
            lo = hd * XA_HEAD_DIM
            k = kv_ref[:, lo:lo + XA_HEAD_DIM]
            v = kv_ref[:, width + lo:width + lo + XA_HEAD_DIM]
            s = _dot_nt(q[:, lo:lo + XA_HEAD_DIM], k) * (XA_HEAD_DIM ** -0.5)
            p = jnp.exp(s - jnp.max(s, -1, keepdims=True))
            p = p * (1.0 / jnp.sum(p, -1, keepdims=True))
            heads.append(_dot(p, v).astype(BF16))
        fs.append(jnp.dot(jnp.concatenate(heads, 1), wo_ref[...], preferred_element_type=F32))
    for rows, f in zip(chunks, fs):
        out = _layer_norm(alpha * h_ref[rows, :] + f, g_ref[...], b_ref[...])
        o_ref[rows, :] = out
        ob_ref[rows, :] = out.astype(BF16)


def _xattn(hb, h, wq, kv, wo, ln_g, ln_b, seq, mem_len, alpha, tm):
    T, D = h.shape
    width = wq.shape[1]
    ns = seq // tm
    tok = lambda i: (i, 0)
    const = lambda i: (0, 0)
    return pl.pallas_call(
        functools.partial(_xattn_kernel, alpha=alpha),
        grid=(T // tm,),
        in_specs=[pl.BlockSpec((tm, D), tok), pl.BlockSpec((tm, D), tok),
                  pl.BlockSpec((D, width), const),
                  pl.BlockSpec((mem_len, 2 * width), lambda i: (i // ns, 0)),
                  pl.BlockSpec((width, D), const),
                  pl.BlockSpec((1, D), const), pl.BlockSpec((1, D), const)],
        out_specs=[pl.BlockSpec((tm, D), tok), pl.BlockSpec((tm, D), tok)],
        out_shape=[jax.ShapeDtypeStruct((T, D), F32), jax.ShapeDtypeStruct((T, D), BF16)],
        compiler_params=_params("parallel"),
        name="cross_attn",
    )(hb, h, wq, kv, wo, ln_g.reshape(1, D), ln_b.reshape(1, D))


def _ffn_kernel(hb_ref, h_ref, wg_ref, wu_ref, wd_ref, g_ref, b_ref, o_ref, *rest, alpha):
    acc_ref = rest[-1]
    j = pl.program_id(1)

    @pl.when(j == 0)
    def _():
        acc_ref[...] = alpha * h_ref[...]

    x = hb_ref[...]
    gate = jnp.dot(x, wg_ref[...], preferred_element_type=F32)
    up = jnp.dot(x, wu_ref[...], preferred_element_type=F32)
    act = (jax.nn.silu(gate) * up).astype(BF16)
    acc_ref[...] += jnp.dot(act, wd_ref[...], preferred_element_type=F32)

    @pl.when(j == pl.num_programs(1) - 1)
    def _():
        out = _layer_norm(acc_ref[...], g_ref[...], b_ref[...])
        o_ref[...] = out
        if len(rest) == 2:
            rest[0][...] = out.astype(BF16)


def _ffn(hb, h, wg, wu, wd, ln_g, ln_b, alpha, tm, th, with_bf16_copy):
    T, D = h.shape
    H = wg.shape[1]
    tok = lambda i, j: (i, 0)
    const = lambda i, j: (0, 0)
    n_out = 2 if with_bf16_copy else 1
    return pl.pallas_call(
        functools.partial(_ffn_kernel, alpha=alpha),
        grid=(T // tm, H // th),
        in_specs=[pl.BlockSpec((tm, D), tok), pl.BlockSpec((tm, D), tok),
                  pl.BlockSpec((D, th), lambda i, j: (0, j)),
                  pl.BlockSpec((D, th), lambda i, j: (0, j)),
                  pl.BlockSpec((th, D), lambda i, j: (j, 0)),
                  pl.BlockSpec((1, D), const), pl.BlockSpec((1, D), const)],
        out_specs=[pl.BlockSpec((tm, D), tok), pl.BlockSpec((tm, D), tok)][:n_out],
        out_shape=[jax.ShapeDtypeStruct((T, D), F32), jax.ShapeDtypeStruct((T, D), BF16)][:n_out],
        scratch_shapes=[pltpu.VMEM((tm, D), F32)],
        compiler_params=_params("parallel", "arbitrary"),
        name="ffn",
    )(hb, h, wg, wu, wd, ln_g.reshape(1, D), ln_b.reshape(1, D))


def _rope_tables(pos):
    inv_freq = ROPE_THETA ** (-jnp.arange(HALF, dtype=F32) / HALF)
    ang = pos.astype(F32)[:, None] * inv_freq[None, :]
    cos = jnp.tile(jnp.cos(ang), (1, LANES // HALF))
    sin = jnp.sin(ang)
    sin = jnp.tile(jnp.concatenate([-sin, sin], 1), (1, LANES // HEAD_DIM))
    return cos, sin


def _overlap_table(seq):
    n_cmp = (seq - CMP_LEN) // CMP_STRIDE + 1
    n_sel = seq // SEL_BLOCK
    ci = np.arange(n_cmp)[:, None] * CMP_STRIDE
    sj = np.arange(n_sel)[None, :] * SEL_BLOCK
    ov = np.clip(np.minimum(ci + CMP_LEN, sj + SEL_BLOCK) - np.maximum(ci, sj), 0, None) / CMP_LEN
    full = np.zeros((LANES, seq // CMP_STRIDE), np.float32)
    full[:n_sel, :n_cmp] = ov.T
    return jnp.asarray(full, dtype=BF16)


def _split_w_in(w_in_stack, layer, width):
    hw = width // 4
    kvw = NSA_KV_HEADS * HEAD_DIM
    n_gate = 3 * (hw // HEAD_DIM)
    o = np.cumsum([0, hw, hw, 6 * kvw, n_gate, 3 * hw, 3 * hw])
    col_scale = np.ones((o[-1],), np.float32)
    for q_lo in (o[1], o[4], o[5]):
        col_scale[q_lo:q_lo + hw] = HEAD_DIM ** -0.5 * math.log2(math.e)
    w_in = _layer_bf16(w_in_stack, layer, jnp.asarray(col_scale))
    u = w_in[:, o[0]:o[1]]
    nq = w_in[:, o[1]:o[2]]
    kv = [w_in[:, o[2] + j * kvw:o[2] + (j + 1) * kvw] for j in range(6)]
    gate = w_in[:, o[3]:o[4]]
    sb = w_in[:, o[4]:o[5]]
    dil = w_in[:, o[5]:o[6]]
    per_group = 3 * NSA_REP
    gates = [jnp.pad(gate[:, g * per_group:(g + 1) * per_group], ((0, 0), (0, LANES - per_group)))
             for g in range(NSA_KV_HEADS)]
    w_rope = jnp.concatenate([nq, kv[2], kv[4], dil[:, :2 * hw]], 1)
    w_plain = jnp.concatenate([sb, dil[:, 2 * hw:], kv[0], kv[1], kv[3], kv[5]] + gates, 1)
    return u, w_rope, w_plain


R_NQ, R_KSLC, R_KWIN, R_DILQ, R_DILK = 0, 4, 5, 6, 10
P_SBQ, P_SBK, P_SBV, P_DILV, P_KCMP, P_VSLC, P_VWIN, P_GATE = 0, 4, 8, 12, 16, 18, 19, 20


def _hybrid_mixer(hb, h, nb, seq, w_in_stack, layer, s5_params, cmp_pe, cmp_w1, cmp_w2, norm_g, w_out,
                  ln_g, ln_b, tables, alpha, tm):
    cos, sin, cos_c, sin_c, overlap = tables
    width = w_out.shape[0]
    n_pairs = width // 4 // LANES
    w_u, w_rope, w_plain = _split_w_in(w_in_stack, layer, width)
    r_arr = _proj(hb, w_rope, min(2 * tm, seq), w_rope.shape[1], seq=seq, rope=(cos, sin))
    p_arr = _proj(hb, w_plain, tm, w_plain.shape[1])
    y_a = _s5(hb, w_u, nb, seq, *s5_params)
    cmp_kv = _compress(p_arr, P_KCMP, nb, seq, cmp_pe, cmp_w1, cmp_w2, cos_c, sin_c)
    y_b = _nsa(r_arr, p_arr, cmp_kv, overlap, nb, seq,
               dict(q=R_NQ, gate=P_GATE, k_slc=R_KSLC, v_slc=P_VSLC, k_win=R_KWIN, v_win=P_VWIN))
    y_c = _stick_breaking(p_arr, nb, seq, P_SBQ, P_SBK, P_SBV, n_pairs)
    y_d = _dilated(r_arr, R_DILQ, r_arr, R_DILK, p_arr, P_DILV, nb, seq, n_pairs)
    return _mix_out(y_a, y_b, y_c, y_d, norm_g, w_out, h, ln_g, ln_b, alpha, tm)


def kernel(x, mem, ln_in_g, ln_in_b, w_in, s5_lambda_re, s5_lambda_im, s5_log_dt, s5_b_re, s5_b_im, s5_c_re, s5_c_im, s5_d, s5_w_glu, s5_b_glu, nsa_cmp_pe, nsa_cmp_w1, nsa_cmp_w2, mix_norm_g, w_out, ln1_g, ln1_b, xa_wq, xa_wkv, xa_wo, ln2_g, ln2_b, ffn_w_gate, ffn_w_up, ffn_w_down, ln3_g, ln3_b):
    nb, seq, d_model = x.shape
    mem_len = mem.shape[1]
    depth = w_in.shape[0]
    alpha = (2 * depth) ** 0.25
    tm = min(512, seq)
    pos = jnp.arange(seq)
    cos, sin = _rope_tables(pos)
    nblk = seq // CMP_STRIDE
    cos_c, sin_c = _rope_tables(jnp.arange(nblk) * CMP_STRIDE + CMP_LEN - 1)
    tables = (cos, sin, cos_c, sin_c, _overlap_table(seq))
    mem_b = mem.reshape(nb * mem_len, d_model).astype(BF16)

    h, hb = _ln_in(x.reshape(nb * seq, d_model), ln_in_g, ln_in_b, tm)
    for l in range(depth):
        s5_params = (s5_lambda_re[l], s5_lambda_im[l], s5_log_dt[l], s5_b_re[l], s5_b_im[l],
                     s5_c_re[l], s5_c_im[l], s5_d[l], s5_w_glu[l], s5_b_glu[l])
        h, hb = _hybrid_mixer(hb, h, nb, seq, w_in, l, s5_params, nsa_cmp_pe[l], nsa_cmp_w1[l],
                              nsa_cmp_w2[l], mix_norm_g[l], _layer_bf16(w_out, l), ln1_g[l], ln1_b[l],
                              tables, alpha, tm)
        kv = _proj(mem_b, _layer_bf16(xa_wkv, l), min(512, nb * mem_len), xa_wkv.shape[2] // 2)
        h, hb = _xattn(hb, h, _layer_bf16(xa_wq, l), kv, _layer_bf16(xa_wo, l), ln2_g[l], ln2_b[l],
                       seq, mem_len, alpha, tm)
        last = l == depth - 1
        outs = _ffn(hb, h, _layer_bf16(ffn_w_gate, l), _layer_bf16(ffn_w_up, l),
                    _layer_bf16(ffn_w_down, l), ln3_g[l], ln3_b[l], alpha, tm, 512, not last)
        h, hb = (outs[0], None) if last else outs
    return h.reshape(nb, seq, d_model)
```
